```python
import jax, jax.numpy as jnp
from jax import lax
import numpy as np

D_MODEL = 1024
BATCH = 8
SEQ = 2048
DEPTH = 1
DEC_BATCH = 128
DEC_SEQ = 4
PAST_LEN = 2048
PAGE_SIZE = 128

HEAD_DIM = 64
GROUP_WIDTH = D_MODEL // 2
GROUP_HEADS = GROUP_WIDTH // HEAD_DIM
N_KV_HEADS = 2 * GROUP_HEADS
MOBA_BLOCK = 256
MOBA_TOP_K = 3
Q_BLOCK = 128
ROPE_THETA = 10000.0
N_MEM = 256
MEM_HEADS = 4
MEM_HEAD_DIM = D_MODEL // MEM_HEADS
MEM_WIDTH = MEM_HEADS * MEM_HEAD_DIM
N_EXPERTS = 32
TOP_K = 4
D_FF = D_MODEL
SWIGLU_LIMIT = 7.0
SWIGLU_ALPHA = 1.702
EXPERT_BLOCK = 128
DEEPNORM_ALPHA = (2 * DEPTH) ** 0.25
DEEPNORM_BETA = (8 * DEPTH) ** -0.25
LN_EPS = 1e-5
RMS_EPS = 1e-6
ATT_SCALE = HEAD_DIM ** -0.5
F32 = jnp.float32

kernel_name = 'moba_stickbreak_hybrid_decode_step'


def layer_norm(x, g, b):
    xf = x.astype(F32)
    xc = xf - jnp.mean(xf, axis=-1, keepdims=True)
    var = jnp.mean(xc * xc, axis=-1, keepdims=True)
    return (xc * lax.rsqrt(var + LN_EPS) * g + b).astype(x.dtype)


def rms_norm(x, g):
    xf = x.astype(F32)
    return (xf * lax.rsqrt(jnp.mean(xf * xf, axis=-1, keepdims=True) + RMS_EPS) * g).astype(x.dtype)


def rope(x, pos):
    half = HEAD_DIM // 2
    inv_freq = ROPE_THETA ** (-jnp.arange(half, dtype=F32) / half)
    ang = pos.astype(F32)[:, None] * inv_freq[None, :]
    cos = jnp.cos(ang)[None, :, None, :]
    sin = jnp.sin(ang)[None, :, None, :]
    xf = x.astype(F32)
    x1, x2 = xf[..., :half], xf[..., half:]
    return jnp.concatenate([x1 * cos - x2 * sin, x2 * cos + x1 * sin], axis=-1).astype(x.dtype)


def project_groups(x, pos, w_in):
    b, t, _ = x.shape
    parts = jnp.split(x @ w_in, 6, axis=-1)
    qa, ka, va, qb, kb, vb = [p.reshape(b, t, GROUP_HEADS, HEAD_DIM) for p in parts]
    return rope(qa, pos), rope(ka, pos), va, qb, kb, vb


def merge_groups(oa, ob, g_moba, g_sb, w_out):
    b, t = oa.shape[:2]
    y = jnp.concatenate([rms_norm(oa.reshape(b, t, GROUP_WIDTH), g_moba),
                         rms_norm(ob.reshape(b, t, GROUP_WIDTH), g_sb)], axis=-1)
    return y @ w_out


def to_blocks(k, v):
    b, l, h, hd = k.shape
    nb = -(-l // MOBA_BLOCK)
    pad = nb * MOBA_BLOCK - l
    def blk(t):
        return jnp.pad(t, ((0, 0), (0, pad), (0, 0), (0, 0))).reshape(b, nb, MOBA_BLOCK, h, hd).transpose(0, 3, 1, 2, 4)
    kb, vb = blk(k), blk(v)
    kbar = jnp.mean(kb.astype(F32), axis=3)
    return kb, vb, kbar


def moba_core(q, q_pos, kb, vb, kbar):
    h, nb = kb.shape[0], kb.shape[1]
    nq = q.shape[0]
    n_sel = min(MOBA_TOP_K, nb)
    own = q_pos // MOBA_BLOCK
    gate = jnp.einsum('qhd,hnd->qhn', q.astype(F32), kbar)
    gate = jnp.where(jnp.arange(nb)[None, None, :] < own[:, None, None], gate, -jnp.inf)
    _, sel = lax.top_k(gate, n_sel)
    sel_ok = sel < own[:, None, None]
    h_idx = jnp.arange(h)[None, :, None]
    k_sel = kb[h_idx, sel]
    v_sel = vb[h_idx, sel]
    k_own = kb[:, own]
    v_own = vb[:, own]
    s_sel = jnp.einsum('qhd,qhnkd->qhnk', q, k_sel, preferred_element_type=F32) * ATT_SCALE
    s_sel = jnp.where(sel_ok[..., None], s_sel, -jnp.inf).reshape(nq, h, n_sel * MOBA_BLOCK)
    s_own = jnp.einsum('qhd,hqkd->qhk', q, k_own, preferred_element_type=F32) * ATT_SCALE
    key_pos = own[:, None] * MOBA_BLOCK + jnp.arange(MOBA_BLOCK)[None, :]
    s_own = jnp.where((key_pos <= q_pos[:, None])[:, None, :], s_own, -jnp.inf)
    p = jax.nn.softmax(jnp.concatenate([s_sel, s_own], axis=-1), axis=-1).astype(vb.dtype)
    p_sel = p[..., :n_sel * MOBA_BLOCK].reshape(nq, h, n_sel, MOBA_BLOCK)
    p_own = p[..., n_sel * MOBA_BLOCK:]
    return jnp.einsum('qhnk,qhnkd->qhd', p_sel, v_sel) + jnp.einsum('qhk,hqkd->qhd', p_own, v_own)


def moba_prompt(q, k, v):
    b, s, h, hd = q.shape
    nc = s // Q_BLOCK
    kb, vb, kbar = to_blocks(k, v)
    qc = q.reshape(b * nc, Q_BLOCK, h, hd)
    b_idx = jnp.repeat(jnp.arange(b), nc)
    pos = jnp.tile(jnp.arange(s).reshape(nc, Q_BLOCK), (b, 1))
    def step(a):
        qi, bi, pi = a
        return moba_core(qi, pi, kb[bi], vb[bi], kbar[bi])
    return lax.map(step, (qc, b_idx, pos)).reshape(b, s, h, hd)


def stick_breaking(q, q_pos, k, v):
    l = k.shape[1]
    z = jnp.einsum('bqhd,bkhd->bhqk', q, k, preferred_element_type=F32) * ATT_SCALE
    strict = jnp.arange(l)[None, :] < q_pos[:, None]
    log_keep = jnp.where(strict, jax.nn.log_sigmoid(-z), 0.0)
    after = lax.cumsum(log_keep, axis=3, reverse=True) - log_keep
    a = jnp.where(strict, jnp.exp(jax.nn.log_sigmoid(z) + after), 0.0)
    return jnp.einsum('bhqk,bkhd->bqhd', a.astype(v.dtype), v)


def sb_prompt(q, k, v):
    b, s, h, hd = q.shape
    nc = s // Q_BLOCK
    qc = q.reshape(b, nc, Q_BLOCK, h, hd).transpose(1, 0, 2, 3, 4)
    pos = jnp.arange(s).reshape(nc, Q_BLOCK)
    out = lax.map(lambda a: stick_breaking(a[0], a[1], k, v), (qc, pos))
    return out.transpose(1, 0, 2, 3, 4).reshape(b, s, h, hd)


def gather_past(cache, layer, page_table, h0, h1):
    rows = cache[layer, page_table, :, h0:h1]
    return rows.reshape(page_table.shape[0], page_table.shape[1] * PAGE_SIZE, h1 - h0, HEAD_DIM)


def mem_kv(mem, w_mkv):
    b, m, _ = mem.shape
    mk, mv = jnp.split(mem @ w_mkv, 2, axis=-1)
    return mk.reshape(b, m, MEM_HEADS, MEM_HEAD_DIM), mv.reshape(b, m, MEM_HEADS, MEM_HEAD_DIM)


def mem_attend(x, mk, mv, w_mq, w_mo):
    b, t, _ = x.shape
    q = (x @ w_mq).reshape(b, t, MEM_HEADS, MEM_HEAD_DIM)
    s = jnp.einsum('bthd,bmhd->bhtm', q, mk, preferred_element_type=F32) * MEM_HEAD_DIM ** -0.5
    p = jax.nn.softmax(s, axis=-1).astype(mv.dtype)
    o = jnp.einsum('bhtm,bmhd->bthd', p, mv).reshape(b, t, MEM_WIDTH)
    return o @ w_mo


def moe_ffn(h, w_router, b_router, w_gu, b_gu, w_down, b_down):
    t, d = h.shape
    logits = jnp.matmul(h, w_router, preferred_element_type=F32) + b_router
    top_val, top_e = lax.top_k(logits, TOP_K)
    gates = jax.nn.softmax(top_val, axis=-1)
    n_assign = t * TOP_K
    flat_e = top_e.reshape(n_assign)
    order = jnp.argsort(flat_e)
    e_sorted = flat_e[order]
    tok_sorted = (order // TOP_K).astype(jnp.int32)
    gate_sorted = gates.reshape(n_assign)[order]
    counts = jnp.bincount(flat_e, length=N_EXPERTS)
    padded = (counts + EXPERT_BLOCK - 1) // EXPERT_BLOCK * EXPERT_BLOCK
    pad_end = jnp.cumsum(padded)
    pad_start = pad_end - padded
    start = jnp.cumsum(counts) - counts
    dest = pad_start[e_sorted] + jnp.arange(n_assign) - start[e_sorted]
    n_blocks = -(-(n_assign + N_EXPERTS * (EXPERT_BLOCK - 1)) // EXPERT_BLOCK)
    n_rows = n_blocks * EXPERT_BLOCK
    row_tok = jnp.full((n_rows,), t, jnp.int32).at[dest].set(tok_sorted)
    row_gate = jnp.zeros((n_rows,), F32).at[dest].set(gate_sorted)
    block_e = jnp.minimum(jnp.searchsorted(pad_end, jnp.arange(n_blocks) * EXPERT_BLOCK, side='right'), N_EXPERTS - 1)
    h_pad = jnp.concatenate([h, jnp.zeros((1, d), h.dtype)], axis=0)
    xb = h_pad[row_tok].reshape(n_blocks, EXPERT_BLOCK, d)
    def expert_block(a):
        xi, e = a
        gate, up = jnp.split(xi @ w_gu[e] + b_gu[e], 2, axis=-1)
        gate = jnp.minimum(gate, SWIGLU_LIMIT)
        up = jnp.clip(up, -SWIGLU_LIMIT, SWIGLU_LIMIT)
        act = (up + 1.0) * gate * jax.nn.sigmoid(SWIGLU_ALPHA * gate)
        return act @ w_down[e] + b_down[e]
    yb = lax.map(expert_block, (xb, block_e)).reshape(n_rows, d)
    out = jnp.zeros((t + 1, d), F32).at[row_tok].add(yb.astype(F32) * row_gate[:, None])
    return out[:t].astype(h.dtype)


def setup_inputs(seed: int = 0) -> dict:
    key = jax.random.key(seed)
    ks = jax.random.split(key, 32)
    def nrm(i, shape, scale):
        return jax.random.normal(ks[i], shape, F32) * scale
    n_pages = PAST_LEN // PAGE_SIZE
    n_used = DEC_BATCH * n_pages
    n_phys = n_used + max(1, n_used // 4)
    page_table = jax.random.permutation(ks[0], n_phys)[:n_used].reshape(DEC_BATCH, n_pages).astype(jnp.int32)
    L, D, GW, E, F = DEPTH, D_MODEL, GROUP_WIDTH, N_EXPERTS, D_FF
    return {
        'x_prompt': nrm(1, (BATCH, SEQ, D), 1.0),
        'x_sample': nrm(2, (DEC_BATCH, DEC_SEQ, D), 1.0),
        'mem_prompt': nrm(3, (BATCH, N_MEM, D), 1.0),
        'cache_k': nrm(4, (L, n_phys, PAGE_SIZE, N_KV_HEADS, HEAD_DIM), 1.0),
        'cache_v': nrm(5, (L, n_phys, PAGE_SIZE, N_KV_HEADS, HEAD_DIM), 1.0),
        'cache_mem_k': nrm(6, (L, DEC_BATCH, N_MEM, MEM_HEADS, MEM_HEAD_DIM), 1.0),
        'cache_mem_v': nrm(7, (L, DEC_BATCH, N_MEM, MEM_HEADS, MEM_HEAD_DIM), 1.0),
        'page_table': page_table,
        'w_in': nrm(8, (L, D, 6 * GW), D ** -0.5),
        'g_moba': 1.0 + nrm(9, (L, GW), 0.02),
        'g_sb': 1.0 + nrm(10, (L, GW), 0.02),
        'w_out': nrm(11, (L, 2 * GW, D), (2 * GW) ** -0.5 * DEEPNORM_BETA),
        'ln1_g': 1.0 + nrm(12, (L, D), 0.02),
        'ln1_b': nrm(13, (L, D), 0.02),
        'w_mq': nrm(14, (L, D, MEM_WIDTH), D ** -0.5),
        'w_mkv': nrm(15, (L, D, 2 * MEM_WIDTH), D ** -0.5),
        'w_mo': nrm(16, (L, MEM_WIDTH, D), MEM_WIDTH ** -0.5 * DEEPNORM_BETA),
        'ln2_g': 1.0 + nrm(17, (L, D), 0.02),
        'ln2_b': nrm(18, (L, D), 0.02),
        'w_router': nrm(19, (L, D, E), D ** -0.5),
        'b_router': nrm(20, (L, E), 0.01),
        'w_gu': nrm(21, (L, E, D, 2 * F), D ** -0.5),
        'b_gu': nrm(22, (L, E, 2 * F), 0.02),
        'w_down': nrm(23, (L, E, F, D), F ** -0.5 * DEEPNORM_BETA),
        'b_down': nrm(24, (L, E, D), 0.02),
        'ln3_g': 1.0 + nrm(25, (L, D), 0.02),
        'ln3_b': nrm(26, (L, D), 0.02),
    }


def reference(x_prompt, x_sample, mem_prompt, cache_k, cache_v, cache_mem_k, cache_mem_v, page_table,
              w_in, g_moba, g_sb, w_out, ln1_g, ln1_b,
              w_mq, w_mkv, w_mo, ln2_g, ln2_b,
              w_router, b_router, w_gu, b_gu, w_down, b_down, ln3_g, ln3_b):
    n_seq, seq_len, d = x_prompt.shape
    n_dec, dec_len, _ = x_sample.shape
    past_len = page_table.shape[1] * PAGE_SIZE
    pos_p = jnp.arange(seq_len)
    pos_s = past_len + jnp.arange(dec_len)
    n_tok_p = n_seq * seq_len
    xp, xs = x_prompt, x_sample
    new_kp, new_vp, new_ks, new_vs, mem_kp, mem_vp = [], [], [], [], [], []
    for l in range(DEPTH):
        qa, ka, va, qb, kb, vb = project_groups(xp, pos_p, w_in[l])
        mix_p = merge_groups(moba_prompt(qa, ka, va), sb_prompt(qb, kb, vb), g_moba[l], g_sb[l], w_out[l])
        new_kp.append(jnp.concatenate([ka, kb], axis=2))
        new_vp.append(jnp.concatenate([va, vb], axis=2))
        qa_s, ka_s, va_s, qb_s, kb_s, vb_s = project_groups(xs, pos_s, w_in[l])
        ka_all = jnp.concatenate([gather_past(cache_k, l, page_table, 0, GROUP_HEADS), ka_s], axis=1)
        va_all = jnp.concatenate([gather_past(cache_v, l, page_table, 0, GROUP_HEADS), va_s], axis=1)
        kb_all = jnp.concatenate([gather_past(cache_k, l, page_table, GROUP_HEADS, N_KV_HEADS), kb_s], axis=1)
        vb_all = jnp.concatenate([gather_past(cache_v, l, page_table, GROUP_HEADS, N_KV_HEADS), vb_s], axis=1)
        kblk, vblk, kbar = to_blocks(ka_all, va_all)
        oa_s = lax.map(lambda a: moba_core(a[0], pos_s, a[1], a[2], a[3]), (qa_s, kblk, vblk, kbar))
        ob_s = stick_breaking(qb_s, pos_s, kb_all, vb_all)
        mix_s = merge_groups(oa_s, ob_s, g_moba[l], g_sb[l], w_out[l])
        new_ks.append(jnp.concatenate([ka_s, kb_s], axis=2))
        new_vs.append(jnp.concatenate([va_s, vb_s], axis=2))
        xp = layer_norm(DEEPNORM_ALPHA * xp + mix_p, ln1_g[l], ln1_b[l])
        xs = layer_norm(DEEPNORM_ALPHA * xs + mix_s, ln1_g[l], ln1_b[l])
        mk, mv = mem_kv(mem_prompt, w_mkv[l])
        mem_kp.append(mk)
        mem_vp.append(mv)
        xp = layer_norm(DEEPNORM_ALPHA * xp + mem_attend(xp, mk, mv, w_mq[l], w_mo[l]), ln2_g[l], ln2_b[l])
        xs = layer_norm(DEEPNORM_ALPHA * xs + mem_attend(xs, cache_mem_k[l], cache_mem_v[l], w_mq[l], w_mo[l]), ln2_g[l], ln2_b[l])
        tok = jnp.concatenate([xp.reshape(n_tok_p, d), xs.reshape(n_dec * dec_len, d)], axis=0)
        ffn = moe_ffn(tok, w_router[l], b_router[l], w_gu[l], b_gu[l], w_down[l], b_down[l])
        tok = layer_norm(DEEPNORM_ALPHA * tok + ffn, ln3_g[l], ln3_b[l])
        xp = tok[:n_tok_p].reshape(n_seq, seq_len, d)
        xs = tok[n_tok_p:].reshape(n_dec, dec_len, d)
    return (xp, xs, jnp.stack(new_kp), jnp.stack(new_vp), jnp.stack(new_ks), jnp.stack(new_vs), jnp.stack(mem_kp), jnp.stack(mem_vp))
```

```python
import functools

import jax
import jax.numpy as jnp
from jax import lax
from jax.experimental import pallas as pl
from jax.experimental.pallas import tpu as pltpu

F32 = jnp.float32
BF16 = jnp.bfloat16
HIGHEST = lax.Precision.HIGHEST

HEAD_DIM = 64
MOBA_BLOCK = 256
MOBA_TOP_K = 3
Q_BLOCK = 128
ROPE_THETA = 10000.0
MEM_HEADS = 4
TOP_K = 4
SWIGLU_LIMIT = 7.0
SWIGLU_ALPHA = 1.702
LN_EPS = 1e-5
RMS_EPS = 1e-6
ATT_SCALE = HEAD_DIM ** -0.5
NEG_INF = float("-inf")

PROJ_ROWS = MOBA_BLOCK
SB_KEYS = 128
TOKEN_ROWS = 512
EXPERT_ROWS = 256
COMBINE_ROWS = 128
NEW_PAD = 16
MEM_GROUP = 4
VMEM_LIMIT = 56 * 1024 * 1024

_NT = (((1,), (1,)), ((), ()))


def _cparams(n_axes):
    return pltpu.CompilerParams(dimension_semantics=("arbitrary",) * n_axes,
                                vmem_limit_bytes=VMEM_LIMIT)


def _row_tile(n_rows):
    rows = TOKEN_ROWS
    while n_rows % rows:
        rows //= 2
    assert rows >= 8, n_rows
    return rows


def _idiv(x, n):
    return x >> (n.bit_length() - 1) if n & (n - 1) == 0 else x // n


def _imod(x, n):
    return x & (n - 1) if n & (n - 1) == 0 else x % n


def _layer_norm(x, g, b):
    mu = jnp.mean(x, axis=-1, keepdims=True)
    xc = x - mu
    var = jnp.mean(xc * xc, axis=-1, keepdims=True)
    return xc * lax.rsqrt(var + LN_EPS) * g + b


def _rms_norm(x, g):
    return x * lax.rsqrt(jnp.mean(x * x, axis=-1, keepdims=True) + RMS_EPS) * g


def _log_sigmoid_pair(z):
    t = jnp.log1p(jnp.exp(-jnp.abs(z)))
    return -(jnp.maximum(z, 0.0) + t), jnp.minimum(z, 0.0) - t


def _suffix_sum(x, upper):
    hi = x.astype(BF16)
    lo = (x - hi.astype(F32)).astype(BF16)
    return (jnp.dot(hi, upper, preferred_element_type=F32)
            + jnp.dot(lo, upper, preferred_element_type=F32))


def _strict_lower(n):
    r = lax.broadcasted_iota(jnp.int32, (n, n), 0)
    c = lax.broadcasted_iota(jnp.int32, (n, n), 1)
    return jnp.where(r > c, 1.0, 0.0).astype(BF16)


def _top_block_bias(gates, n_valid):
    nb = gates.shape[1]
    n_iota = lax.broadcasted_iota(jnp.int32, gates.shape, 1)
    valid = n_iota < n_valid
    g = jnp.where(valid, gates, NEG_INF)
    cnt = jnp.zeros(gates.shape, jnp.int32)
    for m in range(nb):
        gm = g[:, m:m + 1]
        beats = jnp.where(gm > g, 1, jnp.where(gm == g, jnp.where(n_iota > m, 1, 0), 0))
        cnt = cnt + beats
    sel = jnp.where(valid, jnp.where(cnt < MOBA_TOP_K, 1, 0), 0)
    return jnp.where(sel == 1, 0.0, NEG_INF)


def _qkv_kernel(x_ref, w_ref, cos_ref, sin_ref, q_ref, k_ref, v_ref, k16_ref, v16_ref, *rest, gw):
    x16 = x_ref[...].astype(BF16)
    cos = cos_ref[...]
    sin = sin_ref[...]
    lane = lax.broadcasted_iota(jnp.int32, cos.shape, 1)
    first_half = (lane & (HEAD_DIM - 1)) < HEAD_DIM // 2

    def proj(g):
        return jnp.dot(x16, w_ref[:, g * gw:(g + 1) * gw], preferred_element_type=F32)

    def rope(t):
        rot = jnp.where(first_half, pltpu.roll(t, gw - HEAD_DIM // 2, 1), pltpu.roll(t, HEAD_DIM // 2, 1))
        return t * cos + rot * sin

    q_ref[:, :gw] = rope(proj(0))
    ka = rope(proj(1))
    k_ref[:, :gw] = ka
    k16_ref[:, :gw] = ka.astype(BF16)
    if rest:
        rest[0][0] = jnp.mean(ka, axis=0, keepdims=True)
    va = proj(2)
    v_ref[:, :gw] = va
    v16_ref[:, :gw] = va.astype(BF16)
    q_ref[:, gw:] = proj(3)
    kb = proj(4)
    k_ref[:, gw:] = kb
    k16_ref[:, gw:] = kb.astype(BF16)
    vb = proj(5)
    v_ref[:, gw:] = vb
    v16_ref[:, gw:] = vb.astype(BF16)


def _qkv_project(x, w16, cos, sin, rows, with_kbar):
    t, d = x.shape
    gw = w16.shape[1] // 6
    n_pos = cos.shape[0] // rows
    row_spec = lambda width: pl.BlockSpec((rows, width), lambda i: (i, 0))
    tab_spec = pl.BlockSpec((rows, gw), lambda i: (i % n_pos, 0))
    out_shape = [jax.ShapeDtypeStruct((t, 2 * gw), F32)] * 3 + [jax.ShapeDtypeStruct((t, 2 * gw), BF16)] * 2
    out_specs = [row_spec(2 * gw)] * 5
    if with_kbar:
        out_shape.append(jax.ShapeDtypeStruct((t // rows, 1, gw), F32))
        out_specs.append(pl.BlockSpec((1, 1, gw), lambda i: (i, 0, 0)))
    return pl.pallas_call(
        functools.partial(_qkv_kernel, gw=gw),
        grid=(t // rows,),
        in_specs=[row_spec(d), pl.BlockSpec((d, 6 * gw), lambda i: (0, 0)), tab_spec, tab_spec],
        out_specs=out_specs,
        out_shape=out_shape,
        compiler_params=_cparams(1),
        name="qkv_rope",
    )(x, w16, cos, sin)


def _moba_prompt_kernel(q_ref, k_ref, v_ref, kbar_ref, o_ref):
    c = pl.program_id(2)
    own = c // (MOBA_BLOCK // Q_BLOCK)
    tq = q_ref.shape[0]
    heads = q_ref.shape[1] // HEAD_DIM
    row = lax.broadcasted_iota(jnp.int32, (tq, MOBA_BLOCK), 0)
    col = lax.broadcasted_iota(jnp.int32, (tq, MOBA_BLOCK), 1)
    causal = own * MOBA_BLOCK + col <= c * tq + row
    for h in range(heads):
        hs = slice(h * HEAD_DIM, (h + 1) * HEAD_DIM)
        q = q_ref[:, hs]
        gates = lax.dot_general(q, kbar_ref[0, :, hs], _NT, precision=HIGHEST, preferred_element_type=F32)
        bias = _top_block_bias(gates, own)
        n_iota = lax.broadcasted_iota(jnp.int32, bias.shape, 1)
        q16 = (q * ATT_SCALE).astype(BF16)

        def scores(j):
            start = pl.multiple_of(j * MOBA_BLOCK, MOBA_BLOCK)
            kb = k_ref[pl.ds(start, MOBA_BLOCK), hs]
            vb = v_ref[pl.ds(start, MOBA_BLOCK), hs]
            return lax.dot_general(q16, kb, _NT, preferred_element_type=F32), vb

        s, vb = scores(own)
        s = jnp.where(causal, s, NEG_INF)
        m = jnp.max(s, axis=1, keepdims=True)
        p = jnp.exp(s - m)
        l = jnp.sum(p, axis=1, keepdims=True)
        acc = jnp.dot(p.astype(BF16), vb, preferred_element_type=F32)

        def body(j, carry):
            m, l, acc = carry
            s, vb = scores(j)
            s = s + jnp.sum(jnp.where(n_iota == j, bias, 0.0), axis=1, keepdims=True)
            m_new = jnp.maximum(m, jnp.max(s, axis=1, keepdims=True))
            alpha = jnp.exp(m - m_new)
            p = jnp.exp(s - m_new)
            l = alpha * l + jnp.sum(p, axis=1, keepdims=True)
            acc = alpha * acc + jnp.dot(p.astype(BF16), vb, preferred_element_type=F32)
            return m_new, l, acc

        m, l, acc = lax.fori_loop(0, own, body, (m, l, acc))
        o_ref[:, hs] = acc / l


def _moba_prompt(q, k16, v16, kbar, n_seq, seq_len, gw):
    pair = 2 * HEAD_DIM
    n_pairs = gw // pair
    nc = seq_len // Q_BLOCK
    nb = seq_len // MOBA_BLOCK
    kv_spec = pl.BlockSpec((seq_len, pair), lambda b, hp, c: (b, hp))
    return pl.pallas_call(
        _moba_prompt_kernel,
        grid=(n_seq, n_pairs, nc),
        in_specs=[pl.BlockSpec((Q_BLOCK, pair), lambda b, hp, c: (b * nc + c, hp)),
                  kv_spec, kv_spec,
                  pl.BlockSpec((1, nb, pair), lambda b, hp, c: (b, 0, hp))],
        out_specs=pl.BlockSpec((Q_BLOCK, pair), lambda b, hp, c: (b * nc + c, hp)),
        out_shape=jax.ShapeDtypeStruct((n_seq * seq_len, gw), F32),
        compiler_params=_cparams(3),
        name="moba_prompt",
    )(q, k16, v16, kbar)


def _sb_prompt_kernel(q_ref, k_ref, v_ref, o_ref):
    c = pl.program_id(2)
    tq = q_ref.shape[0]
    heads = q_ref.shape[1] // HEAD_DIM
    upper = _strict_lower(SB_KEYS)
    row = lax.broadcasted_iota(jnp.int32, (tq, SB_KEYS), 0)
    col = lax.broadcasted_iota(jnp.int32, (tq, SB_KEYS), 1)
    strict = col < row
    for h in range(heads):
        hs = slice(h * HEAD_DIM, (h + 1) * HEAD_DIM)
        q16 = (q_ref[:, hs] * ATT_SCALE).astype(BF16)

        def tile(j, r, acc, mask):
            start = pl.multiple_of(j * SB_KEYS, SB_KEYS)
            kb = k_ref[pl.ds(start, SB_KEYS), hs]
            vb = v_ref[pl.ds(start, SB_KEYS), hs]
            z = lax.dot_general(q16, kb, _NT, preferred_element_type=F32)
            lk, ls = _log_sigmoid_pair(z)
            if mask is not None:
                lk = jnp.where(mask, lk, 0.0)
            a = jnp.exp(ls + _suffix_sum(lk, upper) + r)
            if mask is not None:
                a = jnp.where(mask, a, 0.0)
            acc = acc + jnp.dot(a.astype(BF16), vb, preferred_element_type=F32)
            return r + jnp.sum(lk, axis=1, keepdims=True), acc

        r, acc = tile(c, jnp.zeros((tq, 1), F32), jnp.zeros((tq, HEAD_DIM), F32), strict)

        def body(i, carry):
            return tile(c - 1 - i, carry[0], carry[1], None)

        r, acc = lax.fori_loop(0, c, body, (r, acc))
        o_ref[:, hs] = acc


def _sb_prompt(q, k16, v16, n_seq, seq_len, gw):
    pair = 2 * HEAD_DIM
    n_pairs = gw // pair
    nc = seq_len // SB_KEYS
    kv_spec = pl.BlockSpec((seq_len, pair), lambda b, hp, c: (b, n_pairs + hp))
    return pl.pallas_call(
        _sb_prompt_kernel,
        grid=(n_seq, n_pairs, nc),
        in_specs=[pl.BlockSpec((SB_KEYS, pair), lambda b, hp, c: (b * nc + c, n_pairs + hp)),
                  kv_spec, kv_spec],
        out_specs=pl.BlockSpec((SB_KEYS, pair), lambda b, hp, c: (b * nc + c, hp)),
        out_shape=jax.ShapeDtypeStruct((n_seq * seq_len, gw), F32),
        compiler_params=_cparams(3),
        name="sb_prompt",
    )(q, k16, v16)


def _decode_kernel(pt_ref, qbd_ref, knew_ref, vnew_ref, kc_ref, vc_ref, oa_ref, ob_ref,
                   q16_sc, kbar_sc, m_sc, l_sc, o_sc, r_sc, accb_sc, *, n_pages, gw, nq):
    del pt_ref
    p = pl.program_id(1)
    lp = n_pages - 1 - p
    nr = q16_sc.shape[0] // 2
    page = kc_ref.shape[1]
    pages_per_block = MOBA_BLOCK // page
    n_blocks = n_pages // pages_per_block
    lane_w = m_sc.shape[2]
    qi = _imod(lax.broadcasted_iota(jnp.int32, (nr, NEW_PAD), 0), nq)
    cj = lax.broadcasted_iota(jnp.int32, (nr, NEW_PAD), 1)

    @pl.when(p == 0)
    def _init():
        q16_sc[...] = (qbd_ref[0] * ATT_SCALE).astype(BF16)
        kbar_sc[...] = jnp.zeros(kbar_sc.shape, F32)
        m_sc[...] = jnp.full(m_sc.shape, NEG_INF, F32)
        l_sc[...] = jnp.zeros(l_sc.shape, F32)
        o_sc[...] = jnp.zeros(o_sc.shape, F32)
        kn = knew_ref[0][:, gw:].astype(BF16)
        vn = vnew_ref[0][:, gw:].astype(BF16)
        z = lax.dot_general(q16_sc[nr:, gw:], kn, _NT, preferred_element_type=F32)
        mask = jnp.where(cj < qi, jnp.where(cj < nq, 1, 0), 0) == 1
        lk, ls = _log_sigmoid_pair(z)
        lk = jnp.where(mask, lk, 0.0)
        a = jnp.where(mask, jnp.exp(ls + _suffix_sum(lk, _strict_lower(NEW_PAD))), 0.0)
        accb_sc[...] = jnp.dot(a.astype(BF16), vn, preferred_element_type=F32)
        r_sc[...] = jnp.broadcast_to(jnp.sum(lk, axis=1, keepdims=True), r_sc.shape)

    kf = kc_ref[0]
    k16 = kf.astype(BF16)
    v16 = vc_ref[0].astype(BF16)

    n = lp // pages_per_block
    s = lax.dot_general(q16_sc[:nr, :gw], k16[:, :gw], _NT, preferred_element_type=F32)
    m_old = m_sc[n][:, :1]
    m_new = jnp.maximum(m_old, jnp.max(s, axis=1, keepdims=True))
    alpha = jnp.exp(m_old - m_new)
    pr = jnp.exp(s - m_new)
    l_sc[n] = jnp.broadcast_to(alpha * l_sc[n][:, :1] + jnp.sum(pr, axis=1, keepdims=True), (nr, lane_w))
    m_sc[n] = jnp.broadcast_to(m_new, (nr, lane_w))
    o_sc[n] = alpha * o_sc[n] + jnp.dot(pr.astype(BF16), v16[:, :gw], preferred_element_type=F32)
    kbar_sc[pl.ds(n, 1), :] = kbar_sc[pl.ds(n, 1), :] + jnp.sum(kf[:, :gw], axis=0, keepdims=True)

    z = lax.dot_general(q16_sc[nr:, gw:], k16[:, gw:], _NT, preferred_element_type=F32)
    lk, ls = _log_sigmoid_pair(z)
    r = r_sc[:, :1]
    a = jnp.exp(ls + _suffix_sum(lk, _strict_lower(page)) + r)
    accb_sc[...] = accb_sc[...] + jnp.dot(a.astype(BF16), v16[:, gw:], preferred_element_type=F32)
    r_sc[...] = jnp.broadcast_to(r + jnp.sum(lk, axis=1, keepdims=True), r_sc.shape)

    @pl.when(p == n_pages - 1)
    def _finish():
        qf = qbd_ref[0][:nr, :gw]
        kbar = kbar_sc[...] * (1.0 / MOBA_BLOCK)
        gates = lax.dot_general(qf, kbar, _NT, precision=HIGHEST, preferred_element_type=F32)
        bias = _top_block_bias(gates, n_blocks)
        kn = knew_ref[0][:, :gw].astype(BF16)
        vn = vnew_ref[0][:, :gw].astype(BF16)
        sn = lax.dot_general(q16_sc[:nr, :gw], kn, _NT, preferred_element_type=F32)
        mask = jnp.where(cj <= qi, jnp.where(cj < nq, 1, 0), 0) == 1
        sn = jnp.where(mask, sn, NEG_INF)
        m_own = jnp.max(sn, axis=1, keepdims=True)
        pn = jnp.exp(sn - m_own)
        l_own = jnp.sum(pn, axis=1, keepdims=True)
        o_own = jnp.dot(pn.astype(BF16), vn, preferred_element_type=F32)
        m_all = m_own
        for b in range(n_blocks):
            m_all = jnp.maximum(m_all, m_sc[b][:, :1] + bias[:, b:b + 1])
        w_own = jnp.exp(m_own - m_all)
        num = w_own * o_own
        den = w_own * l_own
        for b in range(n_blocks):
            w = jnp.exp(m_sc[b][:, :1] + bias[:, b:b + 1] - m_all)
            num = num + w * o_sc[b]
            den = den + w * l_sc[b][:, :1]
        outa = num / den
        rr = lax.broadcasted_iota(jnp.int32, (nr, gw), 0)
        cc = lax.broadcasted_iota(jnp.int32, (nr, gw), 1)
        diag = _idiv(rr, nq) == _idiv(cc, HEAD_DIM)
        si = lax.broadcasted_iota(jnp.int32, (oa_ref.shape[1], nr), 0)
        sr = lax.broadcasted_iota(jnp.int32, (oa_ref.shape[1], nr), 1)
        pick = jnp.where(_imod(sr, nq) == si, 1.0, 0.0)
        oa_ref[0] = jnp.dot(pick, jnp.where(diag, outa, 0.0), precision=HIGHEST, preferred_element_type=F32)
        ob_ref[0] = jnp.dot(pick, jnp.where(diag, accb_sc[...], 0.0), precision=HIGHEST,
                            preferred_element_type=F32)


def _decode_attention(page_table, qbd, knew, vnew, cache_k, cache_v, gw, nq):
    n_dec, n_pages = page_table.shape
    rows2 = qbd.shape[1]
    nr = rows2 // 2
    width = 2 * gw
    page = cache_k.shape[1]
    n_blocks = n_pages * page // MOBA_BLOCK
    out_rows = 8
    per_seq = lambda shape: pl.BlockSpec(shape, lambda b, p, pt: (b, 0, 0))
    page_spec = pl.BlockSpec((1, page, width), lambda b, p, pt: (pt[b, n_pages - 1 - p], 0, 0))
    grid_spec = pltpu.PrefetchScalarGridSpec(
        num_scalar_prefetch=1,
        grid=(n_dec, n_pages),
        in_specs=[per_seq((1, rows2, width)), per_seq((1, NEW_PAD, width)), per_seq((1, NEW_PAD, width)),
                  page_spec, page_spec],
        out_specs=[per_seq((1, out_rows, gw)), per_seq((1, out_rows, gw))],
        scratch_shapes=[pltpu.VMEM((rows2, width), BF16),
                        pltpu.VMEM((n_blocks, gw), F32),
                        pltpu.VMEM((n_blocks, nr, 128), F32),
                        pltpu.VMEM((n_blocks, nr, 128), F32),
                        pltpu.VMEM((n_blocks, nr, gw), F32),
                        pltpu.VMEM((nr, 128), F32),
                        pltpu.VMEM((nr, gw), F32)],
    )
    return pl.pallas_call(
        functools.partial(_decode_kernel, n_pages=n_pages, gw=gw, nq=nq),
        grid_spec=grid_spec,
        out_shape=[jax.ShapeDtypeStruct((n_dec, out_rows, gw), F32)] * 2,
        compiler_params=_cparams(2),
        name="decode_attention",
    )(page_table, qbd, knew, vnew, cache_k, cache_v)


def _merge_kernel(oa_ref, ob_ref, x_ref, ga_ref, gb_ref, w_ref, g_ref, b_ref, y_ref, *, alpha):
    gw = oa_ref.shape[1]
    ya = _rms_norm(oa_ref[...], ga_ref[...]).astype(BF16)
    yb = _rms_norm(ob_ref[...], gb_ref[...]).astype(BF16)
    mix = (jnp.dot(ya, w_ref[:gw, :], preferred_element_type=F32)
           + jnp.dot(yb, w_ref[gw:, :], preferred_element_type=F32))
    y_ref[...] = _layer_norm(alpha * x_ref[...] + mix, g_ref[...], b_ref[...])


def _merge(oa, ob, x, g_a, g_b, w16, ln_g, ln_b, alpha):
    t, d = x.shape
    gw = oa.shape[1]
    rows = _row_tile(t)
    row = lambda width: pl.BlockSpec((rows, width), lambda i: (i, 0))
    full = lambda a: pl.BlockSpec(a.shape, lambda i: (0, 0))
    return pl.pallas_call(
        functools.partial(_merge_kernel, alpha=alpha),
        grid=(t // rows,),
        in_specs=[row(gw), row(gw), row(d), full(g_a), full(g_b), full(w16), full(ln_g), full(ln_b)],
        out_specs=row(d),
        out_shape=jax.ShapeDtypeStruct((t, d), F32),
        compiler_params=_cparams(1),
        name="merge_out_proj",
    )(oa, ob, x, g_a, g_b, w16, ln_g, ln_b)


def _mem_kv_kernel(m_ref, w_ref, k_ref, v_ref, k16_ref, v16_ref):
    width = k_ref.shape[1]
    m16 = m_ref[...].astype(BF16)
    k = jnp.dot(m16, w_ref[:, :width], preferred_element_type=F32)
    v = jnp.dot(m16, w_ref[:, width:], preferred_element_type=F32)
    k_ref[...] = k
    v_ref[...] = v
    k16_ref[...] = k.astype(BF16)
    v16_ref[...] = v.astype(BF16)


def _mem_kv(mem, w16):
    t, d = mem.shape
    width = w16.shape[1] // 2
    rows = _row_tile(t)
    row = lambda w: pl.BlockSpec((rows, w), lambda i: (i, 0))
    return pl.pallas_call(
        _mem_kv_kernel,
        grid=(t // rows,),
        in_specs=[row(d), pl.BlockSpec(w16.shape, lambda i: (0, 0))],
        out_specs=[row(width)] * 4,
        out_shape=[jax.ShapeDtypeStruct((t, width), F32)] * 2 + [jax.ShapeDtypeStruct((t, width), BF16)] * 2,
        compiler_params=_cparams(1),
        name="mem_kv",
    )(mem, w16)


def _mem_heads(q16, mk16, mv16, row_mask=None):
    hd = q16.shape[1] // MEM_HEADS
    outs = []
    for h in range(MEM_HEADS):
        hs = slice(h * hd, (h + 1) * hd)
        s = lax.dot_general(q16[:, hs], mk16[:, hs], _NT, preferred_element_type=F32)
        m = jnp.max(s, axis=1, keepdims=True)
        p = jnp.exp(s - m)
        l = jnp.sum(p, axis=1, keepdims=True)
        o = jnp.dot(p.astype(BF16), mv16[:, hs], preferred_element_type=F32) / l
        outs.append(o if row_mask is None else jnp.where(row_mask, o, 0.0))
    return jnp.concatenate(outs, axis=1)


def _mem_attend_kernel(x_ref, wq_ref, mk_ref, mv_ref, wo_ref, g_ref, b_ref, y_ref, *, alpha):
    x = x_ref[...]
    hd = x.shape[1] // MEM_HEADS
    q16 = (jnp.dot(x.astype(BF16), wq_ref[...], preferred_element_type=F32) * hd ** -0.5).astype(BF16)
    o = _mem_heads(q16, mk_ref[...], mv_ref[...])
    y = jnp.dot(o.astype(BF16), wo_ref[...], preferred_element_type=F32)
    y_ref[...] = _layer_norm(alpha * x + y, g_ref[...], b_ref[...])


def _mem_attend_prompt(x, wq16, mk16, mv16, wo16, ln_g, ln_b, n_seq, alpha):
    t, d = x.shape
    seq_len = t // n_seq
    n_mem = mk16.shape[0] // n_seq
    rows = _row_tile(seq_len)
    tiles = seq_len // rows
    row = pl.BlockSpec((rows, d), lambda b, i: (b * tiles + i, 0))
    full = lambda a: pl.BlockSpec(a.shape, lambda b, i: (0, 0))
    mem = pl.BlockSpec((n_mem, d), lambda b, i: (b, 0))
    return pl.pallas_call(
        functools.partial(_mem_attend_kernel, alpha=alpha),
        grid=(n_seq, tiles),
        in_specs=[row, full(wq16), mem, mem, full(wo16), full(ln_g), full(ln_b)],
        out_specs=row,
        out_shape=jax.ShapeDtypeStruct((t, d), F32),
        compiler_params=_cparams(2),
        name="mem_attend_prompt",
    )(x, wq16, mk16, mv16, wo16, ln_g, ln_b)


def _mem_attend_sample_kernel(x_ref, wq_ref, mk_ref, mv_ref, wo_ref, g_ref, b_ref, y_ref, *, alpha, nq):
    x = x_ref[...]
    hd = x.shape[1] // MEM_HEADS
    q16 = (jnp.dot(x.astype(BF16), wq_ref[...], preferred_element_type=F32) * hd ** -0.5).astype(BF16)
    seq_of_row = _idiv(lax.broadcasted_iota(jnp.int32, (x.shape[0], 1), 0), nq)
    o = jnp.zeros(x.shape, F32)
    for g in range(mk_ref.shape[0]):
        o = o + _mem_heads(q16, mk_ref[g].astype(BF16), mv_ref[g].astype(BF16), seq_of_row == g)
    y = jnp.dot(o.astype(BF16), wo_ref[...], preferred_element_type=F32)
    y_ref[...] = _layer_norm(alpha * x + y, g_ref[...], b_ref[...])


def _mem_attend_sample(x, wq16, cache_mk, cache_mv, wo16, ln_g, ln_b, nq, alpha):
    t, d = x.shape
    n_dec, n_mem, _ = cache_mk.shape
    rows = MEM_GROUP * nq
    row = pl.BlockSpec((rows, d), lambda i: (i, 0))
    full = lambda a: pl.BlockSpec(a.shape, lambda i: (0, 0))
    mem = pl.BlockSpec((MEM_GROUP, n_mem, d), lambda i: (i, 0, 0))
    return pl.pallas_call(
        functools.partial(_mem_attend_sample_kernel, alpha=alpha, nq=nq),
        grid=(n_dec // MEM_GROUP,),
        in_specs=[row, full(wq16), mem, mem, full(wo16), full(ln_g), full(ln_b)],
        out_specs=row,
        out_shape=jax.ShapeDtypeStruct((t, d), F32),
        compiler_params=_cparams(1),
        name="mem_attend_sample",
    )(x, wq16, cache_mk, cache_mv, wo16, ln_g, ln_b)


def _router_kernel(x_ref, w_ref, b_ref, e_ref, g_ref):
    logits = jnp.dot(x_ref[...], w_ref[...], precision=HIGHEST, preferred_element_type=F32) + b_ref[...]
    n_exp = logits.shape[1]
    e_iota = lax.broadcasted_iota(jnp.int32, logits.shape, 1)
    k_iota = lax.broadcasted_iota(jnp.int32, e_ref.shape, 1)
    top_e = jnp.zeros(e_ref.shape, jnp.int32)
    top_v = jnp.zeros(e_ref.shape, F32)
    for k in range(TOP_K):
        mx = jnp.max(logits, axis=1, keepdims=True)
        idx = jnp.min(jnp.where(logits == mx, e_iota, n_exp), axis=1, keepdims=True)
        top_e = jnp.where(k_iota == k, idx, top_e)
        top_v = jnp.where(k_iota == k, mx, top_v)
        logits = jnp.where(e_iota == idx, NEG_INF, logits)
    w = jnp.exp(top_v - top_v[:, :1])
    e_ref[...] = top_e
    g_ref[...] = w / jnp.sum(w, axis=1, keepdims=True)


def _router(x, w_router, b_router):
    t, d = x.shape
    rows = _row_tile(t)
    out = pl.BlockSpec((rows, TOP_K), lambda i: (i, 0))
    return pl.pallas_call(
        _router_kernel,
        grid=(t // rows,),
        in_specs=[pl.BlockSpec((rows, d), lambda i: (i, 0)),
                  pl.BlockSpec(w_router.shape, lambda i: (0, 0)),
                  pl.BlockSpec(b_router.shape, lambda i: (0, 0))],
        out_specs=[out, out],
        out_shape=[jax.ShapeDtypeStruct((t, TOP_K), jnp.int32), jax.ShapeDtypeStruct((t, TOP_K), F32)],
        compiler_params=_cparams(1),
        name="router_top4",
    )(x, w_router, b_router)


def _row_gather(src_hbm, idx_ref, dst, sem, n_rows):
    def issue(r, carry):
        pltpu.make_async_copy(src_hbm.at[pl.ds(idx_ref[0, 0, r], 1)], dst.at[pl.ds(r, 1)], sem).start()
        return carry
    lax.fori_loop(0, n_rows, issue, 0)


def _row_gather_wait(src_hbm, dst, sem, n_rows):
    def wait(r, carry):
        pltpu.make_async_copy(src_hbm.at[pl.ds(0, 1)], dst.at[pl.ds(r, 1)], sem).wait()
        return carry
    lax.fori_loop(0, n_rows, wait, 0)


def _expert_ffn_kernel(be_ref, nu_ref, idx_ref, idx_next_ref, x_hbm, wgu_ref, bgu_ref, wd_ref, bd_ref, y_ref,
                       xbuf, sems):
    del be_ref
    i = pl.program_id(0)
    n_used = nu_ref[0]
    rows = y_ref.shape[0]
    slot = i % 2

    @pl.when(jnp.logical_and(i == 0, n_used > 0))
    def _first():
        _row_gather(x_hbm, idx_ref, xbuf.at[0], sems.at[0], rows)

    @pl.when(i + 1 < n_used)
    def _prefetch():
        _row_gather(x_hbm, idx_next_ref, xbuf.at[1 - slot], sems.at[1 - slot], rows)

    @pl.when(i < n_used)
    def _compute():
        _row_gather_wait(x_hbm, xbuf.at[slot], sems.at[slot], rows)
        f = wd_ref.shape[1]
        x16 = xbuf[slot].astype(BF16)
        hgu = jnp.dot(x16, wgu_ref[0], preferred_element_type=F32) + bgu_ref[0]
        gate = jnp.minimum(hgu[:, :f], SWIGLU_LIMIT)
        up = jnp.clip(hgu[:, f:], -SWIGLU_LIMIT, SWIGLU_LIMIT)
        act = (up + 1.0) * gate * jax.nn.sigmoid(SWIGLU_ALPHA * gate)
        y_ref[...] = jnp.dot(act.astype(BF16), wd_ref[0], preferred_element_type=F32) + bd_ref[0]

    @pl.when(i >= n_used)
    def _unused():
        y_ref[...] = jnp.zeros(y_ref.shape, F32)


def _expert_ffn(x, row_tok, block_e, n_used, wgu16, b_gu, wd16, b_down):
    n_blk, _, rows = row_tok.shape
    n_exp, d, f2 = wgu16.shape
    f = f2 // 2
    idx_spec = lambda shift: pl.BlockSpec(
        (1, 1, rows), lambda i, be, nu: (jnp.minimum(i + shift, n_blk - 1), 0, 0), memory_space=pltpu.SMEM)
    by_expert = lambda shape: pl.BlockSpec(shape, lambda i, be, nu: (be[i], 0, 0))
    grid_spec = pltpu.PrefetchScalarGridSpec(
        num_scalar_prefetch=2,
        grid=(n_blk,),
        in_specs=[idx_spec(0), idx_spec(1), pl.BlockSpec(memory_space=pl.ANY),
                  by_expert((1, d, f2)), by_expert((1, 1, f2)), by_expert((1, f, d)), by_expert((1, 1, d))],
        out_specs=pl.BlockSpec((rows, d), lambda i, be, nu: (i, 0)),
        scratch_shapes=[pltpu.VMEM((2, rows, d), F32), pltpu.SemaphoreType.DMA((2,))],
    )
    return pl.pallas_call(
        _expert_ffn_kernel,
        grid_spec=grid_spec,
        out_shape=jax.ShapeDtypeStruct((n_blk * rows, d), F32),
        compiler_params=_cparams(1),
        name="expert_ffn",
    )(block_e, n_used, row_tok, row_tok, x, wgu16, b_gu.reshape(n_exp, 1, f2), wd16, b_down.reshape(n_exp, 1, d))


def _combine_kernel(idx_ref, idx_next_ref, y_hbm, x_ref, gate_ref, g_ref, b_ref, o_ref, ybuf, sems, *, alpha):
    i = pl.program_id(0)
    n_steps = pl.num_programs(0)
    rows = x_ref.shape[0]
    n_copy = TOP_K * rows
    slot = i % 2

    @pl.when(i == 0)
    def _first():
        _row_gather(y_hbm, idx_ref, ybuf.at[0], sems.at[0], n_copy)

    @pl.when(i + 1 < n_steps)
    def _prefetch():
        _row_gather(y_hbm, idx_next_ref, ybuf.at[1 - slot], sems.at[1 - slot], n_copy)

    _row_gather_wait(y_hbm, ybuf.at[slot], sems.at[slot], n_copy)
    gates = gate_ref[...]
    ffn = jnp.zeros(x_ref.shape, F32)
    for k in range(TOP_K):
        ffn = ffn + gates[:, k:k + 1] * ybuf[slot, k * rows:(k + 1) * rows, :]
    o_ref[...] = _layer_norm(alpha * x_ref[...] + ffn, g_ref[...], b_ref[...])


def _combine(y_rows, dest, x, gates, ln_g, ln_b, alpha):
    t, d = x.shape
    n_steps, _, n_copy = dest.shape
    rows = n_copy // TOP_K
    idx_spec = lambda shift: pl.BlockSpec(
        (1, 1, n_copy), lambda i: (jnp.minimum(i + shift, n_steps - 1), 0, 0), memory_space=pltpu.SMEM)
    row = lambda w: pl.BlockSpec((rows, w), lambda i: (i, 0))
    full = lambda a: pl.BlockSpec(a.shape, lambda i: (0, 0))
    return pl.pallas_call(
        functools.partial(_combine_kernel, alpha=alpha),
        grid=(n_steps,),
        in_specs=[idx_spec(0), idx_spec(1), pl.BlockSpec(memory_space=pl.ANY), row(d), row(TOP_K),
                  full(ln_g), full(ln_b)],
        out_specs=row(d),
        out_shape=jax.ShapeDtypeStruct((t, d), F32),
        scratch_shapes=[pltpu.VMEM((2, n_copy, d), F32), pltpu.SemaphoreType.DMA((2,))],
        compiler_params=_cparams(1),
        name="moe_combine",
    )(dest, dest, y_rows, x, gates, ln_g, ln_b)


def _dispatch_plan(top_e, n_exp, rows):
    n_assign = top_e.size
    flat_e = top_e.reshape(n_assign)
    onehot = (flat_e[:, None] == jnp.arange(n_exp, dtype=jnp.int32)[None, :]).astype(jnp.int32)
    running = jnp.cumsum(onehot, axis=0)
    rank = jnp.sum(jnp.where(onehot == 1, running - 1, 0), axis=1)
    counts = running[-1]
    padded = (counts + rows - 1) // rows * rows
    pad_end = jnp.cumsum(padded)
    pad_start = pad_end - padded
    dest = (pad_start[flat_e] + rank).astype(jnp.int32)
    n_blk = -(-(n_assign + n_exp * (rows - 1)) // rows)
    row_tok = jnp.zeros((n_blk * rows,), jnp.int32).at[dest].set(
        jnp.arange(n_assign, dtype=jnp.int32) // top_e.shape[1])
    block_e = jnp.minimum(jnp.searchsorted(pad_end, jnp.arange(n_blk, dtype=jnp.int32) * rows, side="right"),
                          n_exp - 1).astype(jnp.int32)
    n_used = (pad_end[-1] // rows).astype(jnp.int32).reshape(1)
    return row_tok.reshape(n_blk, 1, rows), dest, block_e, n_used


def _moe(tok, w_router, b_router, wgu16, b_gu, wd16, b_down, ln_g, ln_b, alpha):
    t, d = tok.shape
    n_exp = w_router.shape[1]
    top_e, gates = _router(tok, w_router, b_router.reshape(1, n_exp))
    row_tok, dest, block_e, n_used = _dispatch_plan(top_e, n_exp, EXPERT_ROWS)
    y_rows = _expert_ffn(tok, row_tok, block_e, n_used, wgu16, b_gu, wd16, b_down)
    steps = t // COMBINE_ROWS
    dest_steps = dest.reshape(steps, COMBINE_ROWS, TOP_K).transpose(0, 2, 1).reshape(steps, 1, TOP_K * COMBINE_ROWS)
    return _combine(y_rows, dest_steps, tok, gates, ln_g, ln_b, alpha)


def _rope_tables(pos, n_heads):
    half = HEAD_DIM // 2
    inv_freq = ROPE_THETA ** (-jnp.arange(half, dtype=F32) / half)
    ang = pos.astype(F32)[:, None] * inv_freq[None, :]
    cos = jnp.cos(ang)
    sin = jnp.sin(ang)
    return (jnp.tile(jnp.concatenate([cos, cos], axis=1), (1, n_heads)),
            jnp.tile(jnp.concatenate([-sin, sin], axis=1), (1, n_heads)))


def _block_diag_queries(q, nq, n_heads):
    n_dec = q.shape[0] // nq
    width = n_heads * HEAD_DIM
    rows = jnp.tile(q.reshape(n_dec, 1, nq, width), (1, n_heads, 1, 1)).reshape(n_dec, n_heads * nq, width)
    r = jnp.arange(n_heads * nq)[:, None] // nq
    c = jnp.arange(width)[None, :] // HEAD_DIM
    return jnp.where((r == c)[None], rows, 0.0)


def kernel(x_prompt, x_sample, mem_prompt, cache_k, cache_v, cache_mem_k, cache_mem_v, page_table,
           w_in, g_moba, g_sb, w_out, ln1_g, ln1_b, w_mq, w_mkv, w_mo, ln2_g, ln2_b,
           w_router, b_router, w_gu, b_gu, w_down, b_down, ln3_g, ln3_b):
    n_seq, seq_len, d = x_prompt.shape
    n_dec, nq, _ = x_sample.shape
    depth = w_in.shape[0]
    gw = w_in.shape[2] // 6
    g_heads = gw // HEAD_DIM
    n_pages = page_table.shape[1]
    page = cache_k.shape[2]
    past_len = n_pages * page
    n_mem = mem_prompt.shape[1]
    alpha = (2 * depth) ** 0.25
    assert seq_len % MOBA_BLOCK == 0 and past_len % MOBA_BLOCK == 0 and MOBA_BLOCK % page == 0
    assert nq <= NEW_PAD and nq <= 8 and n_dec % MEM_GROUP == 0
    assert (n_seq * seq_len + n_dec * nq) % COMBINE_ROWS == 0 and (n_dec * nq) % PROJ_ROWS == 0

    cos_p, sin_p = _rope_tables(jnp.arange(seq_len), g_heads)
    cos_s, sin_s = _rope_tables(jnp.tile(past_len + jnp.arange(nq), n_dec), g_heads)
    n_tok_p = n_seq * seq_len
    n_tok_s = n_dec * nq
    xp = x_prompt.reshape(n_tok_p, d)
    xs = x_sample.reshape(n_tok_s, d)
    row2 = lambda a: a.reshape(1, -1)
    outs = [[] for _ in range(6)]
    for l in range(depth):
        w_in16 = w_in[l].astype(BF16)
        w_out16 = w_out[l].astype(BF16)
        q_p, k_p, v_p, k16_p, v16_p, kbar_p = _qkv_project(xp, w_in16, cos_p, sin_p, PROJ_ROWS, True)
        kbar_p = kbar_p.reshape(n_seq, seq_len // MOBA_BLOCK, gw)
        oa_p = _moba_prompt(q_p, k16_p, v16_p, kbar_p, n_seq, seq_len, gw)
        ob_p = _sb_prompt(q_p, k16_p, v16_p, n_seq, seq_len, gw)
        outs[0].append(k_p.reshape(n_seq, seq_len, 2 * g_heads, HEAD_DIM))
        outs[1].append(v_p.reshape(n_seq, seq_len, 2 * g_heads, HEAD_DIM))
        q_s, k_s, v_s, _, _ = _qkv_project(xs, w_in16, cos_s, sin_s, PROJ_ROWS, False)
        pad_new = lambda a: jnp.pad(a.reshape(n_dec, nq, 2 * gw), ((0, 0), (0, NEW_PAD - nq), (0, 0)))
        oa_s, ob_s = _decode_attention(
            page_table, _block_diag_queries(q_s, nq, 2 * g_heads), pad_new(k_s), pad_new(v_s),
            cache_k[l].reshape(-1, page, 2 * gw), cache_v[l].reshape(-1, page, 2 * gw), gw, nq)
        oa_s = oa_s[:, :nq].reshape(n_tok_s, gw)
        ob_s = ob_s[:, :nq].reshape(n_tok_s, gw)
        outs[2].append(k_s.reshape(n_dec, nq, 2 * g_heads, HEAD_DIM))
        outs[3].append(v_s.reshape(n_dec, nq, 2 * g_heads, HEAD_DIM))
        merge = functools.partial(_merge, g_a=row2(g_moba[l]), g_b=row2(g_sb[l]), w16=w_out16,
                                  ln_g=row2(ln1_g[l]), ln_b=row2(ln1_b[l]), alpha=alpha)
        xp = merge(oa_p, ob_p, xp)
        xs = merge(oa_s, ob_s, xs)
        mk, mv, mk16, mv16 = _mem_kv(mem_prompt.reshape(n_seq * n_mem, d), w_mkv[l].astype(BF16))
        outs[4].append(mk.reshape(n_seq, n_mem, MEM_HEADS, d // MEM_HEADS))
        outs[5].append(mv.reshape(n_seq, n_mem, MEM_HEADS, d // MEM_HEADS))
        wq16 = w_mq[l].astype(BF16)
        wo16 = w_mo[l].astype(BF16)
        xp = _mem_attend_prompt(xp, wq16, mk16, mv16, wo16, row2(ln2_g[l]), row2(ln2_b[l]), n_seq, alpha)
        xs = _mem_attend_sample(xs, wq16, cache_mem_k[l].reshape(n_dec, n_mem, d),
                                cache_mem_v[l].reshape(n_dec, n_mem, d), wo16,
                                row2(ln2_g[l]), row2(ln2_b[l]), nq, alpha)
        tok = jnp.concatenate([xp, xs], axis=0)
        tok = _moe(tok, w_router[l], b_router[l], w_gu[l].astype(BF16), b_gu[l], w_down[l].astype(BF16),
                   b_down[l], row2(ln3_g[l]), row2(ln3_b[l]), alpha)
        xp = tok[:n_tok_p]
        xs = tok[n_tok_p:]
    return (xp.reshape(n_seq, seq_len, d), xs.reshape(n_dec, nq, d)) + tuple(jnp.stack(o) for o in outs)
```

```python
import functools

import jax
import jax.numpy as jnp
from jax import lax
from jax.experimental import pallas as pl
from jax.experimental.pallas import tpu as pltpu

F32 = jnp.float32
BF16 = jnp.bfloat16
HIGHEST = lax.Precision.HIGHEST

HEAD_DIM = 64
MOBA_BLOCK = 256
MOBA_TOP_K = 3
Q_BLOCK = 128
ROPE_THETA = 10000.0
MEM_HEADS = 4
TOP_K = 4
SWIGLU_LIMIT = 7.0
SWIGLU_ALPHA = 1.702
LN_EPS = 1e-5
RMS_EPS = 1e-6
ATT_SCALE = HEAD_DIM ** -0.5
NEG_INF = float("-inf")
SB_NEGLIGIBLE = -150.0

PROJ_ROWS = MOBA_BLOCK
SB_KEYS = 128
MOBA_HEADS = 4
SB_HEADS = 8
DECODE_PAGES = 4
TOKEN_ROWS = 512
EXPERT_ROWS = 256
COMBINE_ROWS = 128
NEW_PAD = 16
MEM_GROUP = 4
DMA_UNROLL = 8
VMEM_LIMIT = 56 * 1024 * 1024

_NT = (((1,), (1,)), ((), ()))


def _cparams(n_axes):
    return pltpu.CompilerParams(dimension_semantics=("arbitrary",) * n_axes,
                                vmem_limit_bytes=VMEM_LIMIT)


def _row_tile(n_rows):
    rows = TOKEN_ROWS
    while n_rows % rows:
        rows //= 2
    assert rows >= 8, n_rows
    return rows


def _idiv(x, n):
    return x >> (n.bit_length() - 1) if n & (n - 1) == 0 else x // n


def _imod(x, n):
    return x & (n - 1) if n & (n - 1) == 0 else x % n


def _layer_norm(x, g, b):
    mu = jnp.mean(x, axis=-1, keepdims=True)
    xc = x - mu
    var = jnp.mean(xc * xc, axis=-1, keepdims=True)
    return xc * lax.rsqrt(var + LN_EPS) * g + b


def _rms_norm(x, g):
    return x * lax.rsqrt(jnp.mean(x * x, axis=-1, keepdims=True) + RMS_EPS) * g


def _log_sigmoid_pair(z):
    t = jnp.log1p(jnp.exp(-jnp.abs(z)))
    return -(jnp.maximum(z, 0.0) + t), jnp.minimum(z, 0.0) - t


def _suffix_sum(x, upper):
    hi = x.astype(BF16)
    lo = (x - hi.astype(F32)).astype(BF16)
    return (jnp.dot(hi, upper, preferred_element_type=F32)
            + jnp.dot(lo, upper, preferred_element_type=F32))


def _strict_lower(n):
    r = lax.broadcasted_iota(jnp.int32, (n, n), 0)
    c = lax.broadcasted_iota(jnp.int32, (n, n), 1)
    return jnp.where(r > c, 1.0, 0.0).astype(BF16)


def _strided_suffix_sum(x, stride, col):
    n = x.shape[1]
    sh = stride
    while sh < n:
        x = x + jnp.where(col + sh < n, pltpu.roll(x, n - sh, 1), 0.0)
        sh *= 2
    return x


def _top_block_bias(gates, n_valid):
    nb = gates.shape[1]
    n_iota = lax.broadcasted_iota(jnp.int32, gates.shape, 1)
    valid = n_iota < n_valid
    g = jnp.where(valid, gates, NEG_INF)
    cnt = jnp.zeros(gates.shape, jnp.int32)
    for m in range(nb):
        gm = g[:, m:m + 1]
        beats = jnp.where(gm > g, 1, jnp.where(gm == g, jnp.where(n_iota > m, 1, 0), 0))
        cnt = cnt + beats
    sel = jnp.where(valid, jnp.where(cnt < MOBA_TOP_K, 1, 0), 0)
    return jnp.where(sel == 1, 0.0, NEG_INF)


def _qkv_kernel(x_ref, w_ref, cos_ref, sin_ref, q_ref, k_ref, v_ref, k16_ref, v16_ref, *rest, gw):
    x16 = x_ref[...].astype(BF16)
    cos = cos_ref[...]
    sin = sin_ref[...]
    lane = lax.broadcasted_iota(jnp.int32, cos.shape, 1)
    first_half = (lane & (HEAD_DIM - 1)) < HEAD_DIM // 2

    def proj(g):
        return jnp.dot(x16, w_ref[:, g * gw:(g + 1) * gw], preferred_element_type=F32)

    def rope(t):
        rot = jnp.where(first_half, pltpu.roll(t, gw - HEAD_DIM // 2, 1), pltpu.roll(t, HEAD_DIM // 2, 1))
        return t * cos + rot * sin

    q_ref[:, :gw] = rope(proj(0))
    ka = rope(proj(1))
    k_ref[:, :gw] = ka
    k16_ref[:, :gw] = ka.astype(BF16)
    if rest:
        rest[0][0] = jnp.mean(ka, axis=0, keepdims=True)
    va = proj(2)
    v_ref[:, :gw] = va
    v16_ref[:, :gw] = va.astype(BF16)
    q_ref[:, gw:] = proj(3)
    kb = proj(4)
    k_ref[:, gw:] = kb
    k16_ref[:, gw:] = kb.astype(BF16)
    vb = proj(5)
    v_ref[:, gw:] = vb
    v16_ref[:, gw:] = vb.astype(BF16)


def _qkv_project(x, w16, cos, sin, rows, with_kbar):
    t, d = x.shape
    gw = w16.shape[1] // 6
    n_pos = cos.shape[0] // rows
    row_spec = lambda width: pl.BlockSpec((rows, width), lambda i: (i, 0))
    tab_spec = pl.BlockSpec((rows, gw), lambda i: (i % n_pos, 0))
    out_shape = [jax.ShapeDtypeStruct((t, 2 * gw), F32)] * 3 + [jax.ShapeDtypeStruct((t, 2 * gw), BF16)] * 2
    out_specs = [row_spec(2 * gw)] * 5
    if with_kbar:
        out_shape.append(jax.ShapeDtypeStruct((t // rows, 1, gw), F32))
        out_specs.append(pl.BlockSpec((1, 1, gw), lambda i: (i, 0, 0)))
    return pl.pallas_call(
        functools.partial(_qkv_kernel, gw=gw),
        grid=(t // rows,),
        in_specs=[row_spec(d), pl.BlockSpec((d, 6 * gw), lambda i: (0, 0)), tab_spec, tab_spec],
        out_specs=out_specs,
        out_shape=out_shape,
        compiler_params=_cparams(1),
        name="qkv_rope",
    )(x, w16, cos, sin)


def _moba_prompt_kernel(q_ref, k_ref, v_ref, kbar_ref, o_ref):
    c = pl.program_id(2)
    own = c // (MOBA_BLOCK // Q_BLOCK)
    tq = q_ref.shape[0]
    heads = q_ref.shape[1] // HEAD_DIM
    nb = kbar_ref.shape[1]
    row = lax.broadcasted_iota(jnp.int32, (tq, MOBA_BLOCK), 0)
    col = lax.broadcasted_iota(jnp.int32, (tq, MOBA_BLOCK), 1)
    causal = own * MOBA_BLOCK + col <= c * tq + row
    n_iota = lax.broadcasted_iota(jnp.int32, (tq, nb), 1)
    hs = [slice(h * HEAD_DIM, (h + 1) * HEAD_DIM) for h in range(heads)]
    q16, bias = [], []
    for h in range(heads):
        q = q_ref[:, hs[h]]
        gates = lax.dot_general(q, kbar_ref[0, :, hs[h]], _NT, precision=HIGHEST, preferred_element_type=F32)
        bias.append(_top_block_bias(gates, own))
        q16.append((q * ATT_SCALE).astype(BF16))

    def scores(j, h):
        start = pl.multiple_of(j * MOBA_BLOCK, MOBA_BLOCK)
        kb = k_ref[pl.ds(start, MOBA_BLOCK), hs[h]]
        return lax.dot_general(q16[h], kb, _NT, preferred_element_type=F32), v_ref[pl.ds(start, MOBA_BLOCK), hs[h]]

    state = []
    for h in range(heads):
        s, vb = scores(own, h)
        s = jnp.where(causal, s, NEG_INF)
        m = jnp.max(s, axis=1, keepdims=True)
        p = jnp.exp(s - m)
        state.append((m, jnp.sum(p, axis=1, keepdims=True), jnp.dot(p.astype(BF16), vb, preferred_element_type=F32)))

    def body(j, state):
        sv = [scores(j, h) for h in range(heads)]
        s = [sv[h][0] + jnp.sum(jnp.where(n_iota == j, bias[h], 0.0), axis=1, keepdims=True) for h in range(heads)]
        m_new = [jnp.maximum(state[h][0], jnp.max(s[h], axis=1, keepdims=True)) for h in range(heads)]
        p = [jnp.exp(s[h] - m_new[h]) for h in range(heads)]
        new = []
        for h in range(heads):
            m, l, acc = state[h]
            alpha = jnp.exp(m - m_new[h])
            new.append((m_new[h], alpha * l + jnp.sum(p[h], axis=1, keepdims=True),
                        alpha * acc + jnp.dot(p[h].astype(BF16), sv[h][1], preferred_element_type=F32)))
        return tuple(new)

    state = lax.fori_loop(0, own, body, tuple(state))
    for h in range(heads):
        o_ref[:, hs[h]] = state[h][2] / state[h][1]


def _moba_prompt(q, k16, v16, kbar, n_seq, seq_len, gw):
    width = min(MOBA_HEADS * HEAD_DIM, gw)
    n_groups = gw // width
    nc = seq_len // Q_BLOCK
    nb = seq_len // MOBA_BLOCK
    kv_spec = pl.BlockSpec((seq_len, width), lambda b, hg, c: (b, hg))
    return pl.pallas_call(
        _moba_prompt_kernel,
        grid=(n_seq, n_groups, nc),
        in_specs=[pl.BlockSpec((Q_BLOCK, width), lambda b, hg, c: (b * nc + c, hg)),
                  kv_spec, kv_spec,
                  pl.BlockSpec((1, nb, width), lambda b, hg, c: (b, 0, hg))],
        out_specs=pl.BlockSpec((Q_BLOCK, width), lambda b, hg, c: (b * nc + c, hg)),
        out_shape=jax.ShapeDtypeStruct((n_seq * seq_len, gw), F32),
        compiler_params=_cparams(3),
        name="moba_prompt",
    )(q, k16, v16, kbar)


def _sb_prompt_kernel(q_ref, k_ref, v_ref, o_ref):
    c = pl.program_id(2)
    tq = q_ref.shape[0]
    heads = q_ref.shape[1] // HEAD_DIM
    upper = _strict_lower(SB_KEYS)
    row = lax.broadcasted_iota(jnp.int32, (tq, SB_KEYS), 0)
    col = lax.broadcasted_iota(jnp.int32, (tq, SB_KEYS), 1)
    strict = col < row
    hs = [slice(h * HEAD_DIM, (h + 1) * HEAD_DIM) for h in range(heads)]
    q16 = [(q_ref[:, hs[h]] * ATT_SCALE).astype(BF16) for h in range(heads)]

    def tile(j, state, mask):
        start = pl.multiple_of(j * SB_KEYS, SB_KEYS)
        z = [lax.dot_general(q16[h], k_ref[pl.ds(start, SB_KEYS), hs[h]], _NT, preferred_element_type=F32)
             for h in range(heads)]
        pairs = [_log_sigmoid_pair(zh) for zh in z]
        lk = [pr[0] if mask is None else jnp.where(mask, pr[0], 0.0) for pr in pairs]
        hi = [x.astype(BF16) for x in lk]
        lo = [(x - xh.astype(F32)).astype(BF16) for x, xh in zip(lk, hi)]
        suffix = jnp.dot(jnp.concatenate(hi + lo, axis=0), upper, preferred_element_type=F32)
        new = []
        for h in range(heads):
            r, acc = state[h]
            after = suffix[h * tq:(h + 1) * tq] + suffix[(heads + h) * tq:(heads + h + 1) * tq]
            a = jnp.exp(pairs[h][1] + after + r)
            if mask is not None:
                a = jnp.where(mask, a, 0.0)
            acc = acc + jnp.dot(a.astype(BF16), v_ref[pl.ds(start, SB_KEYS), hs[h]], preferred_element_type=F32)
            new.append((r + jnp.sum(lk[h], axis=1, keepdims=True), acc))
        return tuple(new)

    state = tile(c, tuple((jnp.zeros((tq, 1), F32), jnp.zeros((tq, HEAD_DIM), F32)) for _ in range(heads)), strict)

    def cond(carry):
        i, live, _ = carry
        return jnp.logical_and(i < c, live > 0)

    def body(carry):
        i, _, state = carry
        r_max = state[0][0]
        for h in range(1, heads):
            r_max = jnp.maximum(r_max, state[h][0])
        live = (jnp.max(r_max) > SB_NEGLIGIBLE).astype(jnp.int32)
        return i + 1, live, tile(c - 1 - i, state, None)

    _, _, state = lax.while_loop(cond, body, (jnp.int32(0), jnp.int32(1), state))
    for h in range(heads):
        o_ref[:, hs[h]] = state[h][1]


def _sb_prompt(q, k16, v16, n_seq, seq_len, gw):
    width = min(SB_HEADS * HEAD_DIM, gw)
    n_groups = gw // width
    nc = seq_len // SB_KEYS
    kv_spec = pl.BlockSpec((seq_len, width), lambda b, hg, c: (b, n_groups + hg))
    return pl.pallas_call(
        _sb_prompt_kernel,
        grid=(n_seq, n_groups, nc),
        in_specs=[pl.BlockSpec((SB_KEYS, width), lambda b, hg, c: (b * nc + c, n_groups + hg)),
                  kv_spec, kv_spec],
        out_specs=pl.BlockSpec((SB_KEYS, width), lambda b, hg, c: (b * nc + c, hg)),
        out_shape=jax.ShapeDtypeStruct((n_seq * seq_len, gw), F32),
        compiler_params=_cparams(3),
        name="sb_prompt",
    )(q, k16, v16)


def _decode_kernel(pt_ref, q_ref, knew_ref, vnew_ref, *refs, n_pages, nq):
    del pt_ref
    page_refs = refs[:4 * DECODE_PAGES]
    oa_ref, ob_ref, kbar_sc, m_sc, l_sc, o_sc, r_sc, accb_sc = refs[4 * DECODE_PAGES:]
    p = pl.program_id(1)
    n_steps = n_pages // DECODE_PAGES
    rows = q_ref.shape[1] // 2
    page, gh, hd = page_refs[0].shape[1:]
    pk = page * gh
    pn = knew_ref.shape[2]
    pages_per_block = MOBA_BLOCK // page
    n_blocks = n_pages // pages_per_block
    lane_w = m_sc.shape[2]

    qa16 = (q_ref[0, :rows, :] * ATT_SCALE).astype(BF16)
    qb16 = (q_ref[0, rows:, :] * ATT_SCALE).astype(BF16)
    col = lax.broadcasted_iota(jnp.int32, (rows, pk), 1)
    valid = _idiv(lax.broadcasted_iota(jnp.int32, (rows, pk), 0), nq) == _imod(col, gh)
    rown = lax.broadcasted_iota(jnp.int32, (rows, pn), 0)
    coln = lax.broadcasted_iota(jnp.int32, (rows, pn), 1)
    validn = _idiv(rown, nq) == _imod(coln, gh)
    new_key = _idiv(coln, gh)
    new_query = _imod(rown, nq)

    def rows_of(ref):
        return ref[0].reshape(pk, hd).astype(BF16)

    @pl.when(p == 0)
    def _init():
        z = lax.dot_general(qb16, knew_ref[0, 1].astype(BF16), _NT, preferred_element_type=F32)
        mask = jnp.where(validn, jnp.where(new_key < new_query, 1, 0), 0) == 1
        lk, ls = _log_sigmoid_pair(z)
        lk = jnp.where(mask, lk, 0.0)
        after = _strided_suffix_sum(lk, gh, coln) - lk
        a = jnp.where(mask, jnp.exp(ls + after), 0.0)
        accb_sc[...] = jnp.dot(a.astype(BF16), vnew_ref[0, 1].astype(BF16), preferred_element_type=F32)
        r_sc[...] = jnp.broadcast_to(jnp.sum(lk, axis=1, keepdims=True), r_sc.shape)

    for blk in range(DECODE_PAGES // pages_per_block):
        n = n_blocks - 1 - (p * (DECODE_PAGES // pages_per_block) + blk)
        pages = range(blk * pages_per_block, (blk + 1) * pages_per_block)
        s = [jnp.where(valid, lax.dot_general(qa16, rows_of(page_refs[4 * j]), _NT, preferred_element_type=F32),
                       NEG_INF) for j in pages]
        m_blk = functools.reduce(jnp.maximum, [jnp.max(sj, axis=1, keepdims=True) for sj in s])
        pr = [jnp.exp(sj - m_blk) for sj in s]
        l_blk = functools.reduce(jnp.add, [jnp.sum(pj, axis=1, keepdims=True) for pj in pr])
        o_blk = functools.reduce(jnp.add, [
            jnp.dot(pj.astype(BF16), rows_of(page_refs[4 * j + 2]), preferred_element_type=F32)
            for pj, j in zip(pr, pages)])
        m_sc[n] = jnp.broadcast_to(m_blk, (rows, lane_w))
        l_sc[n] = jnp.broadcast_to(l_blk, (rows, lane_w))
        o_sc[n] = o_blk
        kbar_sc[n] = functools.reduce(jnp.add, [jnp.sum(page_refs[4 * j][0], axis=0) for j in pages])

    z = [lax.dot_general(qb16, rows_of(page_refs[4 * j + 1]), _NT, preferred_element_type=F32)
         for j in range(DECODE_PAGES)]
    pairs = [_log_sigmoid_pair(zj) for zj in z]
    lk = [jnp.where(valid, pr[0], 0.0) for pr in pairs]
    after = [_strided_suffix_sum(x, gh, col) - x for x in lk]
    r = r_sc[:, :1]
    acc = accb_sc[...]
    for j in range(DECODE_PAGES):
        a = jnp.where(valid, jnp.exp(pairs[j][1] + after[j] + r), 0.0)
        acc = acc + jnp.dot(a.astype(BF16), rows_of(page_refs[4 * j + 3]), preferred_element_type=F32)
        r = r + jnp.sum(lk[j], axis=1, keepdims=True)
    accb_sc[...] = acc
    r_sc[...] = jnp.broadcast_to(r, r_sc.shape)

    @pl.when(p == n_steps - 1)
    def _finish():
        qf = q_ref[0, :rows, :]
        er = lax.broadcasted_iota(jnp.int32, (rows, gh), 0)
        ec = lax.broadcasted_iota(jnp.int32, (rows, gh), 1)
        expand = jnp.where(_idiv(er, nq) == ec, 1.0, 0.0)
        g_iota = lax.broadcasted_iota(jnp.int32, (rows, n_blocks), 1)
        gates = jnp.zeros((rows, n_blocks), F32)
        for b in range(n_blocks):
            kbar = jnp.dot(expand, kbar_sc[b] * (1.0 / MOBA_BLOCK), precision=HIGHEST, preferred_element_type=F32)
            gates = jnp.where(g_iota == b, jnp.sum(qf * kbar, axis=1, keepdims=True), gates)
        bias = _top_block_bias(gates, n_blocks)
        sn = lax.dot_general(qa16, knew_ref[0, 0].astype(BF16), _NT, preferred_element_type=F32)
        mask = jnp.where(validn, jnp.where(new_key <= new_query, 1, 0), 0) == 1
        sn = jnp.where(mask, sn, NEG_INF)
        m_own = jnp.max(sn, axis=1, keepdims=True)
        pn_ = jnp.exp(sn - m_own)
        l_own = jnp.sum(pn_, axis=1, keepdims=True)
        o_own = jnp.dot(pn_.astype(BF16), vnew_ref[0, 0].astype(BF16), preferred_element_type=F32)
        m_all = m_own
        for b in range(n_blocks):
            m_all = jnp.maximum(m_all, m_sc[b][:, :1] + bias[:, b:b + 1])
        w_own = jnp.exp(m_own - m_all)
        num = w_own * o_own
        den = w_own * l_own
        for b in range(n_blocks):
            w = jnp.exp(m_sc[b][:, :1] + bias[:, b:b + 1] - m_all)
            num = num + w * o_sc[b]
            den = den + w * l_sc[b][:, :1]
        oa_ref[0] = num / den
        ob_ref[0] = accb_sc[...]


def _decode_attention(page_table, q_rows, knew, vnew, cache_k, cache_v, nq):
    n_dec, n_pages = page_table.shape
    rows = q_rows.shape[1] // 2
    _, page, heads, hd = cache_k.shape
    gh = heads // 2
    n_blocks = n_pages * page // MOBA_BLOCK
    per_seq = lambda a: pl.BlockSpec((1,) + a.shape[1:], lambda b, p, pt: (b,) + (0,) * (a.ndim - 1))
    page_spec = lambda j, g: pl.BlockSpec(
        (1, page, gh, hd), lambda b, p, pt: (pt[b, n_pages - 1 - (p * DECODE_PAGES + j)], 0, g, 0))
    out_spec = pl.BlockSpec((1, rows, hd), lambda b, p, pt: (b, 0, 0))
    assert n_pages % DECODE_PAGES == 0 and DECODE_PAGES % (MOBA_BLOCK // page) == 0
    grid_spec = pltpu.PrefetchScalarGridSpec(
        num_scalar_prefetch=1,
        grid=(n_dec, n_pages // DECODE_PAGES),
        in_specs=[per_seq(q_rows), per_seq(knew), per_seq(vnew)]
        + [page_spec(j, g) for j in range(DECODE_PAGES) for g in (0, 1, 0, 1)],
        out_specs=[out_spec, out_spec],
        scratch_shapes=[pltpu.VMEM((n_blocks, gh, hd), F32),
                        pltpu.VMEM((n_blocks, rows, 128), F32),
                        pltpu.VMEM((n_blocks, rows, 128), F32),
                        pltpu.VMEM((n_blocks, rows, hd), F32),
                        pltpu.VMEM((rows, 128), F32),
                        pltpu.VMEM((rows, hd), F32)],
    )
    return pl.pallas_call(
        functools.partial(_decode_kernel, n_pages=n_pages, nq=nq),
        grid_spec=grid_spec,
        out_shape=[jax.ShapeDtypeStruct((n_dec, rows, hd), F32)] * 2,
        compiler_params=_cparams(2),
        name="decode_attention",
    )(page_table, q_rows, knew, vnew, *([cache_k, cache_k, cache_v, cache_v] * DECODE_PAGES))


def _merge_kernel(oa_ref, ob_ref, x_ref, ga_ref, gb_ref, w_ref, g_ref, b_ref, y_ref, *, alpha):
    gw = oa_ref.shape[1]
    ya = _rms_norm(oa_ref[...], ga_ref[...]).astype(BF16)
    yb = _rms_norm(ob_ref[...], gb_ref[...]).astype(BF16)
    mix = (jnp.dot(ya, w_ref[:gw, :], preferred_element_type=F32)
           + jnp.dot(yb, w_ref[gw:, :], preferred_element_type=F32))
    y_ref[...] = _layer_norm(alpha * x_ref[...] + mix, g_ref[...], b_ref[...])


def _merge(oa, ob, x, g_a, g_b, w16, ln_g, ln_b, alpha):
    t, d = x.shape
    gw = oa.shape[1]
    rows = _row_tile(t)
    row = lambda width: pl.BlockSpec((rows, width), lambda i: (i, 0))
    full = lambda a: pl.BlockSpec(a.shape, lambda i: (0, 0))
    return pl.pallas_call(
        functools.partial(_merge_kernel, alpha=alpha),
        grid=(t // rows,),
        in_specs=[row(gw), row(gw), row(d), full(g_a), full(g_b), full(w16), full(ln_g), full(ln_b)],
        out_specs=row(d),
        out_shape=jax.ShapeDtypeStruct((t, d), F32),
        compiler_params=_cparams(1),
        name="merge_out_proj",
    )(oa, ob, x, g_a, g_b, w16, ln_g, ln_b)


def _mem_kv_kernel(m_ref, w_ref, k_ref, v_ref, k16_ref, v16_ref):
    width = k_ref.shape[1]
    m16 = m_ref[...].astype(BF16)
    k = jnp.dot(m16, w_ref[:, :width], preferred_element_type=F32)
    v = jnp.dot(m16, w_ref[:, width:], preferred_element_type=F32)
    k_ref[...] = k
    v_ref[...] = v
    k16_ref[...] = k.astype(BF16)
    v16_ref[...] = v.astype(BF16)


def _mem_kv(mem, w16):
    t, d = mem.shape
    width = w16.shape[1] // 2
    rows = _row_tile(t)
    row = lambda w: pl.BlockSpec((rows, w), lambda i: (i, 0))
    return pl.pallas_call(
        _mem_kv_kernel,
        grid=(t // rows,),
        in_specs=[row(d), pl.BlockSpec(w16.shape, lambda i: (0, 0))],
        out_specs=[row(width)] * 4,
        out_shape=[jax.ShapeDtypeStruct((t, width), F32)] * 2 + [jax.ShapeDtypeStruct((t, width), BF16)] * 2,
        compiler_params=_cparams(1),
        name="mem_kv",
    )(mem, w16)


def _mem_heads(q16, mk16, mv16, row_mask=None):
    hd = q16.shape[1] // MEM_HEADS
    outs = []
    for h in range(MEM_HEADS):
        hs = slice(h * hd, (h + 1) * hd)
        s = lax.dot_general(q16[:, hs], mk16[:, hs], _NT, preferred_element_type=F32)
        m = jnp.max(s, axis=1, keepdims=True)
        p = jnp.exp(s - m)
        l = jnp.sum(p, axis=1, keepdims=True)
        o = jnp.dot(p.astype(BF16), mv16[:, hs], preferred_element_type=F32) / l
        outs.append(o if row_mask is None else jnp.where(row_mask, o, 0.0))
    return jnp.concatenate(outs, axis=1)


def _mem_attend_kernel(x_ref, wq_ref, mk_ref, mv_ref, wo_ref, g_ref, b_ref, y_ref, *, alpha):
    x = x_ref[...]
    hd = x.shape[1] // MEM_HEADS
    q16 = (jnp.dot(x.astype(BF16), wq_ref[...], preferred_element_type=F32) * hd ** -0.5).astype(BF16)
    o = _mem_heads(q16, mk_ref[...], mv_ref[...])
    y = jnp.dot(o.astype(BF16), wo_ref[...], preferred_element_type=F32)
    y_ref[...] = _layer_norm(alpha * x + y, g_ref[...], b_ref[...])


def _mem_attend_prompt(x, wq16, mk16, mv16, wo16, ln_g, ln_b, n_seq, alpha):
    t, d = x.shape
    seq_len = t // n_seq
    n_mem = mk16.shape[0] // n_seq
    rows = _row_tile(seq_len)
    tiles = seq_len // rows
    row = pl.BlockSpec((rows, d), lambda b, i: (b * tiles + i, 0))
    full = lambda a: pl.BlockSpec(a.shape, lambda b, i: (0, 0))
    mem = pl.BlockSpec((n_mem, d), lambda b, i: (b, 0))
    return pl.pallas_call(
        functools.partial(_mem_attend_kernel, alpha=alpha),
        grid=(n_seq, tiles),
        in_specs=[row, full(wq16), mem, mem, full(wo16), full(ln_g), full(ln_b)],
        out_specs=row,
        out_shape=jax.ShapeDtypeStruct((t, d), F32),
        compiler_params=_cparams(2),
        name="mem_attend_prompt",
    )(x, wq16, mk16, mv16, wo16, ln_g, ln_b)


def _mem_attend_sample_kernel(x_ref, wq_ref, mk_ref, mv_ref, wo_ref, g_ref, b_ref, y_ref, *, alpha, nq):
    x = x_ref[...]
    hd = x.shape[1] // MEM_HEADS
    q16 = (jnp.dot(x.astype(BF16), wq_ref[...], preferred_element_type=F32) * hd ** -0.5).astype(BF16)
    seq_of_row = _idiv(lax.broadcasted_iota(jnp.int32, (x.shape[0], 1), 0), nq)
    o = jnp.zeros(x.shape, F32)
    for g in range(mk_ref.shape[0]):
        o = o + _mem_heads(q16, mk_ref[g].astype(BF16), mv_ref[g].astype(BF16), seq_of_row == g)
    y = jnp.dot(o.astype(BF16), wo_ref[...], preferred_element_type=F32)
    y_ref[...] = _layer_norm(alpha * x + y, g_ref[...], b_ref[...])


def _mem_attend_sample(x, wq16, cache_mk, cache_mv, wo16, ln_g, ln_b, nq, alpha):
    t, d = x.shape
    n_dec, n_mem, _ = cache_mk.shape
    rows = MEM_GROUP * nq
    row = pl.BlockSpec((rows, d), lambda i: (i, 0))
    full = lambda a: pl.BlockSpec(a.shape, lambda i: (0, 0))
    mem = pl.BlockSpec((MEM_GROUP, n_mem, d), lambda i: (i, 0, 0))
    return pl.pallas_call(
        functools.partial(_mem_attend_sample_kernel, alpha=alpha, nq=nq),
        grid=(n_dec // MEM_GROUP,),
        in_specs=[row, full(wq16), mem, mem, full(wo16), full(ln_g), full(ln_b)],
        out_specs=row,
        out_shape=jax.ShapeDtypeStruct((t, d), F32),
        compiler_params=_cparams(1),
        name="mem_attend_sample",
    )(x, wq16, cache_mk, cache_mv, wo16, ln_g, ln_b)


def _router_kernel(x_ref, w_ref, b_ref, e_ref, g_ref):
    logits = jnp.dot(x_ref[...], w_ref[...], precision=HIGHEST, preferred_element_type=F32) + b_ref[...]
    n_exp = logits.shape[1]
    e_iota = lax.broadcasted_iota(jnp.int32, logits.shape, 1)
    k_iota = lax.broadcasted_iota(jnp.int32, e_ref.shape, 1)
    top_e = jnp.zeros(e_ref.shape, jnp.int32)
    top_v = jnp.zeros(e_ref.shape, F32)
    for k in range(TOP_K):
        mx = jnp.max(logits, axis=1, keepdims=True)
        idx = jnp.min(jnp.where(logits == mx, e_iota, n_exp), axis=1, keepdims=True)
        top_e = jnp.where(k_iota == k, idx, top_e)
        top_v = jnp.where(k_iota == k, mx, top_v)
        logits = jnp.where(e_iota == idx, NEG_INF, logits)
    w = jnp.exp(top_v - top_v[:, :1])
    e_ref[...] = top_e
    g_ref[...] = w / jnp.sum(w, axis=1, keepdims=True)


def _router(x, w_router, b_router):
    t, d = x.shape
    rows = _row_tile(t)
    out = pl.BlockSpec((rows, TOP_K), lambda i: (i, 0))
    return pl.pallas_call(
        _router_kernel,
        grid=(t // rows,),
        in_specs=[pl.BlockSpec((rows, d), lambda i: (i, 0)),
                  pl.BlockSpec(w_router.shape, lambda i: (0, 0)),
                  pl.BlockSpec(b_router.shape, lambda i: (0, 0))],
        out_specs=[out, out],
        out_shape=[jax.ShapeDtypeStruct((t, TOP_K), jnp.int32), jax.ShapeDtypeStruct((t, TOP_K), F32)],
        compiler_params=_cparams(1),
        name="router_top4",
    )(x, w_router, b_router)


def _row_copy(src_hbm, src_row, dst, r, sem):
    return pltpu.make_async_copy(src_hbm.at[pl.ds(src_row, 1)], dst.at[pl.ds(r, 1)], sem)


def _row_gather(src_hbm, idx_ref, dst, sem, n_rows):
    for r in range(n_rows):
        _row_copy(src_hbm, idx_ref[0, 0, r], dst, r, sem).start()


def _row_gather_wait(src_hbm, dst, sem, n_rows):
    def wait(r, carry):
        _row_copy(src_hbm, 0, dst, r, sem).wait()
        return carry
    lax.fori_loop(0, n_rows, wait, 0, unroll=DMA_UNROLL)


def _expert_ffn_kernel(be_ref, nu_ref, idx_ref, x_hbm, wgu_ref, bgu_ref, wd_ref, bd_ref, y_ref, xbuf, sems):
    del be_ref
    s = pl.program_id(0)
    n_used = nu_ref[0]
    rows = y_ref.shape[0]

    @pl.when(s < n_used)
    def _fetch():
        _row_gather(x_hbm, idx_ref, xbuf.at[s % 2], sems.at[s % 2], rows)

    @pl.when(jnp.logical_and(s >= 1, s <= n_used))
    def _compute():
        slot = (s - 1) % 2
        _row_gather_wait(x_hbm, xbuf.at[slot], sems.at[slot], rows)
        f = wd_ref.shape[1]
        x16 = xbuf[slot].astype(BF16)
        hgu = jnp.dot(x16, wgu_ref[0], preferred_element_type=F32) + bgu_ref[0]
        gate = jnp.minimum(hgu[:, :f], SWIGLU_LIMIT)
        up = jnp.clip(hgu[:, f:], -SWIGLU_LIMIT, SWIGLU_LIMIT)
        act = (up + 1.0) * gate * jax.nn.sigmoid(SWIGLU_ALPHA * gate)
        y_ref[...] = jnp.dot(act.astype(BF16), wd_ref[0], preferred_element_type=F32) + bd_ref[0]

    @pl.when(s > n_used)
    def _unused():
        y_ref[...] = jnp.zeros(y_ref.shape, F32)


def _expert_ffn(x, row_tok, block_e, n_used, wgu16, b_gu, wd16, b_down):
    n_blk, _, rows = row_tok.shape
    n_exp, d, f2 = wgu16.shape
    f = f2 // 2
    prev = lambda s: jnp.maximum(s - 1, 0)
    by_expert = lambda shape: pl.BlockSpec(shape, lambda s, be, nu: (be[prev(s)], 0, 0))
    grid_spec = pltpu.PrefetchScalarGridSpec(
        num_scalar_prefetch=2,
        grid=(n_blk + 1,),
        in_specs=[pl.BlockSpec((1, 1, rows), lambda s, be, nu: (jnp.minimum(s, n_blk - 1), 0, 0),
                               memory_space=pltpu.SMEM),
                  pl.BlockSpec(memory_space=pl.ANY),
                  by_expert((1, d, f2)), by_expert((1, 1, f2)), by_expert((1, f, d)), by_expert((1, 1, d))],
        out_specs=pl.BlockSpec((rows, d), lambda s, be, nu: (prev(s), 0)),
        scratch_shapes=[pltpu.VMEM((2, rows, d), F32), pltpu.SemaphoreType.DMA((2,))],
    )
    return pl.pallas_call(
        _expert_ffn_kernel,
        grid_spec=grid_spec,
        out_shape=jax.ShapeDtypeStruct((n_blk * rows, d), F32),
        compiler_params=_cparams(1),
        name="expert_ffn",
    )(block_e, n_used, row_tok, x, wgu16, b_gu.reshape(n_exp, 1, f2), wd16, b_down.reshape(n_exp, 1, d))


def _combine_kernel(idx_ref, y_hbm, x_ref, gate_ref, g_ref, b_ref, o_ref, ybuf, sems, *, alpha):
    s = pl.program_id(0)
    n_steps = pl.num_programs(0) - 1
    rows = x_ref.shape[0]
    n_copy = TOP_K * rows

    @pl.when(s < n_steps)
    def _fetch():
        _row_gather(y_hbm, idx_ref, ybuf.at[s % 2], sems.at[s % 2], n_copy)

    @pl.when(s >= 1)
    def _compute():
        slot = (s - 1) % 2
        _row_gather_wait(y_hbm, ybuf.at[slot], sems.at[slot], n_copy)
        gates = gate_ref[...]
        ffn = jnp.zeros(x_ref.shape, F32)
        for k in range(TOP_K):
            ffn = ffn + gates[:, k:k + 1] * ybuf[slot, k * rows:(k + 1) * rows, :]
        o_ref[...] = _layer_norm(alpha * x_ref[...] + ffn, g_ref[...], b_ref[...])


def _combine(y_rows, dest, x, gates, ln_g, ln_b, alpha):
    t, d = x.shape
    n_steps, _, n_copy = dest.shape
    rows = n_copy // TOP_K
    idx_spec = pl.BlockSpec((1, 1, n_copy), lambda s: (jnp.minimum(s, n_steps - 1), 0, 0),
                            memory_space=pltpu.SMEM)
    row = lambda w: pl.BlockSpec((rows, w), lambda s: (jnp.maximum(s - 1, 0), 0))
    full = lambda a: pl.BlockSpec(a.shape, lambda s: (0, 0))
    return pl.pallas_call(
        functools.partial(_combine_kernel, alpha=alpha),
        grid=(n_steps + 1,),
        in_specs=[idx_spec, pl.BlockSpec(memory_space=pl.ANY), row(d), row(TOP_K), full(ln_g), full(ln_b)],
        out_specs=row(d),
        out_shape=jax.ShapeDtypeStruct((t, d), F32),
        scratch_shapes=[pltpu.VMEM((2, n_copy, d), F32), pltpu.SemaphoreType.DMA((2,))],
        compiler_params=_cparams(1),
        name="moe_combine",
    )(dest, y_rows, x, gates, ln_g, ln_b)


def _dispatch_plan(top_e, n_exp, rows):
    n_assign = top_e.size
    flat_e = top_e.reshape(n_assign)
    onehot = (flat_e[:, None] == jnp.arange(n_exp, dtype=jnp.int32)[None, :]).astype(jnp.int32)
    running = jnp.cumsum(onehot, axis=0)
    rank = jnp.sum(jnp.where(onehot == 1, running - 1, 0), axis=1)
    counts = running[-1]
    padded = (counts + rows - 1) // rows * rows
    pad_end = jnp.cumsum(padded)
    pad_start = pad_end - padded
    dest = (pad_start[flat_e] + rank).astype(jnp.int32)
    n_blk = -(-(n_assign + n_exp * (rows - 1)) // rows)
    row_tok = jnp.zeros((n_blk * rows,), jnp.int32).at[dest].set(
        jnp.arange(n_assign, dtype=jnp.int32) // top_e.shape[1])
    block_e = jnp.minimum(jnp.searchsorted(pad_end, jnp.arange(n_blk, dtype=jnp.int32) * rows, side="right"),
                          n_exp - 1).astype(jnp.int32)
    n_used = (pad_end[-1] // rows).astype(jnp.int32).reshape(1)
    return row_tok.reshape(n_blk, 1, rows), dest, block_e, n_used


def _moe(tok, w_router, b_router, wgu16, b_gu, wd16, b_down, ln_g, ln_b, alpha):
    t, d = tok.shape
    n_exp = w_router.shape[1]
    top_e, gates = _router(tok, w_router, b_router.reshape(1, n_exp))
    row_tok, dest, block_e, n_used = _dispatch_plan(top_e, n_exp, EXPERT_ROWS)
    y_rows = _expert_ffn(tok, row_tok, block_e, n_used, wgu16, b_gu, wd16, b_down)
    steps = t // COMBINE_ROWS
    dest_steps = dest.reshape(steps, COMBINE_ROWS, TOP_K).transpose(0, 2, 1).reshape(steps, 1, TOP_K * COMBINE_ROWS)
    return _combine(y_rows, dest_steps, tok, gates, ln_g, ln_b, alpha)


def _rope_tables(pos, n_heads):
    half = HEAD_DIM // 2
    inv_freq = ROPE_THETA ** (-jnp.arange(half, dtype=F32) / half)
    ang = pos.astype(F32)[:, None] * inv_freq[None, :]
    cos = jnp.cos(ang)
    sin = jnp.sin(ang)
    return (jnp.tile(jnp.concatenate([cos, cos], axis=1), (1, n_heads)),
            jnp.tile(jnp.concatenate([-sin, sin], axis=1), (1, n_heads)))


def kernel(x_prompt, x_sample, mem_prompt, cache_k, cache_v, cache_mem_k, cache_mem_v, page_table,
           w_in, g_moba, g_sb, w_out, ln1_g, ln1_b, w_mq, w_mkv, w_mo, ln2_g, ln2_b,
           w_router, b_router, w_gu, b_gu, w_down, b_down, ln3_g, ln3_b):
    n_seq, seq_len, d = x_prompt.shape
    n_dec, nq, _ = x_sample.shape
    depth = w_in.shape[0]
    gw = w_in.shape[2] // 6
    g_heads = gw // HEAD_DIM
    heads = 2 * g_heads
    n_pages = page_table.shape[1]
    page = cache_k.shape[2]
    past_len = n_pages * page
    n_mem = mem_prompt.shape[1]
    alpha = (2 * depth) ** 0.25
    assert seq_len % MOBA_BLOCK == 0 and past_len % MOBA_BLOCK == 0 and MOBA_BLOCK % page == 0
    assert nq <= NEW_PAD and n_dec % MEM_GROUP == 0 and g_heads % 8 == 0
    assert (n_seq * seq_len + n_dec * nq) % COMBINE_ROWS == 0 and (n_dec * nq) % PROJ_ROWS == 0

    cos_p, sin_p = _rope_tables(jnp.arange(seq_len), g_heads)
    cos_s, sin_s = _rope_tables(jnp.tile(past_len + jnp.arange(nq), n_dec), g_heads)
    n_tok_p = n_seq * seq_len
    n_tok_s = n_dec * nq
    xp = x_prompt.reshape(n_tok_p, d)
    xs = x_sample.reshape(n_tok_s, d)
    row2 = lambda a: a.reshape(1, -1)

    def new_rows(a):
        a = a.reshape(n_dec, nq, 2, g_heads, HEAD_DIM).transpose(0, 2, 1, 3, 4)
        a = jnp.pad(a, ((0, 0), (0, 0), (0, NEW_PAD - nq), (0, 0), (0, 0)))
        return a.reshape(n_dec, 2, NEW_PAD * g_heads, HEAD_DIM)

    def from_head_rows(o):
        return o.reshape(n_dec, g_heads, nq, HEAD_DIM).transpose(0, 2, 1, 3).reshape(n_tok_s, gw)

    outs = [[] for _ in range(6)]
    for l in range(depth):
        w_in16 = w_in[l].astype(BF16)
        w_out16 = w_out[l].astype(BF16)
        q_p, k_p, v_p, k16_p, v16_p, kbar_p = _qkv_project(xp, w_in16, cos_p, sin_p, PROJ_ROWS, True)
        kbar_p = kbar_p.reshape(n_seq, seq_len // MOBA_BLOCK, gw)
        oa_p = _moba_prompt(q_p, k16_p, v16_p, kbar_p, n_seq, seq_len, gw)
        ob_p = _sb_prompt(q_p, k16_p, v16_p, n_seq, seq_len, gw)
        outs[0].append(k_p.reshape(n_seq, seq_len, heads, HEAD_DIM))
        outs[1].append(v_p.reshape(n_seq, seq_len, heads, HEAD_DIM))
        q_s, k_s, v_s, _, _ = _qkv_project(xs, w_in16, cos_s, sin_s, PROJ_ROWS, False)
        q_rows = q_s.reshape(n_dec, nq, heads, HEAD_DIM).transpose(0, 2, 1, 3).reshape(n_dec, heads * nq, HEAD_DIM)
        oa_s, ob_s = _decode_attention(page_table, q_rows, new_rows(k_s), new_rows(v_s),
                                       cache_k[l], cache_v[l], nq)
        outs[2].append(k_s.reshape(n_dec, nq, heads, HEAD_DIM))
        outs[3].append(v_s.reshape(n_dec, nq, heads, HEAD_DIM))
        merge = functools.partial(_merge, g_a=row2(g_moba[l]), g_b=row2(g_sb[l]), w16=w_out16,
                                  ln_g=row2(ln1_g[l]), ln_b=row2(ln1_b[l]), alpha=alpha)
        xp = merge(oa_p, ob_p, xp)
        xs = merge(from_head_rows(oa_s), from_head_rows(ob_s), xs)
        mk, mv, mk16, mv16 = _mem_kv(mem_prompt.reshape(n_seq * n_mem, d), w_mkv[l].astype(BF16))
        outs[4].append(mk.reshape(n_seq, n_mem, MEM_HEADS, d // MEM_HEADS))
        outs[5].append(mv.reshape(n_seq, n_mem, MEM_HEADS, d // MEM_HEADS))
        wq16 = w_mq[l].astype(BF16)
        wo16 = w_mo[l].astype(BF16)
        xp = _mem_attend_prompt(xp, wq16, mk16, mv16, wo16, row2(ln2_g[l]), row2(ln2_b[l]), n_seq, alpha)
        xs = _mem_attend_sample(xs, wq16, cache_mem_k[l].reshape(n_dec, n_mem, d),
                                cache_mem_v[l].reshape(n_dec, n_mem, d), wo16,
                                row2(ln2_g[l]), row2(ln2_b[l]), nq, alpha)
        tok = jnp.concatenate([xp, xs], axis=0)
        tok = _moe(tok, w_router[l], b_router[l], w_gu[l].astype(BF16), b_gu[l], w_down[l].astype(BF16),
                   b_down[l], row2(ln3_g[l]), row2(ln3_b[l]), alpha)
        xp = tok[:n_tok_p]
        xs = tok[n_tok_p:]
    return (xp.reshape(n_seq, seq_len, d), xs.reshape(n_dec, nq, d)) + tuple(jnp.stack(o) for o in outs)
```

```python
import functools

import jax
import jax.numpy as jnp
from jax import lax
from jax.experimental import pallas as pl
from jax.experimental.pallas import tpu as pltpu

F32 = jnp.float32
BF16 = jnp.bfloat16
HIGHEST = lax.Precision.HIGHEST

HEAD_DIM = 64
MOBA_BLOCK = 256
MOBA_TOP_K = 3
Q_BLOCK = 128
ROPE_THETA = 10000.0
MEM_HEADS = 4
TOP_K = 4
SWIGLU_LIMIT = 7.0
SWIGLU_ALPHA = 1.702
LN_EPS = 1e-5
RMS_EPS = 1e-6
ATT_SCALE = HEAD_DIM ** -0.5
NEG_INF = float("-inf")
SB_NEGLIGIBLE = -150.0

PROJ_ROWS = 256
SB_KEYS = 128
MOBA_HEADS = 4
SB_HEADS = 8
DECODE_PAGES = 4
TOKEN_ROWS = 512
EXPERT_ROWS = 256
COMBINE_ROWS = 128
NEW_PAD = 16
OUT_ROWS = 8
MEM_GROUP = 4
DMA_UNROLL = 8
VMEM_LIMIT = 56 * 1024 * 1024

_NT = (((1,), (1,)), ((), ()))


def _cparams(n_axes):
    return pltpu.CompilerParams(dimension_semantics=("arbitrary",) * n_axes,
                                vmem_limit_bytes=VMEM_LIMIT)


def _row_tile(n_rows):
    rows = TOKEN_ROWS
    while n_rows % rows:
        rows //= 2
    assert rows >= 8, n_rows
    return rows


def _idiv(x, n):
    return x >> (n.bit_length() - 1) if n & (n - 1) == 0 else x // n


def _imod(x, n):
    return x & (n - 1) if n & (n - 1) == 0 else x % n


def _layer_norm(x, g, b):
    mu = jnp.mean(x, axis=-1, keepdims=True)
    xc = x - mu
    var = jnp.mean(xc * xc, axis=-1, keepdims=True)
    return xc * lax.rsqrt(var + LN_EPS) * g + b


def _rms_norm(x, g):
    return x * lax.rsqrt(jnp.mean(x * x, axis=-1, keepdims=True) + RMS_EPS) * g


def _log_sigmoid_pair(z):
    t = jnp.log1p(jnp.exp(-jnp.abs(z)))
    return -(jnp.maximum(z, 0.0) + t), jnp.minimum(z, 0.0) - t


def _suffix_sums(xs, upper):
    rows = xs[0].shape[0]
    hi = [x.astype(BF16) for x in xs]
    lo = [(x - xh.astype(F32)).astype(BF16) for x, xh in zip(xs, hi)]
    s = jnp.dot(jnp.concatenate(hi + lo, axis=0), upper, preferred_element_type=F32)
    n = len(xs)
    return [s[i * rows:(i + 1) * rows] + s[(n + i) * rows:(n + i + 1) * rows] for i in range(n)]


def _strict_lower(n):
    r = lax.broadcasted_iota(jnp.int32, (n, n), 0)
    c = lax.broadcasted_iota(jnp.int32, (n, n), 1)
    return jnp.where(r > c, 1.0, 0.0).astype(BF16)


def _top_block_bias(gates, n_valid):
    nb = gates.shape[1]
    n_iota = lax.broadcasted_iota(jnp.int32, gates.shape, 1)
    valid = n_iota < n_valid
    g = jnp.where(valid, gates, NEG_INF)
    cnt = jnp.zeros(gates.shape, jnp.int32)
    for m in range(nb):
        gm = g[:, m:m + 1]
        beats = jnp.where(gm > g, 1, jnp.where(gm == g, jnp.where(n_iota > m, 1, 0), 0))
        cnt = cnt + beats
    sel = jnp.where(valid, jnp.where(cnt < MOBA_TOP_K, 1, 0), 0)
    return jnp.where(sel == 1, 0.0, NEG_INF)


def _columns(cols, width):
    lane = lax.broadcasted_iota(jnp.int32, (cols[0].shape[0], width), 1)
    out = jnp.zeros((cols[0].shape[0], width), F32)
    for n, c in enumerate(cols):
        out = jnp.where(lane == n, c, out)
    return out


def _qkv_kernel(x_ref, wq_ref, wkv_ref, cos_ref, sin_ref, cost_ref, sint_ref,
                q_ref, kt_ref, vt_ref, kt16_ref, vt16_ref, *, gw):
    half = HEAD_DIM // 2
    x16 = x_ref[...].astype(BF16)
    cos = cos_ref[...]
    sin = sin_ref[...]
    lane = lax.broadcasted_iota(jnp.int32, cos.shape, 1)
    first_half = (lane & (HEAD_DIM - 1)) < half
    qa = jnp.dot(x16, wq_ref[:, :gw], preferred_element_type=F32)
    rot = jnp.where(first_half, pltpu.roll(qa, gw - half, 1), pltpu.roll(qa, half, 1))
    q_ref[:, :gw] = qa * cos + rot * sin
    q_ref[:, gw:] = jnp.dot(x16, wq_ref[:, gw:], preferred_element_type=F32)

    def proj_t(g):
        return lax.dot_general(wkv_ref[g * gw:(g + 1) * gw, :], x16, _NT, preferred_element_type=F32)

    def put(ref, ref16, r0, val):
        ref[r0:r0 + val.shape[0], :] = val
        ref16[r0:r0 + val.shape[0], :] = val.astype(BF16)

    kat = proj_t(0)
    cost = cost_ref[...]
    sint = sint_ref[...]
    for h in range(gw // HEAD_DIM):
        x1 = kat[h * HEAD_DIM:h * HEAD_DIM + half]
        x2 = kat[h * HEAD_DIM + half:(h + 1) * HEAD_DIM]
        put(kt_ref, kt16_ref, h * HEAD_DIM, x1 * cost - x2 * sint)
        put(kt_ref, kt16_ref, h * HEAD_DIM + half, x2 * cost + x1 * sint)
    put(kt_ref, kt16_ref, gw, proj_t(1))
    put(vt_ref, vt16_ref, 0, proj_t(2))
    put(vt_ref, vt16_ref, gw, proj_t(3))


def _qkv_project(x, wq16, wkvt16, cos, sin, cost, sint, rows, seq_len):
    t, d = x.shape
    gw = wq16.shape[1] // 2
    tiles = seq_len // rows
    full = lambda a: pl.BlockSpec(a.shape, lambda i: (0, 0))
    row_spec = lambda width: pl.BlockSpec((rows, width), lambda i: (i, 0))
    t_spec = pl.BlockSpec((2 * gw, rows), lambda i: (i // tiles, i % tiles))
    t_shape = lambda dt: jax.ShapeDtypeStruct((t // seq_len * 2 * gw, seq_len), dt)
    return pl.pallas_call(
        functools.partial(_qkv_kernel, gw=gw),
        grid=(t // rows,),
        in_specs=[row_spec(d), full(wq16), full(wkvt16),
                  pl.BlockSpec((rows, gw), lambda i: (i % tiles, 0)),
                  pl.BlockSpec((rows, gw), lambda i: (i % tiles, 0)),
                  pl.BlockSpec((HEAD_DIM // 2, rows), lambda i: (0, i % tiles)),
                  pl.BlockSpec((HEAD_DIM // 2, rows), lambda i: (0, i % tiles))],
        out_specs=[row_spec(2 * gw), t_spec, t_spec, t_spec, t_spec],
        out_shape=[jax.ShapeDtypeStruct((t, 2 * gw), F32), t_shape(F32), t_shape(F32), t_shape(BF16), t_shape(BF16)],
        compiler_params=_cparams(1),
        name="qkv_rope",
    )(x, wq16, wkvt16, cos, sin, cost, sint)


def _moba_prompt_kernel(q_ref, kt_ref, vt_ref, ktf_ref, o_ref, kbar_sc):
    c = pl.program_id(2)
    own = c // (MOBA_BLOCK // Q_BLOCK)
    tq = q_ref.shape[0]
    heads = q_ref.shape[1] // HEAD_DIM
    nb = kt_ref.shape[1] // MOBA_BLOCK

    @pl.when(c == 0)
    def _block_means():
        kbar_sc[...] = _columns([jnp.mean(ktf_ref[:, n * MOBA_BLOCK:(n + 1) * MOBA_BLOCK], axis=1, keepdims=True)
                                 for n in range(nb)], kbar_sc.shape[1])

    row = lax.broadcasted_iota(jnp.int32, (tq, MOBA_BLOCK), 0)
    col = lax.broadcasted_iota(jnp.int32, (tq, MOBA_BLOCK), 1)
    causal = own * MOBA_BLOCK + col <= c * tq + row
    n_iota = lax.broadcasted_iota(jnp.int32, (tq, nb), 1)
    hs = [slice(h * HEAD_DIM, (h + 1) * HEAD_DIM) for h in range(heads)]
    q16, bias = [], []
    for h in range(heads):
        q = q_ref[:, hs[h]]
        gates = jnp.dot(q, kbar_sc[hs[h], :], precision=HIGHEST, preferred_element_type=F32)[:, :nb]
        bias.append(_top_block_bias(gates, own))
        q16.append((q * ATT_SCALE).astype(BF16))

    def scores(j, h):
        keys = pl.ds(pl.multiple_of(j * MOBA_BLOCK, MOBA_BLOCK), MOBA_BLOCK)
        return jnp.dot(q16[h], kt_ref[hs[h], keys], preferred_element_type=F32), vt_ref[hs[h], keys]

    state = []
    for h in range(heads):
        s, vb = scores(own, h)
        s = jnp.where(causal, s, NEG_INF)
        m = jnp.max(s, axis=1, keepdims=True)
        p = jnp.exp(s - m)
        state.append((m, jnp.sum(p, axis=1, keepdims=True),
                      lax.dot_general(p.astype(BF16), vb, _NT, preferred_element_type=F32)))

    def body(j, state):
        sv = [scores(j, h) for h in range(heads)]
        s = [sv[h][0] + jnp.sum(jnp.where(n_iota == j, bias[h], 0.0), axis=1, keepdims=True) for h in range(heads)]
        m_new = [jnp.maximum(state[h][0], jnp.max(s[h], axis=1, keepdims=True)) for h in range(heads)]
        p = [jnp.exp(s[h] - m_new[h]) for h in range(heads)]
        new = []
        for h in range(heads):
            m, l, acc = state[h]
            alpha = jnp.exp(m - m_new[h])
            new.append((m_new[h], alpha * l + jnp.sum(p[h], axis=1, keepdims=True),
                        alpha * acc + lax.dot_general(p[h].astype(BF16), sv[h][1], _NT, preferred_element_type=F32)))
        return tuple(new)

    state = lax.fori_loop(0, own, body, tuple(state))
    for h in range(heads):
        o_ref[:, hs[h]] = state[h][2] / state[h][1]


def _moba_prompt(q, kt16, vt16, kt, n_seq, seq_len, gw):
    width = min(MOBA_HEADS * HEAD_DIM, gw)
    n_groups = gw // width
    nc = seq_len // Q_BLOCK
    kv_spec = pl.BlockSpec((width, seq_len), lambda b, hg, c: (b * 2 * n_groups + hg, 0))
    return pl.pallas_call(
        _moba_prompt_kernel,
        grid=(n_seq, n_groups, nc),
        in_specs=[pl.BlockSpec((Q_BLOCK, width), lambda b, hg, c: (b * nc + c, hg)), kv_spec, kv_spec, kv_spec],
        out_specs=pl.BlockSpec((Q_BLOCK, width), lambda b, hg, c: (b * nc + c, hg)),
        out_shape=jax.ShapeDtypeStruct((n_seq * seq_len, gw), F32),
        scratch_shapes=[pltpu.VMEM((width, 128), F32)],
        compiler_params=_cparams(3),
        name="moba_prompt",
    )(q, kt16, vt16, kt)


def _sb_prompt_kernel(q_ref, kt_ref, vt_ref, o_ref):
    c = pl.program_id(2)
    tq = q_ref.shape[0]
    heads = q_ref.shape[1] // HEAD_DIM
    upper = _strict_lower(SB_KEYS)
    row = lax.broadcasted_iota(jnp.int32, (tq, SB_KEYS), 0)
    col = lax.broadcasted_iota(jnp.int32, (tq, SB_KEYS), 1)
    strict = col < row
    hs = [slice(h * HEAD_DIM, (h + 1) * HEAD_DIM) for h in range(heads)]
    q16 = [(q_ref[:, hs[h]] * ATT_SCALE).astype(BF16) for h in range(heads)]

    def tile(j, state, mask):
        keys = pl.ds(pl.multiple_of(j * SB_KEYS, SB_KEYS), SB_KEYS)
        z = [jnp.dot(q16[h], kt_ref[hs[h], keys], preferred_element_type=F32) for h in range(heads)]
        pairs = [_log_sigmoid_pair(zh) for zh in z]
        lk = [pr[0] if mask is None else jnp.where(mask, pr[0], 0.0) for pr in pairs]
        after = _suffix_sums(lk, upper)
        new = []
        for h in range(heads):
            r, acc = state[h]
            a = jnp.exp(pairs[h][1] + after[h] + r)
            if mask is not None:
                a = jnp.where(mask, a, 0.0)
            acc = acc + lax.dot_general(a.astype(BF16), vt_ref[hs[h], keys], _NT, preferred_element_type=F32)
            new.append((r + jnp.sum(lk[h], axis=1, keepdims=True), acc))
        return tuple(new)

    state = tile(c, tuple((jnp.zeros((tq, 1), F32), jnp.zeros((tq, HEAD_DIM), F32)) for _ in range(heads)), strict)

    def cond(carry):
        i, live, _ = carry
        return jnp.logical_and(i < c, live > 0)

    def body(carry):
        i, _, state = carry
        r_max = state[0][0]
        for h in range(1, heads):
            r_max = jnp.maximum(r_max, state[h][0])
        live = (jnp.max(r_max) > SB_NEGLIGIBLE).astype(jnp.int32)
        return i + 1, live, tile(c - 1 - i, state, None)

    _, _, state = lax.while_loop(cond, body, (jnp.int32(0), jnp.int32(1), state))
    for h in range(heads):
        o_ref[:, hs[h]] = state[h][1]


def _sb_prompt(q, kt16, vt16, n_seq, seq_len, gw):
    width = min(SB_HEADS * HEAD_DIM, gw)
    n_groups = gw // width
    nc = seq_len // SB_KEYS
    kv_spec = pl.BlockSpec((width, seq_len), lambda b, hg, c: (b * 2 * n_groups + n_groups + hg, 0))
    return pl.pallas_call(
        _sb_prompt_kernel,
        grid=(n_seq, n_groups, nc),
        in_specs=[pl.BlockSpec((SB_KEYS, width), lambda b, hg, c: (b * nc + c, n_groups + hg)), kv_spec, kv_spec],
        out_specs=pl.BlockSpec((SB_KEYS, width), lambda b, hg, c: (b * nc + c, hg)),
        out_shape=jax.ShapeDtypeStruct((n_seq * seq_len, gw), F32),
        compiler_params=_cparams(3),
        name="sb_prompt",
    )(q, kt16, vt16)


def _decode_kernel(pt_ref, q_ref, knew_ref, vnew_ref, *refs, n_pages, nq):
    del pt_ref
    page_refs = refs[:4 * DECODE_PAGES]
    oa_ref, ob_ref, ksum_sc, m_sc, l_sc, o_sc, r_sc, accb_sc = refs[4 * DECODE_PAGES:]
    p = pl.program_id(1)
    n_steps = n_pages // DECODE_PAGES
    rows = q_ref.shape[1] // 2
    gh, hd, page = page_refs[0].shape[1:]
    gw = gh * hd
    pn = knew_ref.shape[3]
    pages_per_block = MOBA_BLOCK // page
    blocks_per_step = DECODE_PAGES // pages_per_block
    n_blocks = n_pages // pages_per_block
    lane_w = m_sc.shape[2]

    qa16 = (q_ref[0, :rows, :gw] * ATT_SCALE).astype(BF16)
    qb16 = (q_ref[0, rows:, gw:] * ATT_SCALE).astype(BF16)
    new_query = _imod(lax.broadcasted_iota(jnp.int32, (rows, pn), 0), nq)
    new_key = lax.broadcasted_iota(jnp.int32, (rows, pn), 1)

    def mat(ref):
        return ref[0].reshape(gw, page).astype(BF16)

    @pl.when(p == 0)
    def _init():
        z = jnp.dot(qb16, knew_ref[0, 1].astype(BF16), preferred_element_type=F32)
        mask = new_key < new_query
        lk, ls = _log_sigmoid_pair(z)
        lk = jnp.where(mask, lk, 0.0)
        a = jnp.where(mask, jnp.exp(ls + _suffix_sums([lk], _strict_lower(pn))[0]), 0.0)
        accb_sc[...] = lax.dot_general(a.astype(BF16), vnew_ref[0, 1].astype(BF16), _NT, preferred_element_type=F32)
        r_sc[...] = jnp.broadcast_to(jnp.sum(lk, axis=1, keepdims=True), r_sc.shape)

    for blk in range(blocks_per_step):
        n = n_blocks - 1 - (p * blocks_per_step + blk)
        pages = range(blk * pages_per_block, (blk + 1) * pages_per_block)
        s = [jnp.dot(qa16, mat(page_refs[4 * j]), preferred_element_type=F32) for j in pages]
        m_blk = functools.reduce(jnp.maximum, [jnp.max(sj, axis=1, keepdims=True) for sj in s])
        pr = [jnp.exp(sj - m_blk) for sj in s]
        l_blk = functools.reduce(jnp.add, [jnp.sum(pj, axis=1, keepdims=True) for pj in pr])
        o_blk = functools.reduce(jnp.add, [
            lax.dot_general(pj.astype(BF16), mat(page_refs[4 * j + 2]), _NT, preferred_element_type=F32)
            for pj, j in zip(pr, pages)])
        m_sc[n] = jnp.broadcast_to(m_blk, (rows, lane_w))
        l_sc[n] = jnp.broadcast_to(l_blk, (rows, lane_w))
        o_sc[n] = o_blk
        ksum_sc[n] = functools.reduce(jnp.add, [page_refs[4 * j][0].reshape(gw, page) for j in pages])

    z = [jnp.dot(qb16, mat(page_refs[4 * j + 1]), preferred_element_type=F32) for j in range(DECODE_PAGES)]
    pairs = [_log_sigmoid_pair(zj) for zj in z]
    after = _suffix_sums([pr[0] for pr in pairs], _strict_lower(page))
    r = r_sc[:, :1]
    acc = accb_sc[...]
    for j in range(DECODE_PAGES):
        a = jnp.exp(pairs[j][1] + after[j] + r)
        acc = acc + lax.dot_general(a.astype(BF16), mat(page_refs[4 * j + 3]), _NT, preferred_element_type=F32)
        r = r + jnp.sum(pairs[j][0], axis=1, keepdims=True)
    accb_sc[...] = acc
    r_sc[...] = jnp.broadcast_to(r, r_sc.shape)

    @pl.when(p == n_steps - 1)
    def _finish():
        kbar = _columns([jnp.sum(ksum_sc[b], axis=1, keepdims=True) * (1.0 / MOBA_BLOCK) for b in range(n_blocks)],
                        lane_w)
        gates = jnp.dot(q_ref[0, :rows, :gw], kbar, precision=HIGHEST, preferred_element_type=F32)[:, :n_blocks]
        bias = _top_block_bias(gates, n_blocks)
        sn = jnp.dot(qa16, knew_ref[0, 0].astype(BF16), preferred_element_type=F32)
        sn = jnp.where(new_key <= new_query, sn, NEG_INF)
        m_own = jnp.max(sn, axis=1, keepdims=True)
        pn_ = jnp.exp(sn - m_own)
        l_own = jnp.sum(pn_, axis=1, keepdims=True)
        o_own = lax.dot_general(pn_.astype(BF16), vnew_ref[0, 0].astype(BF16), _NT, preferred_element_type=F32)
        m_all = m_own
        for b in range(n_blocks):
            m_all = jnp.maximum(m_all, m_sc[b][:, :1] + bias[:, b:b + 1])
        w_own = jnp.exp(m_own - m_all)
        num = w_own * o_own
        den = w_own * l_own
        for b in range(n_blocks):
            w = jnp.exp(m_sc[b][:, :1] + bias[:, b:b + 1] - m_all)
            num = num + w * o_sc[b]
            den = den + w * l_sc[b][:, :1]
        outa = num / den
        rr = lax.broadcasted_iota(jnp.int32, (rows, gw), 0)
        cc = lax.broadcasted_iota(jnp.int32, (rows, gw), 1)
        diag = _idiv(rr, nq) == _idiv(cc, hd)
        si = lax.broadcasted_iota(jnp.int32, (oa_ref.shape[1], rows), 0)
        sr = lax.broadcasted_iota(jnp.int32, (oa_ref.shape[1], rows), 1)
        pick = jnp.where(_imod(sr, nq) == si, 1.0, 0.0)
        oa_ref[0] = jnp.dot(pick, jnp.where(diag, outa, 0.0), precision=HIGHEST, preferred_element_type=F32)
        ob_ref[0] = jnp.dot(pick, jnp.where(diag, accb_sc[...], 0.0), precision=HIGHEST,
                            preferred_element_type=F32)


def _decode_attention(page_table, qbd, knew, vnew, cache_kt, cache_vt, nq):
    n_dec, n_pages = page_table.shape
    rows = qbd.shape[1] // 2
    _, heads, hd, page = cache_kt.shape
    gh = heads // 2
    gw = gh * hd
    n_blocks = n_pages * page // MOBA_BLOCK
    per_seq = lambda a: pl.BlockSpec((1,) + a.shape[1:], lambda b, p, pt: (b,) + (0,) * (a.ndim - 1))
    page_spec = lambda j, g: pl.BlockSpec(
        (1, gh, hd, page), lambda b, p, pt: (pt[b, n_pages - 1 - (p * DECODE_PAGES + j)], g, 0, 0))
    out_spec = pl.BlockSpec((1, OUT_ROWS, gw), lambda b, p, pt: (b, 0, 0))
    assert n_pages % DECODE_PAGES == 0 and DECODE_PAGES % (MOBA_BLOCK // page) == 0 and nq <= OUT_ROWS
    grid_spec = pltpu.PrefetchScalarGridSpec(
        num_scalar_prefetch=1,
        grid=(n_dec, n_pages // DECODE_PAGES),
        in_specs=[per_seq(qbd), per_seq(knew), per_seq(vnew)]
        + [page_spec(j, g) for j in range(DECODE_PAGES) for g in (0, 1, 0, 1)],
        out_specs=[out_spec, out_spec],
        scratch_shapes=[pltpu.VMEM((n_blocks, gw, page), F32),
                        pltpu.VMEM((n_blocks, rows, 128), F32),
                        pltpu.VMEM((n_blocks, rows, 128), F32),
                        pltpu.VMEM((n_blocks, rows, gw), F32),
                        pltpu.VMEM((rows, 128), F32),
                        pltpu.VMEM((rows, gw), F32)],
    )
    return pl.pallas_call(
        functools.partial(_decode_kernel, n_pages=n_pages, nq=nq),
        grid_spec=grid_spec,
        out_shape=[jax.ShapeDtypeStruct((n_dec, OUT_ROWS, gw), F32)] * 2,
        compiler_params=_cparams(2),
        name="decode_attention",
    )(page_table, qbd, knew, vnew, *([cache_kt, cache_kt, cache_vt, cache_vt] * DECODE_PAGES))


def _merge_kernel(oa_ref, ob_ref, x_ref, ga_ref, gb_ref, w_ref, g_ref, b_ref, y_ref, *, alpha):
    gw = oa_ref.shape[1]
    ya = _rms_norm(oa_ref[...], ga_ref[...]).astype(BF16)
    yb = _rms_norm(ob_ref[...], gb_ref[...]).astype(BF16)
    mix = (jnp.dot(ya, w_ref[:gw, :], preferred_element_type=F32)
           + jnp.dot(yb, w_ref[gw:, :], preferred_element_type=F32))
    y_ref[...] = _layer_norm(alpha * x_ref[...] + mix, g_ref[...], b_ref[...])


def _merge(oa, ob, x, g_a, g_b, w16, ln_g, ln_b, alpha):
    t, d = x.shape
    gw = oa.shape[1]
    rows = _row_tile(t)
    row = lambda width: pl.BlockSpec((rows, width), lambda i: (i, 0))
    full = lambda a: pl.BlockSpec(a.shape, lambda i: (0, 0))
    return pl.pallas_call(
        functools.partial(_merge_kernel, alpha=alpha),
        grid=(t // rows,),
        in_specs=[row(gw), row(gw), row(d), full(g_a), full(g_b), full(w16), full(ln_g), full(ln_b)],
        out_specs=row(d),
        out_shape=jax.ShapeDtypeStruct((t, d), F32),
        compiler_params=_cparams(1),
        name="merge_out_proj",
    )(oa, ob, x, g_a, g_b, w16, ln_g, ln_b)


def _mem_kv_kernel(m_ref, w_ref, k_ref, v_ref, k16_ref, v16_ref):
    width = k_ref.shape[1]
    m16 = m_ref[...].astype(BF16)
    k = jnp.dot(m16, w_ref[:, :width], preferred_element_type=F32)
    v = jnp.dot(m16, w_ref[:, width:], preferred_element_type=F32)
    k_ref[...] = k
    v_ref[...] = v
    k16_ref[...] = k.astype(BF16)
    v16_ref[...] = v.astype(BF16)


def _mem_kv(mem, w16):
    t, d = mem.shape
    width = w16.shape[1] // 2
    rows = _row_tile(t)
    row = lambda w: pl.BlockSpec((rows, w), lambda i: (i, 0))
    return pl.pallas_call(
        _mem_kv_kernel,
        grid=(t // rows,),
        in_specs=[row(d), pl.BlockSpec(w16.shape, lambda i: (0, 0))],
        out_specs=[row(width)] * 4,
        out_shape=[jax.ShapeDtypeStruct((t, width), F32)] * 2 + [jax.ShapeDtypeStruct((t, width), BF16)] * 2,
        compiler_params=_cparams(1),
        name="mem_kv",
    )(mem, w16)


def _mem_heads(q16, head_kv, row_mask=None):
    hd = q16.shape[1] // MEM_HEADS
    outs = []
    for h in range(MEM_HEADS):
        k16, v16 = head_kv(h)
        s = lax.dot_general(q16[:, h * hd:(h + 1) * hd], k16, _NT, preferred_element_type=F32)
        m = jnp.max(s, axis=1, keepdims=True)
        p = jnp.exp(s - m)
        l = jnp.sum(p, axis=1, keepdims=True)
        o = jnp.dot(p.astype(BF16), v16, preferred_element_type=F32) / l
        outs.append(o if row_mask is None else jnp.where(row_mask, o, 0.0))
    return jnp.concatenate(outs, axis=1)


def _mem_attend_kernel(x_ref, wq_ref, mk_ref, mv_ref, wo_ref, g_ref, b_ref, y_ref, *, alpha):
    x = x_ref[...]
    hd = x.shape[1] // MEM_HEADS
    q16 = (jnp.dot(x.astype(BF16), wq_ref[...], preferred_element_type=F32) * hd ** -0.5).astype(BF16)
    o = _mem_heads(q16, lambda h: (mk_ref[:, h * hd:(h + 1) * hd], mv_ref[:, h * hd:(h + 1) * hd]))
    y = jnp.dot(o.astype(BF16), wo_ref[...], preferred_element_type=F32)
    y_ref[...] = _layer_norm(alpha * x + y, g_ref[...], b_ref[...])


def _mem_attend_prompt(x, wq16, mk16, mv16, wo16, ln_g, ln_b, n_seq, alpha):
    t, d = x.shape
    seq_len = t // n_seq
    n_mem = mk16.shape[0] // n_seq
    rows = _row_tile(seq_len)
    tiles = seq_len // rows
    row = pl.BlockSpec((rows, d), lambda b, i: (b * tiles + i, 0))
    full = lambda a: pl.BlockSpec(a.shape, lambda b, i: (0, 0))
    mem = pl.BlockSpec((n_mem, d), lambda b, i: (b, 0))
    return pl.pallas_call(
        functools.partial(_mem_attend_kernel, alpha=alpha),
        grid=(n_seq, tiles),
        in_specs=[row, full(wq16), mem, mem, full(wo16), full(ln_g), full(ln_b)],
        out_specs=row,
        out_shape=jax.ShapeDtypeStruct((t, d), F32),
        compiler_params=_cparams(2),
        name="mem_attend_prompt",
    )(x, wq16, mk16, mv16, wo16, ln_g, ln_b)


def _mem_attend_sample_kernel(x_ref, wq_ref, mk_ref, mv_ref, wo_ref, g_ref, b_ref, y_ref, *, alpha, nq):
    x = x_ref[...]
    hd = x.shape[1] // MEM_HEADS
    q16 = (jnp.dot(x.astype(BF16), wq_ref[...], preferred_element_type=F32) * hd ** -0.5).astype(BF16)
    seq_of_row = _idiv(lax.broadcasted_iota(jnp.int32, (x.shape[0], 1), 0), nq)
    o = jnp.zeros(x.shape, F32)
    for g in range(mk_ref.shape[0]):
        head_kv = lambda h, g=g: (mk_ref[g, :, h * hd:(h + 1) * hd].astype(BF16),
                                  mv_ref[g, :, h * hd:(h + 1) * hd].astype(BF16))
        o = o + _mem_heads(q16, head_kv, seq_of_row == g)
    y = jnp.dot(o.astype(BF16), wo_ref[...], preferred_element_type=F32)
    y_ref[...] = _layer_norm(alpha * x + y, g_ref[...], b_ref[...])


def _mem_attend_sample(x, wq16, cache_mk, cache_mv, wo16, ln_g, ln_b, nq, alpha):
    t, d = x.shape
    n_dec, n_mem, _ = cache_mk.shape
    rows = MEM_GROUP * nq
    row = pl.BlockSpec((rows, d), lambda i: (i, 0))
    full = lambda a: pl.BlockSpec(a.shape, lambda i: (0, 0))
    mem = pl.BlockSpec((MEM_GROUP, n_mem, d), lambda i: (i, 0, 0))
    return pl.pallas_call(
        functools.partial(_mem_attend_sample_kernel, alpha=alpha, nq=nq),
        grid=(n_dec // MEM_GROUP,),
        in_specs=[row, full(wq16), mem, mem, full(wo16), full(ln_g), full(ln_b)],
        out_specs=row,
        out_shape=jax.ShapeDtypeStruct((t, d), F32),
        compiler_params=_cparams(1),
        name="mem_attend_sample",
    )(x, wq16, cache_mk, cache_mv, wo16, ln_g, ln_b)


def _router_kernel(x_ref, w_ref, b_ref, e_ref, g_ref, rank_ref, cnt_ref, run_sc):
    rows = x_ref.shape[0]
    logits = jnp.dot(x_ref[...], w_ref[...], precision=HIGHEST, preferred_element_type=F32) + b_ref[...]
    n_exp = logits.shape[1]
    e_iota = lax.broadcasted_iota(jnp.int32, logits.shape, 1)
    k_iota = lax.broadcasted_iota(jnp.int32, e_ref.shape, 1)
    top_e = jnp.zeros(e_ref.shape, jnp.int32)
    top_v = jnp.zeros(e_ref.shape, F32)
    onehot = []
    for k in range(TOP_K):
        mx = jnp.max(logits, axis=1, keepdims=True)
        idx = jnp.min(jnp.where(logits == mx, e_iota, n_exp), axis=1, keepdims=True)
        top_e = jnp.where(k_iota == k, idx, top_e)
        top_v = jnp.where(k_iota == k, mx, top_v)
        onehot.append(jnp.where(e_iota == idx, 1.0, 0.0))
        logits = jnp.where(e_iota == idx, NEG_INF, logits)
    w = jnp.exp(top_v - top_v[:, :1])
    e_ref[...] = top_e
    g_ref[...] = w / jnp.sum(w, axis=1, keepdims=True)

    @pl.when(pl.program_id(0) == 0)
    def _zero():
        run_sc[...] = jnp.zeros(run_sc.shape, F32)

    chosen = functools.reduce(jnp.add, onehot)
    before = jnp.dot(_strict_lower(rows), chosen.astype(BF16), preferred_element_type=F32) + run_sc[...]
    rank = jnp.zeros(e_ref.shape, F32)
    for k in range(TOP_K):
        rank = jnp.where(k_iota == k, jnp.sum(onehot[k] * before, axis=1, keepdims=True), rank)
    rank_ref[...] = rank.astype(jnp.int32)
    run_sc[...] = run_sc[...] + jnp.sum(chosen, axis=0, keepdims=True)
    cnt_ref[...] = run_sc[...]


def _router(x, w_router, b_router):
    t, d = x.shape
    n_exp = w_router.shape[1]
    rows = _row_tile(t)
    out = pl.BlockSpec((rows, TOP_K), lambda i: (i, 0))
    return pl.pallas_call(
        _router_kernel,
        grid=(t // rows,),
        in_specs=[pl.BlockSpec((rows, d), lambda i: (i, 0)),
                  pl.BlockSpec(w_router.shape, lambda i: (0, 0)),
                  pl.BlockSpec(b_router.shape, lambda i: (0, 0))],
        out_specs=[out, out, out, pl.BlockSpec((1, n_exp), lambda i: (0, 0))],
        out_shape=[jax.ShapeDtypeStruct((t, TOP_K), jnp.int32), jax.ShapeDtypeStruct((t, TOP_K), F32),
                   jax.ShapeDtypeStruct((t, TOP_K), jnp.int32), jax.ShapeDtypeStruct((1, n_exp), F32)],
        scratch_shapes=[pltpu.VMEM((1, n_exp), F32)],
        compiler_params=_cparams(1),
        name="router_top4",
    )(x, w_router, b_router)


def _row_copy(src_hbm, src_row, dst, r, sem):
    return pltpu.make_async_copy(src_hbm.at[pl.ds(src_row, 1)], dst.at[pl.ds(r, 1)], sem)


def _row_gather(src_hbm, idx_ref, dst, sem, n_rows):
    for r in range(n_rows):
        _row_copy(src_hbm, idx_ref[0, 0, r], dst, r, sem).start()


def _row_gather_wait(src_hbm, dst, sem, n_rows):
    def wait(r, carry):
        _row_copy(src_hbm, 0, dst, r, sem).wait()
        return carry
    lax.fori_loop(0, n_rows, wait, 0, unroll=DMA_UNROLL)


def _expert_ffn_kernel(be_ref, nu_ref, idx_ref, x_hbm, wgu_ref, bgu_ref, wd_ref, bd_ref, y_ref,
                       xbuf, sems, wgu16, wd16):
    s = pl.program_id(0)
    n_used = nu_ref[0]
    rows = y_ref.shape[0]

    @pl.when(s < n_used)
    def _fetch():
        _row_gather(x_hbm, idx_ref, xbuf.at[s % 2], sems.at[s % 2], rows)

    active = jnp.logical_and(s >= 1, s <= n_used)
    new_expert = jnp.logical_or(s == 1, be_ref[jnp.maximum(s - 1, 0)] != be_ref[jnp.maximum(s - 2, 0)])

    @pl.when(jnp.logical_and(active, new_expert))
    def _cast_weights():
        wgu16[...] = wgu_ref[0].astype(BF16)
        wd16[...] = wd_ref[0].astype(BF16)

    @pl.when(active)
    def _compute():
        slot = (s - 1) % 2
        _row_gather_wait(x_hbm, xbuf.at[slot], sems.at[slot], rows)
        f = wd_ref.shape[1]
        x16 = xbuf[slot].astype(BF16)
        hgu = jnp.dot(x16, wgu16[...], preferred_element_type=F32) + bgu_ref[0]
        gate = jnp.minimum(hgu[:, :f], SWIGLU_LIMIT)
        up = jnp.clip(hgu[:, f:], -SWIGLU_LIMIT, SWIGLU_LIMIT)
        act = (up + 1.0) * gate * jax.nn.sigmoid(SWIGLU_ALPHA * gate)
        y_ref[...] = jnp.dot(act.astype(BF16), wd16[...], preferred_element_type=F32) + bd_ref[0]

    @pl.when(s > n_used)
    def _unused():
        y_ref[...] = jnp.zeros(y_ref.shape, F32)


def _expert_ffn(x, row_tok, block_e, n_used, w_gu, b_gu, w_down, b_down):
    n_blk, _, rows = row_tok.shape
    n_exp, d, f2 = w_gu.shape
    f = f2 // 2
    prev = lambda s: jnp.maximum(s - 1, 0)
    by_expert = lambda shape: pl.BlockSpec(shape, lambda s, be, nu: (be[prev(s)], 0, 0))
    grid_spec = pltpu.PrefetchScalarGridSpec(
        num_scalar_prefetch=2,
        grid=(n_blk + 1,),
        in_specs=[pl.BlockSpec((1, 1, rows), lambda s, be, nu: (jnp.minimum(s, n_blk - 1), 0, 0),
                               memory_space=pltpu.SMEM),
                  pl.BlockSpec(memory_space=pl.ANY),
                  by_expert((1, d, f2)), by_expert((1, 1, f2)), by_expert((1, f, d)), by_expert((1, 1, d))],
        out_specs=pl.BlockSpec((rows, d), lambda s, be, nu: (prev(s), 0)),
        scratch_shapes=[pltpu.VMEM((2, rows, d), F32), pltpu.SemaphoreType.DMA((2,)),
                        pltpu.VMEM((d, f2), BF16), pltpu.VMEM((f, d), BF16)],
    )
    return pl.pallas_call(
        _expert_ffn_kernel,
        grid_spec=grid_spec,
        out_shape=jax.ShapeDtypeStruct((n_blk * rows, d), F32),
        compiler_params=_cparams(1),
        name="expert_ffn",
    )(block_e, n_used, row_tok, x, w_gu, b_gu.reshape(n_exp, 1, f2), w_down, b_down.reshape(n_exp, 1, d))


def _combine_kernel(idx_ref, y_hbm, x_ref, gate_ref, g_ref, b_ref, o_ref, ybuf, sems, *, alpha):
    s = pl.program_id(0)
    n_steps = pl.num_programs(0) - 1
    rows = x_ref.shape[0]
    n_copy = TOP_K * rows

    @pl.when(s < n_steps)
    def _fetch():
        _row_gather(y_hbm, idx_ref, ybuf.at[s % 2], sems.at[s % 2], n_copy)

    @pl.when(s >= 1)
    def _compute():
        slot = (s - 1) % 2
        _row_gather_wait(y_hbm, ybuf.at[slot], sems.at[slot], n_copy)
        gates = gate_ref[...]
        ffn = jnp.zeros(x_ref.shape, F32)
        for k in range(TOP_K):
            ffn = ffn + gates[:, k:k + 1] * ybuf[slot, k * rows:(k + 1) * rows, :]
        o_ref[...] = _layer_norm(alpha * x_ref[...] + ffn, g_ref[...], b_ref[...])


def _combine(y_rows, dest, x, gates, ln_g, ln_b, alpha):
    t, d = x.shape
    n_steps, _, n_copy = dest.shape
    rows = n_copy // TOP_K
    idx_spec = pl.BlockSpec((1, 1, n_copy), lambda s: (jnp.minimum(s, n_steps - 1), 0, 0),
                            memory_space=pltpu.SMEM)
    row = lambda w: pl.BlockSpec((rows, w), lambda s: (jnp.maximum(s - 1, 0), 0))
    full = lambda a: pl.BlockSpec(a.shape, lambda s: (0, 0))
    return pl.pallas_call(
        functools.partial(_combine_kernel, alpha=alpha),
        grid=(n_steps + 1,),
        in_specs=[idx_spec, pl.BlockSpec(memory_space=pl.ANY), row(d), row(TOP_K), full(ln_g), full(ln_b)],
        out_specs=row(d),
        out_shape=jax.ShapeDtypeStruct((t, d), F32),
        scratch_shapes=[pltpu.VMEM((2, n_copy, d), F32), pltpu.SemaphoreType.DMA((2,))],
        compiler_params=_cparams(1),
        name="moe_combine",
    )(dest, y_rows, x, gates, ln_g, ln_b)


def _dispatch_plan(top_e, rank, counts, rows):
    n_assign = top_e.size
    n_exp = counts.shape[0]
    flat_e = top_e.reshape(n_assign)
    padded = (counts + rows - 1) // rows * rows
    pad_end = jnp.cumsum(padded)
    pad_start = pad_end - padded
    dest = (pad_start[flat_e] + rank.reshape(n_assign)).astype(jnp.int32)
    n_blk = -(-(n_assign + n_exp * (rows - 1)) // rows)
    row_tok = jnp.zeros((n_blk * rows,), jnp.int32).at[dest].set(
        jnp.arange(n_assign, dtype=jnp.int32) // top_e.shape[1])
    block_e = jnp.minimum(jnp.searchsorted(pad_end, jnp.arange(n_blk, dtype=jnp.int32) * rows, side="right"),
                          n_exp - 1).astype(jnp.int32)
    n_used = (pad_end[-1] // rows).astype(jnp.int32).reshape(1)
    return row_tok.reshape(n_blk, 1, rows), dest, block_e, n_used


def _moe(tok, w_router, b_router, w_gu, b_gu, w_down, b_down, ln_g, ln_b, alpha):
    t, d = tok.shape
    n_exp = w_router.shape[1]
    top_e, gates, rank, counts = _router(tok, w_router, b_router.reshape(1, n_exp))
    row_tok, dest, block_e, n_used = _dispatch_plan(top_e, rank, counts[0].astype(jnp.int32), EXPERT_ROWS)
    y_rows = _expert_ffn(tok, row_tok, block_e, n_used, w_gu, b_gu, w_down, b_down)
    steps = t // COMBINE_ROWS
    dest_steps = dest.reshape(steps, COMBINE_ROWS, TOP_K).transpose(0, 2, 1).reshape(steps, 1, TOP_K * COMBINE_ROWS)
    return _combine(y_rows, dest_steps, tok, gates, ln_g, ln_b, alpha)


def _rope_tables(pos, n_heads):
    half = HEAD_DIM // 2
    inv_freq = ROPE_THETA ** (-jnp.arange(half, dtype=F32) / half)
    ang = pos.astype(F32)[:, None] * inv_freq[None, :]
    cos = jnp.cos(ang)
    sin = jnp.sin(ang)
    return (jnp.tile(jnp.concatenate([cos, cos], axis=1), (1, n_heads)),
            jnp.tile(jnp.concatenate([-sin, sin], axis=1), (1, n_heads)), cos.T, sin.T)


def _block_diag_queries(q, nq, n_heads):
    n_dec = q.shape[0] // nq
    width = n_heads * HEAD_DIM
    rows = jnp.tile(q.reshape(n_dec, 1, nq, width), (1, n_heads, 1, 1)).reshape(n_dec, n_heads * nq, width)
    r = jnp.arange(n_heads * nq)[:, None] // nq
    c = jnp.arange(width)[None, :] // HEAD_DIM
    return jnp.where((r == c)[None], rows, 0.0)


def kernel(x_prompt, x_sample, mem_prompt, cache_k, cache_v, cache_mem_k, cache_mem_v, page_table,
           w_in, g_moba, g_sb, w_out, ln1_g, ln1_b, w_mq, w_mkv, w_mo, ln2_g, ln2_b,
           w_router, b_router, w_gu, b_gu, w_down, b_down, ln3_g, ln3_b):
    n_seq, seq_len, d = x_prompt.shape
    n_dec, nq, _ = x_sample.shape
    depth = w_in.shape[0]
    gw = w_in.shape[2] // 6
    g_heads = gw // HEAD_DIM
    heads = 2 * g_heads
    n_pages = page_table.shape[1]
    page = cache_k.shape[2]
    past_len = n_pages * page
    n_mem = mem_prompt.shape[1]
    alpha = (2 * depth) ** 0.25
    n_tok_p = n_seq * seq_len
    n_tok_s = n_dec * nq
    assert seq_len % MOBA_BLOCK == 0 and past_len % MOBA_BLOCK == 0 and MOBA_BLOCK % page == 0
    assert nq <= NEW_PAD and n_dec % MEM_GROUP == 0
    assert (n_tok_p + n_tok_s) % COMBINE_ROWS == 0 and n_tok_s % PROJ_ROWS == 0 and seq_len % PROJ_ROWS == 0

    tables_p = _rope_tables(jnp.arange(seq_len), g_heads)
    tables_s = _rope_tables(jnp.tile(past_len + jnp.arange(nq), n_dec), g_heads)
    xp = x_prompt.reshape(n_tok_p, d)
    xs = x_sample.reshape(n_tok_s, d)
    row2 = lambda a: a.reshape(1, -1)

    def to_positions(at, n, length):
        return at.reshape(n, heads, HEAD_DIM, length).transpose(0, 3, 1, 2)

    def new_columns(at):
        a = at.reshape(2, gw, n_dec, nq).transpose(2, 0, 1, 3)
        return jnp.pad(a, ((0, 0), (0, 0), (0, 0), (0, NEW_PAD - nq)))

    outs = [[] for _ in range(6)]
    for l in range(depth):
        w = w_in[l].astype(BF16)
        col = lambda g: w[:, g * gw:(g + 1) * gw]
        wq16 = jnp.concatenate([col(0), col(3)], axis=1)
        wkvt16 = jnp.concatenate([col(1), col(4), col(2), col(5)], axis=1).T
        w_out16 = w_out[l].astype(BF16)
        q_p, kt_p, vt_p, kt16_p, vt16_p = _qkv_project(xp, wq16, wkvt16, *tables_p, PROJ_ROWS, seq_len)
        oa_p = _moba_prompt(q_p, kt16_p, vt16_p, kt_p, n_seq, seq_len, gw)
        ob_p = _sb_prompt(q_p, kt16_p, vt16_p, n_seq, seq_len, gw)
        outs[0].append(to_positions(kt_p, n_seq, seq_len))
        outs[1].append(to_positions(vt_p, n_seq, seq_len))
        q_s, kt_s, vt_s, _, _ = _qkv_project(xs, wq16, wkvt16, *tables_s, PROJ_ROWS, n_tok_s)
        oa_s, ob_s = _decode_attention(
            page_table, _block_diag_queries(q_s, nq, heads), new_columns(kt_s), new_columns(vt_s),
            cache_k[l].transpose(0, 2, 3, 1), cache_v[l].transpose(0, 2, 3, 1), nq)
        outs[2].append(to_positions(kt_s, 1, n_tok_s).reshape(n_dec, nq, heads, HEAD_DIM))
        outs[3].append(to_positions(vt_s, 1, n_tok_s).reshape(n_dec, nq, heads, HEAD_DIM))
        merge = functools.partial(_merge, g_a=row2(g_moba[l]), g_b=row2(g_sb[l]), w16=w_out16,
                                  ln_g=row2(ln1_g[l]), ln_b=row2(ln1_b[l]), alpha=alpha)
        xp = merge(oa_p, ob_p, xp)
        xs = merge(oa_s[:, :nq].reshape(n_tok_s, gw), ob_s[:, :nq].reshape(n_tok_s, gw), xs)
        mk, mv, mk16, mv16 = _mem_kv(mem_prompt.reshape(n_seq * n_mem, d), w_mkv[l].astype(BF16))
        outs[4].append(mk.reshape(n_seq, n_mem, MEM_HEADS, d // MEM_HEADS))
        outs[5].append(mv.reshape(n_seq, n_mem, MEM_HEADS, d // MEM_HEADS))
        wmq16 = w_mq[l].astype(BF16)
        wmo16 = w_mo[l].astype(BF16)
        xp = _mem_attend_prompt(xp, wmq16, mk16, mv16, wmo16, row2(ln2_g[l]), row2(ln2_b[l]), n_seq, alpha)
        xs = _mem_attend_sample(xs, wmq16, cache_mem_k[l].reshape(n_dec, n_mem, d),
                                cache_mem_v[l].reshape(n_dec, n_mem, d), wmo16,
                                row2(ln2_g[l]), row2(ln2_b[l]), nq, alpha)
        tok = jnp.concatenate([xp, xs], axis=0)
        tok = _moe(tok, w_router[l], b_router[l], w_gu[l], b_gu[l], w_down[l],
                   b_down[l], row2(ln3_g[l]), row2(ln3_b[l]), alpha)
        xp = tok[:n_tok_p]
        xs = tok[n_tok_p:]
    return (xp.reshape(n_seq, seq_len, d), xs.reshape(n_dec, nq, d)) + tuple(jnp.stack(o) for o in outs)
```

```python
import functools

import jax
import jax.numpy as jnp
from jax import lax
from jax.experimental import pallas as pl
from jax.experimental.pallas import tpu as pltpu

F32 = jnp.float32
BF16 = jnp.bfloat16
HIGHEST = lax.Precision.HIGHEST

HEAD_DIM = 64
MOBA_BLOCK = 256
MOBA_TOP_K = 3
Q_BLOCK = 128
ROPE_THETA = 10000.0
MEM_HEADS = 4
TOP_K = 4
SWIGLU_LIMIT = 7.0
SWIGLU_ALPHA = 1.702
LN_EPS = 1e-5
RMS_EPS = 1e-6
ATT_SCALE = HEAD_DIM ** -0.5
NEG_INF = float("-inf")
SB_NEGLIGIBLE = -150.0

PROJ_ROWS = 256
SB_KEYS = 128
MOBA_HEADS = 4
SB_HEADS = 8
DECODE_PAGES = 4
TOKEN_ROWS = 512
EXPERT_ROWS = 256
COMBINE_ROWS = 128
NEW_PAD = 16
OUT_ROWS = 8
MEM_GROUP = 4
DMA_UNROLL = 8
VMEM_LIMIT = 56 * 1024 * 1024

_NT = (((1,), (1,)), ((), ()))


def _cparams(n_axes):
    return pltpu.CompilerParams(dimension_semantics=("arbitrary",) * n_axes,
                                vmem_limit_bytes=VMEM_LIMIT)


def _row_tile(n_rows):
    rows = TOKEN_ROWS
    while n_rows % rows:
        rows //= 2
    assert rows >= 8, n_rows
    return rows


def _idiv(x, n):
    return x >> (n.bit_length() - 1) if n & (n - 1) == 0 else x // n


def _imod(x, n):
    return x & (n - 1) if n & (n - 1) == 0 else x % n


def _layer_norm(x, g, b):
    mu = jnp.mean(x, axis=-1, keepdims=True)
    xc = x - mu
    var = jnp.mean(xc * xc, axis=-1, keepdims=True)
    return xc * lax.rsqrt(var + LN_EPS) * g + b


def _rms_norm(x, g):
    return x * lax.rsqrt(jnp.mean(x * x, axis=-1, keepdims=True) + RMS_EPS) * g


def _log_sigmoid_pair(z):
    t = jnp.log1p(jnp.exp(-jnp.abs(z)))
    return -(jnp.maximum(z, 0.0) + t), jnp.minimum(z, 0.0) - t


def _suffix_sums(xs, upper):
    rows = xs[0].shape[0]
    hi = [x.astype(BF16) for x in xs]
    lo = [(x - xh.astype(F32)).astype(BF16) for x, xh in zip(xs, hi)]
    s = jnp.dot(jnp.concatenate(hi + lo, axis=0), upper, preferred_element_type=F32)
    n = len(xs)
    return [s[i * rows:(i + 1) * rows] + s[(n + i) * rows:(n + i + 1) * rows] for i in range(n)]


def _strict_lower(n):
    r = lax.broadcasted_iota(jnp.int32, (n, n), 0)
    c = lax.broadcasted_iota(jnp.int32, (n, n), 1)
    return jnp.where(r > c, 1.0, 0.0).astype(BF16)


def _top_block_bias(gates, n_valid):
    nb = gates.shape[1]
    n_iota = lax.broadcasted_iota(jnp.int32, gates.shape, 1)
    valid = n_iota < n_valid
    g = jnp.where(valid, gates, NEG_INF)
    cnt = jnp.zeros(gates.shape, jnp.int32)
    for m in range(nb):
        gm = g[:, m:m + 1]
        beats = jnp.where(gm > g, 1, jnp.where(gm == g, jnp.where(n_iota > m, 1, 0), 0))
        cnt = cnt + beats
    sel = jnp.where(valid, jnp.where(cnt < MOBA_TOP_K, 1, 0), 0)
    return jnp.where(sel == 1, 0.0, NEG_INF)


def _top_block_select_t(gates_t, n_valid):
    nb = gates_t.shape[0]
    n_iota = lax.broadcasted_iota(jnp.int32, gates_t.shape, 0)
    valid = n_iota < n_valid
    g = jnp.where(valid, gates_t, NEG_INF)
    cnt = jnp.zeros(gates_t.shape, jnp.int32)
    for m in range(nb):
        gm = g[m:m + 1, :]
        cnt = cnt + jnp.where(gm > g, 1, jnp.where(gm == g, jnp.where(n_iota > m, 1, 0), 0))
    return jnp.where(valid, jnp.where(cnt < MOBA_TOP_K, 1.0, 0.0), 0.0)


def _columns(cols, width):
    lane = lax.broadcasted_iota(jnp.int32, (cols[0].shape[0], width), 1)
    out = jnp.zeros((cols[0].shape[0], width), F32)
    for n, c in enumerate(cols):
        out = jnp.where(lane == n, c, out)
    return out


def _qkv_kernel(x_ref, wq_ref, wkv_ref, cos_ref, sin_ref, cost_ref, sint_ref,
                q_ref, kt_ref, vt_ref, kt16_ref, vt16_ref, *, gw):
    half = HEAD_DIM // 2
    x16 = x_ref[...].astype(BF16)
    cos = cos_ref[...]
    sin = sin_ref[...]
    lane = lax.broadcasted_iota(jnp.int32, cos.shape, 1)
    first_half = (lane & (HEAD_DIM - 1)) < half
    qa = jnp.dot(x16, wq_ref[:, :gw], preferred_element_type=F32)
    rot = jnp.where(first_half, pltpu.roll(qa, gw - half, 1), pltpu.roll(qa, half, 1))
    q_ref[:, :gw] = qa * cos + rot * sin
    q_ref[:, gw:] = jnp.dot(x16, wq_ref[:, gw:], preferred_element_type=F32)

    def proj_t(g):
        return lax.dot_general(wkv_ref[g * gw:(g + 1) * gw, :], x16, _NT, preferred_element_type=F32)

    def put(ref, ref16, r0, val):
        ref[r0:r0 + val.shape[0], :] = val
        ref16[r0:r0 + val.shape[0], :] = val.astype(BF16)

    kat = proj_t(0)
    cost = cost_ref[...]
    sint = sint_ref[...]
    for h in range(gw // HEAD_DIM):
        x1 = kat[h * HEAD_DIM:h * HEAD_DIM + half]
        x2 = kat[h * HEAD_DIM + half:(h + 1) * HEAD_DIM]
        put(kt_ref, kt16_ref, h * HEAD_DIM, x1 * cost - x2 * sint)
        put(kt_ref, kt16_ref, h * HEAD_DIM + half, x2 * cost + x1 * sint)
    put(kt_ref, kt16_ref, gw, proj_t(1))
    put(vt_ref, vt16_ref, 0, proj_t(2))
    put(vt_ref, vt16_ref, gw, proj_t(3))


def _qkv_project(x, wq16, wkvt16, cos, sin, cost, sint, rows, seq_len):
    t, d = x.shape
    gw = wq16.shape[1] // 2
    tiles = seq_len // rows
    full = lambda a: pl.BlockSpec(a.shape, lambda i: (0, 0))
    row_spec = lambda width: pl.BlockSpec((rows, width), lambda i: (i, 0))
    t_spec = pl.BlockSpec((2 * gw, rows), lambda i: (i // tiles, i % tiles))
    t_shape = lambda dt: jax.ShapeDtypeStruct((t // seq_len * 2 * gw, seq_len), dt)
    return pl.pallas_call(
        functools.partial(_qkv_kernel, gw=gw),
        grid=(t // rows,),
        in_specs=[row_spec(d), full(wq16), full(wkvt16),
                  pl.BlockSpec((rows, gw), lambda i: (i % tiles, 0)),
                  pl.BlockSpec((rows, gw), lambda i: (i % tiles, 0)),
                  pl.BlockSpec((HEAD_DIM // 2, rows), lambda i: (0, i % tiles)),
                  pl.BlockSpec((HEAD_DIM // 2, rows), lambda i: (0, i % tiles))],
        out_specs=[row_spec(2 * gw), t_spec, t_spec, t_spec, t_spec],
        out_shape=[jax.ShapeDtypeStruct((t, 2 * gw), F32), t_shape(F32), t_shape(F32), t_shape(BF16), t_shape(BF16)],
        compiler_params=_cparams(1),
        name="qkv_rope",
    )(x, wq16, wkvt16, cos, sin, cost, sint)


def _moba_prompt_kernel(q_ref, kt_ref, vt_ref, ktf_ref, o_ref, kbar_sc):
    c = pl.program_id(2)
    own = c // (MOBA_BLOCK // Q_BLOCK)
    tq = q_ref.shape[0]
    heads = q_ref.shape[1] // HEAD_DIM
    nb = kt_ref.shape[1] // MOBA_BLOCK

    nbp = -(-nb // 8) * 8

    @pl.when(c == 0)
    def _block_means():
        kbar_sc[...] = _columns([jnp.mean(ktf_ref[:, n * MOBA_BLOCK:(n + 1) * MOBA_BLOCK], axis=1, keepdims=True)
                                 for n in range(nb)], kbar_sc.shape[0]).T

    row = lax.broadcasted_iota(jnp.int32, (tq, MOBA_BLOCK), 0)
    col = lax.broadcasted_iota(jnp.int32, (tq, MOBA_BLOCK), 1)
    causal = own * MOBA_BLOCK + col <= c * tq + row
    n_iota = lax.broadcasted_iota(jnp.int32, (tq, nbp), 1)
    hs = [slice(h * HEAD_DIM, (h + 1) * HEAD_DIM) for h in range(heads)]
    q16, bias = [], []
    for h in range(heads):
        q = q_ref[:, hs[h]]
        gates_t = lax.dot_general(kbar_sc[:nbp, hs[h]], q, _NT, precision=HIGHEST, preferred_element_type=F32)
        bias.append(jnp.where(_top_block_select_t(gates_t, own).T > 0.5, 0.0, NEG_INF))
        q16.append((q * ATT_SCALE).astype(BF16))

    def scores(j, h):
        keys = pl.ds(pl.multiple_of(j * MOBA_BLOCK, MOBA_BLOCK), MOBA_BLOCK)
        return jnp.dot(q16[h], kt_ref[hs[h], keys], preferred_element_type=F32), vt_ref[hs[h], keys]

    state = []
    for h in range(heads):
        s, vb = scores(own, h)
        s = jnp.where(causal, s, NEG_INF)
        m = jnp.max(s, axis=1, keepdims=True)
        p = jnp.exp(s - m)
        state.append((m, jnp.sum(p, axis=1, keepdims=True),
                      lax.dot_general(p.astype(BF16), vb, _NT, preferred_element_type=F32)))

    def body(j, state):
        sv = [scores(j, h) for h in range(heads)]
        s = [sv[h][0] + jnp.sum(jnp.where(n_iota == j, bias[h], 0.0), axis=1, keepdims=True) for h in range(heads)]
        m_new = [jnp.maximum(state[h][0], jnp.max(s[h], axis=1, keepdims=True)) for h in range(heads)]
        p = [jnp.exp(s[h] - m_new[h]) for h in range(heads)]
        new = []
        for h in range(heads):
            m, l, acc = state[h]
            alpha = jnp.exp(m - m_new[h])
            new.append((m_new[h], alpha * l + jnp.sum(p[h], axis=1, keepdims=True),
                        alpha * acc + lax.dot_general(p[h].astype(BF16), sv[h][1], _NT, preferred_element_type=F32)))
        return tuple(new)

    state = lax.fori_loop(0, own, body, tuple(state))
    for h in range(heads):
        o_ref[:, hs[h]] = state[h][2] / state[h][1]


def _moba_prompt(q, kt16, vt16, kt, n_seq, seq_len, gw):
    width = min(MOBA_HEADS * HEAD_DIM, gw)
    n_groups = gw // width
    nc = seq_len // Q_BLOCK
    kv_spec = pl.BlockSpec((width, seq_len), lambda b, hg, c: (b * 2 * n_groups + hg, 0))
    return pl.pallas_call(
        _moba_prompt_kernel,
        grid=(n_seq, n_groups, nc),
        in_specs=[pl.BlockSpec((Q_BLOCK, width), lambda b, hg, c: (b * nc + c, hg)), kv_spec, kv_spec, kv_spec],
        out_specs=pl.BlockSpec((Q_BLOCK, width), lambda b, hg, c: (b * nc + c, hg)),
        out_shape=jax.ShapeDtypeStruct((n_seq * seq_len, gw), F32),
        scratch_shapes=[pltpu.VMEM((128, width), F32)],
        compiler_params=_cparams(3),
        name="moba_prompt",
    )(q, kt16, vt16, kt)


def _sb_prompt_kernel(q_ref, kt_ref, vt_ref, o_ref):
    c = pl.program_id(2)
    tq = q_ref.shape[0]
    heads = q_ref.shape[1] // HEAD_DIM
    upper = _strict_lower(SB_KEYS)
    row = lax.broadcasted_iota(jnp.int32, (tq, SB_KEYS), 0)
    col = lax.broadcasted_iota(jnp.int32, (tq, SB_KEYS), 1)
    strict = col < row
    hs = [slice(h * HEAD_DIM, (h + 1) * HEAD_DIM) for h in range(heads)]
    q16 = [(q_ref[:, hs[h]] * ATT_SCALE).astype(BF16) for h in range(heads)]

    def tile(j, state, mask):
        keys = pl.ds(pl.multiple_of(j * SB_KEYS, SB_KEYS), SB_KEYS)
        z = [jnp.dot(q16[h], kt_ref[hs[h], keys], preferred_element_type=F32) for h in range(heads)]
        pairs = [_log_sigmoid_pair(zh) for zh in z]
        lk = [pr[0] if mask is None else jnp.where(mask, pr[0], 0.0) for pr in pairs]
        after = _suffix_sums(lk, upper)
        new = []
        for h in range(heads):
            r, acc = state[h]
            a = jnp.exp(pairs[h][1] + after[h] + r)
            if mask is not None:
                a = jnp.where(mask, a, 0.0)
            acc = acc + lax.dot_general(a.astype(BF16), vt_ref[hs[h], keys], _NT, preferred_element_type=F32)
            new.append((r + jnp.sum(lk[h], axis=1, keepdims=True), acc))
        return tuple(new)

    state = tile(c, tuple((jnp.zeros((tq, 1), F32), jnp.zeros((tq, HEAD_DIM), F32)) for _ in range(heads)), strict)

    def cond(carry):
        i, live, _ = carry
        return jnp.logical_and(i < c, live > 0)

    def body(carry):
        i, _, state = carry
        r_max = state[0][0]
        for h in range(1, heads):
            r_max = jnp.maximum(r_max, state[h][0])
        live = (jnp.max(r_max) > SB_NEGLIGIBLE).astype(jnp.int32)
        return i + 1, live, tile(c - 1 - i, state, None)

    _, _, state = lax.while_loop(cond, body, (jnp.int32(0), jnp.int32(1), state))
    for h in range(heads):
        o_ref[:, hs[h]] = state[h][1]


def _sb_prompt(q, kt16, vt16, n_seq, seq_len, gw):
    width = min(SB_HEADS * HEAD_DIM, gw)
    n_groups = gw // width
    nc = seq_len // SB_KEYS
    kv_spec = pl.BlockSpec((width, seq_len), lambda b, hg, c: (b * 2 * n_groups + n_groups + hg, 0))
    return pl.pallas_call(
        _sb_prompt_kernel,
        grid=(n_seq, n_groups, nc),
        in_specs=[pl.BlockSpec((SB_KEYS, width), lambda b, hg, c: (b * nc + c, n_groups + hg)), kv_spec, kv_spec],
        out_specs=pl.BlockSpec((SB_KEYS, width), lambda b, hg, c: (b * nc + c, hg)),
        out_shape=jax.ShapeDtypeStruct((n_seq * seq_len, gw), F32),
        compiler_params=_cparams(3),
        name="sb_prompt",
    )(q, kt16, vt16)


def _decode_kernel(pt_ref, q_ref, knew_ref, vnew_ref, *refs, n_pages, nq):
    del pt_ref
    page_refs = refs[:4 * DECODE_PAGES]
    oa_ref, ob_ref, ksum_sc, m_sc, l_sc, o_sc, r_sc, accb_sc = refs[4 * DECODE_PAGES:]
    p = pl.program_id(1)
    n_steps = n_pages // DECODE_PAGES
    rows = q_ref.shape[1] // 2
    gh, hd, page = page_refs[0].shape[1:]
    gw = gh * hd
    pn = knew_ref.shape[3]
    pages_per_block = MOBA_BLOCK // page
    blocks_per_step = DECODE_PAGES // pages_per_block
    n_blocks = n_pages // pages_per_block
    lane_w = m_sc.shape[2]

    qa16 = (q_ref[0, :rows, :gw] * ATT_SCALE).astype(BF16)
    qb16 = (q_ref[0, rows:, gw:] * ATT_SCALE).astype(BF16)
    new_query = _imod(lax.broadcasted_iota(jnp.int32, (rows, pn), 0), nq)
    new_key = lax.broadcasted_iota(jnp.int32, (rows, pn), 1)

    def mat(ref):
        return ref[0].reshape(gw, page).astype(BF16)

    @pl.when(p == 0)
    def _init():
        z = jnp.dot(qb16, knew_ref[0, 1].astype(BF16), preferred_element_type=F32)
        mask = new_key < new_query
        lk, ls = _log_sigmoid_pair(z)
        lk = jnp.where(mask, lk, 0.0)
        a = jnp.where(mask, jnp.exp(ls + _suffix_sums([lk], _strict_lower(pn))[0]), 0.0)
        accb_sc[...] = lax.dot_general(a.astype(BF16), vnew_ref[0, 1].astype(BF16), _NT, preferred_element_type=F32)
        r_sc[...] = jnp.broadcast_to(jnp.sum(lk, axis=1, keepdims=True), r_sc.shape)

    for blk in range(blocks_per_step):
        n = n_blocks - 1 - (p * blocks_per_step + blk)
        pages = range(blk * pages_per_block, (blk + 1) * pages_per_block)
        s = [jnp.dot(qa16, mat(page_refs[4 * j]), preferred_element_type=F32) for j in pages]
        m_blk = functools.reduce(jnp.maximum, [jnp.max(sj, axis=1, keepdims=True) for sj in s])
        pr = [jnp.exp(sj - m_blk) for sj in s]
        l_blk = functools.reduce(jnp.add, [jnp.sum(pj, axis=1, keepdims=True) for pj in pr])
        o_blk = functools.reduce(jnp.add, [
            lax.dot_general(pj.astype(BF16), mat(page_refs[4 * j + 2]), _NT, preferred_element_type=F32)
            for pj, j in zip(pr, pages)])
        m_sc[n] = jnp.broadcast_to(m_blk, (rows, lane_w))
        l_sc[n] = jnp.broadcast_to(l_blk, (rows, lane_w))
        o_sc[n] = o_blk
        ksum_sc[n] = functools.reduce(jnp.add, [page_refs[4 * j][0].reshape(gw, page) for j in pages])

    z = [jnp.dot(qb16, mat(page_refs[4 * j + 1]), preferred_element_type=F32) for j in range(DECODE_PAGES)]
    pairs = [_log_sigmoid_pair(zj) for zj in z]
    after = _suffix_sums([pr[0] for pr in pairs], _strict_lower(page))
    r = r_sc[:, :1]
    acc = accb_sc[...]
    for j in range(DECODE_PAGES):
        a = jnp.exp(pairs[j][1] + after[j] + r)
        acc = acc + lax.dot_general(a.astype(BF16), mat(page_refs[4 * j + 3]), _NT, preferred_element_type=F32)
        r = r + jnp.sum(pairs[j][0], axis=1, keepdims=True)
    accb_sc[...] = acc
    r_sc[...] = jnp.broadcast_to(r, r_sc.shape)

    @pl.when(p == n_steps - 1)
    def _finish():
        kbar = _columns([jnp.sum(ksum_sc[b], axis=1, keepdims=True) * (1.0 / MOBA_BLOCK) for b in range(n_blocks)],
                        lane_w)
        gates = jnp.dot(q_ref[0, :rows, :gw], kbar, precision=HIGHEST, preferred_element_type=F32)[:, :n_blocks]
        bias = _top_block_bias(gates, n_blocks)
        sn = jnp.dot(qa16, knew_ref[0, 0].astype(BF16), preferred_element_type=F32)
        sn = jnp.where(new_key <= new_query, sn, NEG_INF)
        m_own = jnp.max(sn, axis=1, keepdims=True)
        pn_ = jnp.exp(sn - m_own)
        l_own = jnp.sum(pn_, axis=1, keepdims=True)
        o_own = lax.dot_general(pn_.astype(BF16), vnew_ref[0, 0].astype(BF16), _NT, preferred_element_type=F32)
        m_all = m_own
        for b in range(n_blocks):
            m_all = jnp.maximum(m_all, m_sc[b][:, :1] + bias[:, b:b + 1])
        w_own = jnp.exp(m_own - m_all)
        num = w_own * o_own
        den = w_own * l_own
        for b in range(n_blocks):
            w = jnp.exp(m_sc[b][:, :1] + bias[:, b:b + 1] - m_all)
            num = num + w * o_sc[b]
            den = den + w * l_sc[b][:, :1]
        outa = num / den
        rr = lax.broadcasted_iota(jnp.int32, (rows, gw), 0)
        cc = lax.broadcasted_iota(jnp.int32, (rows, gw), 1)
        diag = _idiv(rr, nq) == _idiv(cc, hd)
        si = lax.broadcasted_iota(jnp.int32, (oa_ref.shape[1], rows), 0)
        sr = lax.broadcasted_iota(jnp.int32, (oa_ref.shape[1], rows), 1)
        pick = jnp.where(_imod(sr, nq) == si, 1.0, 0.0)
        oa_ref[0] = jnp.dot(pick, jnp.where(diag, outa, 0.0), precision=HIGHEST, preferred_element_type=F32)
        ob_ref[0] = jnp.dot(pick, jnp.where(diag, accb_sc[...], 0.0), precision=HIGHEST,
                            preferred_element_type=F32)


def _decode_attention(page_table, qbd, knew, vnew, cache_kt, cache_vt, nq):
    n_dec, n_pages = page_table.shape
    rows = qbd.shape[1] // 2
    _, heads, hd, page = cache_kt.shape
    gh = heads // 2
    gw = gh * hd
    n_blocks = n_pages * page // MOBA_BLOCK
    per_seq = lambda a: pl.BlockSpec((1,) + a.shape[1:], lambda b, p, pt: (b,) + (0,) * (a.ndim - 1))
    page_spec = lambda j, g: pl.BlockSpec(
        (1, gh, hd, page), lambda b, p, pt: (pt[b, n_pages - 1 - (p * DECODE_PAGES + j)], g, 0, 0))
    out_spec = pl.BlockSpec((1, OUT_ROWS, gw), lambda b, p, pt: (b, 0, 0))
    assert n_pages % DECODE_PAGES == 0 and DECODE_PAGES % (MOBA_BLOCK // page) == 0 and nq <= OUT_ROWS
    grid_spec = pltpu.PrefetchScalarGridSpec(
        num_scalar_prefetch=1,
        grid=(n_dec, n_pages // DECODE_PAGES),
        in_specs=[per_seq(qbd), per_seq(knew), per_seq(vnew)]
        + [page_spec(j, g) for j in range(DECODE_PAGES) for g in (0, 1, 0, 1)],
        out_specs=[out_spec, out_spec],
        scratch_shapes=[pltpu.VMEM((n_blocks, gw, page), F32),
                        pltpu.VMEM((n_blocks, rows, 128), F32),
                        pltpu.VMEM((n_blocks, rows, 128), F32),
                        pltpu.VMEM((n_blocks, rows, gw), F32),
                        pltpu.VMEM((rows, 128), F32),
                        pltpu.VMEM((rows, gw), F32)],
    )
    return pl.pallas_call(
        functools.partial(_decode_kernel, n_pages=n_pages, nq=nq),
        grid_spec=grid_spec,
        out_shape=[jax.ShapeDtypeStruct((n_dec, OUT_ROWS, gw), F32)] * 2,
        compiler_params=_cparams(2),
        name="decode_attention",
    )(page_table, qbd, knew, vnew, *([cache_kt, cache_kt, cache_vt, cache_vt] * DECODE_PAGES))


def _merge_kernel(oa_ref, ob_ref, x_ref, ga_ref, gb_ref, w_ref, g_ref, b_ref, y_ref, *, alpha):
    gw = oa_ref.shape[1]
    ya = _rms_norm(oa_ref[...], ga_ref[...]).astype(BF16)
    yb = _rms_norm(ob_ref[...], gb_ref[...]).astype(BF16)
    mix = (jnp.dot(ya, w_ref[:gw, :], preferred_element_type=F32)
           + jnp.dot(yb, w_ref[gw:, :], preferred_element_type=F32))
    y_ref[...] = _layer_norm(alpha * x_ref[...] + mix, g_ref[...], b_ref[...])


def _merge(oa, ob, x, g_a, g_b, w16, ln_g, ln_b, alpha):
    t, d = x.shape
    gw = oa.shape[1]
    rows = _row_tile(t)
    row = lambda width: pl.BlockSpec((rows, width), lambda i: (i, 0))
    full = lambda a: pl.BlockSpec(a.shape, lambda i: (0, 0))
    return pl.pallas_call(
        functools.partial(_merge_kernel, alpha=alpha),
        grid=(t // rows,),
        in_specs=[row(gw), row(gw), row(d), full(g_a), full(g_b), full(w16), full(ln_g), full(ln_b)],
        out_specs=row(d),
        out_shape=jax.ShapeDtypeStruct((t, d), F32),
        compiler_params=_cparams(1),
        name="merge_out_proj",
    )(oa, ob, x, g_a, g_b, w16, ln_g, ln_b)


def _mem_kv_kernel(m_ref, w_ref, k_ref, v_ref, k16_ref, v16_ref):
    width = k_ref.shape[1]
    m16 = m_ref[...].astype(BF16)
    k = jnp.dot(m16, w_ref[:, :width], preferred_element_type=F32)
    v = jnp.dot(m16, w_ref[:, width:], preferred_element_type=F32)
    k_ref[...] = k
    v_ref[...] = v
    k16_ref[...] = k.astype(BF16)
    v16_ref[...] = v.astype(BF16)


def _mem_kv(mem, w16):
    t, d = mem.shape
    width = w16.shape[1] // 2
    rows = _row_tile(t)
    row = lambda w: pl.BlockSpec((rows, w), lambda i: (i, 0))
    return pl.pallas_call(
        _mem_kv_kernel,
        grid=(t // rows,),
        in_specs=[row(d), pl.BlockSpec(w16.shape, lambda i: (0, 0))],
        out_specs=[row(width)] * 4,
        out_shape=[jax.ShapeDtypeStruct((t, width), F32)] * 2 + [jax.ShapeDtypeStruct((t, width), BF16)] * 2,
        compiler_params=_cparams(1),
        name="mem_kv",
    )(mem, w16)


def _mem_heads(q16, head_kv, row_mask=None):
    hd = q16.shape[1] // MEM_HEADS
    outs = []
    for h in range(MEM_HEADS):
        k16, v16 = head_kv(h)
        s = lax.dot_general(q16[:, h * hd:(h + 1) * hd], k16, _NT, preferred_element_type=F32)
        m = jnp.max(s, axis=1, keepdims=True)
        p = jnp.exp(s - m)
        l = jnp.sum(p, axis=1, keepdims=True)
        o = jnp.dot(p.astype(BF16), v16, preferred_element_type=F32) / l
        outs.append(o if row_mask is None else jnp.where(row_mask, o, 0.0))
    return jnp.concatenate(outs, axis=1)


def _mem_attend_kernel(x_ref, wq_ref, mk_ref, mv_ref, wo_ref, g_ref, b_ref, y_ref, *, alpha):
    x = x_ref[...]
    hd = x.shape[1] // MEM_HEADS
    q16 = (jnp.dot(x.astype(BF16), wq_ref[...], preferred_element_type=F32) * hd ** -0.5).astype(BF16)
    o = _mem_heads(q16, lambda h: (mk_ref[:, h * hd:(h + 1) * hd], mv_ref[:, h * hd:(h + 1) * hd]))
    y = jnp.dot(o.astype(BF16), wo_ref[...], preferred_element_type=F32)
    y_ref[...] = _layer_norm(alpha * x + y, g_ref[...], b_ref[...])


def _mem_attend_prompt(x, wq16, mk16, mv16, wo16, ln_g, ln_b, n_seq, alpha):
    t, d = x.shape
    seq_len = t // n_seq
    n_mem = mk16.shape[0] // n_seq
    rows = _row_tile(seq_len)
    tiles = seq_len // rows
    row = pl.BlockSpec((rows, d), lambda b, i: (b * tiles + i, 0))
    full = lambda a: pl.BlockSpec(a.shape, lambda b, i: (0, 0))
    mem = pl.BlockSpec((n_mem, d), lambda b, i: (b, 0))
    return pl.pallas_call(
        functools.partial(_mem_attend_kernel, alpha=alpha),
        grid=(n_seq, tiles),
        in_specs=[row, full(wq16), mem, mem, full(wo16), full(ln_g), full(ln_b)],
        out_specs=row,
        out_shape=jax.ShapeDtypeStruct((t, d), F32),
        compiler_params=_cparams(2),
        name="mem_attend_prompt",
    )(x, wq16, mk16, mv16, wo16, ln_g, ln_b)


def _mem_attend_sample_kernel(x_ref, wq_ref, mk_ref, mv_ref, wo_ref, g_ref, b_ref, y_ref, *, alpha, nq):
    x = x_ref[...]
    hd = x.shape[1] // MEM_HEADS
    q16 = (jnp.dot(x.astype(BF16), wq_ref[...], preferred_element_type=F32) * hd ** -0.5).astype(BF16)
    seq_of_row = _idiv(lax.broadcasted_iota(jnp.int32, (x.shape[0], 1), 0), nq)
    o = jnp.zeros(x.shape, F32)
    for g in range(mk_ref.shape[0]):
        head_kv = lambda h, g=g: (mk_ref[g, :, h * hd:(h + 1) * hd].astype(BF16),
                                  mv_ref[g, :, h * hd:(h + 1) * hd].astype(BF16))
        o = o + _mem_heads(q16, head_kv, seq_of_row == g)
    y = jnp.dot(o.astype(BF16), wo_ref[...], preferred_element_type=F32)
    y_ref[...] = _layer_norm(alpha * x + y, g_ref[...], b_ref[...])


def _mem_attend_sample(x, wq16, cache_mk, cache_mv, wo16, ln_g, ln_b, nq, alpha):
    t, d = x.shape
    n_dec, n_mem, _ = cache_mk.shape
    rows = MEM_GROUP * nq
    row = pl.BlockSpec((rows, d), lambda i: (i, 0))
    full = lambda a: pl.BlockSpec(a.shape, lambda i: (0, 0))
    mem = pl.BlockSpec((MEM_GROUP, n_mem, d), lambda i: (i, 0, 0))
    return pl.pallas_call(
        functools.partial(_mem_attend_sample_kernel, alpha=alpha, nq=nq),
        grid=(n_dec // MEM_GROUP,),
        in_specs=[row, full(wq16), mem, mem, full(wo16), full(ln_g), full(ln_b)],
        out_specs=row,
        out_shape=jax.ShapeDtypeStruct((t, d), F32),
        compiler_params=_cparams(1),
        name="mem_attend_sample",
    )(x, wq16, cache_mk, cache_mv, wo16, ln_g, ln_b)


def _router_kernel(x_ref, w_ref, b_ref, e_ref, g_ref, rank_ref, cnt_ref, run_sc):
    rows = x_ref.shape[0]
    logits = jnp.dot(x_ref[...], w_ref[...], precision=HIGHEST, preferred_element_type=F32) + b_ref[...]
    n_exp = logits.shape[1]
    e_iota = lax.broadcasted_iota(jnp.int32, logits.shape, 1)
    k_iota = lax.broadcasted_iota(jnp.int32, e_ref.shape, 1)
    top_e = jnp.zeros(e_ref.shape, jnp.int32)
    top_v = jnp.zeros(e_ref.shape, F32)
    onehot = []
    for k in range(TOP_K):
        mx = jnp.max(logits, axis=1, keepdims=True)
        idx = jnp.min(jnp.where(logits == mx, e_iota, n_exp), axis=1, keepdims=True)
        top_e = jnp.where(k_iota == k, idx, top_e)
        top_v = jnp.where(k_iota == k, mx, top_v)
        onehot.append(jnp.where(e_iota == idx, 1.0, 0.0))
        logits = jnp.where(e_iota == idx, NEG_INF, logits)
    w = jnp.exp(top_v - top_v[:, :1])
    e_ref[...] = top_e
    g_ref[...] = w / jnp.sum(w, axis=1, keepdims=True)

    @pl.when(pl.program_id(0) == 0)
    def _zero():
        run_sc[...] = jnp.zeros(run_sc.shape, F32)

    chosen = functools.reduce(jnp.add, onehot)
    before = jnp.dot(_strict_lower(rows), chosen.astype(BF16), preferred_element_type=F32) + run_sc[...]
    rank = jnp.zeros(e_ref.shape, F32)
    for k in range(TOP_K):
        rank = jnp.where(k_iota == k, jnp.sum(onehot[k] * before, axis=1, keepdims=True), rank)
    rank_ref[...] = rank.astype(jnp.int32)
    run_sc[...] = run_sc[...] + jnp.sum(chosen, axis=0, keepdims=True)
    cnt_ref[...] = run_sc[...]


def _router(x, w_router, b_router):
    t, d = x.shape
    n_exp = w_router.shape[1]
    rows = _row_tile(t)
    out = pl.BlockSpec((rows, TOP_K), lambda i: (i, 0))
    return pl.pallas_call(
        _router_kernel,
        grid=(t // rows,),
        in_specs=[pl.BlockSpec((rows, d), lambda i: (i, 0)),
                  pl.BlockSpec(w_router.shape, lambda i: (0, 0)),
                  pl.BlockSpec(b_router.shape, lambda i: (0, 0))],
        out_specs=[out, out, out, pl.BlockSpec((1, n_exp), lambda i: (0, 0))],
        out_shape=[jax.ShapeDtypeStruct((t, TOP_K), jnp.int32), jax.ShapeDtypeStruct((t, TOP_K), F32),
                   jax.ShapeDtypeStruct((t, TOP_K), jnp.int32), jax.ShapeDtypeStruct((1, n_exp), F32)],
        scratch_shapes=[pltpu.VMEM((1, n_exp), F32)],
        compiler_params=_cparams(1),
        name="router_top4",
    )(x, w_router, b_router)


def _row_copy(src_hbm, src_row, dst, r, sem):
    return pltpu.make_async_copy(src_hbm.at[pl.ds(src_row, 1)], dst.at[pl.ds(r, 1)], sem)


def _row_gather(src_hbm, idx_ref, dst, sem, n_rows):
    for r in range(n_rows):
        _row_copy(src_hbm, idx_ref[0, 0, r], dst, r, sem).start()


def _row_gather_wait(src_hbm, dst, sem, n_rows):
    def wait(r, carry):
        _row_copy(src_hbm, 0, dst, r, sem).wait()
        return carry
    lax.fori_loop(0, n_rows, wait, 0, unroll=DMA_UNROLL)


def _dispatch_kernel(idx_ref, x_ref, rows_in, rows_out, stage, sems):
    del rows_in
    s = pl.program_id(0)
    n_steps = pl.num_programs(0)
    rows = x_ref.shape[0]
    slot = s % 2

    def copy(buf, t, dst_row):
        return pltpu.make_async_copy(stage.at[buf, pl.ds(t, 1)], rows_out.at[pl.ds(dst_row, 1)], sems.at[buf])

    def drain(buf):
        def wait(r, carry):
            copy(buf, 0, 0).wait()
            return carry
        lax.fori_loop(0, TOP_K * rows, wait, 0, unroll=DMA_UNROLL)

    @pl.when(s >= 2)
    def _reuse():
        drain(slot)

    stage[slot] = x_ref[...]
    for k in range(TOP_K):
        for t in range(rows):
            copy(slot, t, idx_ref[0, 0, k * rows + t]).start()

    @pl.when(s == n_steps - 1)
    def _finish():
        @pl.when(s >= 1)
        def _other():
            drain(1 - slot)
        drain(slot)


def _dispatch(x, dest, n_rows):
    t, d = x.shape
    n_steps, _, n_copy = dest.shape
    rows = n_copy // TOP_K
    return pl.pallas_call(
        _dispatch_kernel,
        grid=(n_steps,),
        in_specs=[pl.BlockSpec((1, 1, n_copy), lambda s: (s, 0, 0), memory_space=pltpu.SMEM),
                  pl.BlockSpec((rows, d), lambda s: (s, 0)),
                  pl.BlockSpec(memory_space=pl.ANY)],
        out_specs=pl.BlockSpec(memory_space=pl.ANY),
        out_shape=jax.ShapeDtypeStruct((n_rows, d), F32),
        input_output_aliases={2: 0},
        scratch_shapes=[pltpu.VMEM((2, rows, d), F32), pltpu.SemaphoreType.DMA((2,))],
        compiler_params=_cparams(1),
        name="moe_dispatch",
    )(dest, x, jnp.zeros((n_rows, d), F32))


def _expert_ffn_kernel(be_ref, nu_ref, x_ref, wgu_ref, bgu_ref, wd_ref, bd_ref, y_ref, wgu16, wd16):
    s = pl.program_id(0)
    active = s < nu_ref[0]
    new_expert = jnp.logical_or(s == 0, be_ref[s] != be_ref[jnp.maximum(s - 1, 0)])

    @pl.when(jnp.logical_and(active, new_expert))
    def _cast_weights():
        wgu16[...] = wgu_ref[0].astype(BF16)
        wd16[...] = wd_ref[0].astype(BF16)

    @pl.when(active)
    def _compute():
        f = wd_ref.shape[1]
        hgu = jnp.dot(x_ref[...].astype(BF16), wgu16[...], preferred_element_type=F32) + bgu_ref[0]
        gate = jnp.minimum(hgu[:, :f], SWIGLU_LIMIT)
        up = jnp.clip(hgu[:, f:], -SWIGLU_LIMIT, SWIGLU_LIMIT)
        act = (up + 1.0) * gate * jax.nn.sigmoid(SWIGLU_ALPHA * gate)
        y_ref[...] = jnp.dot(act.astype(BF16), wd16[...], preferred_element_type=F32) + bd_ref[0]

    @pl.when(jnp.logical_not(active))
    def _unused():
        y_ref[...] = jnp.zeros(y_ref.shape, F32)


def _expert_ffn(x_rows, block_e, n_used, w_gu, b_gu, w_down, b_down, rows):
    n_exp, d, f2 = w_gu.shape
    f = f2 // 2
    n_blk = x_rows.shape[0] // rows
    by_expert = lambda shape: pl.BlockSpec(shape, lambda s, be, nu: (be[s], 0, 0))
    row_spec = pl.BlockSpec((rows, d), lambda s, be, nu: (s, 0))
    grid_spec = pltpu.PrefetchScalarGridSpec(
        num_scalar_prefetch=2,
        grid=(n_blk,),
        in_specs=[row_spec, by_expert((1, d, f2)), by_expert((1, 1, f2)), by_expert((1, f, d)),
                  by_expert((1, 1, d))],
        out_specs=row_spec,
        scratch_shapes=[pltpu.VMEM((d, f2), BF16), pltpu.VMEM((f, d), BF16)],
    )
    return pl.pallas_call(
        _expert_ffn_kernel,
        grid_spec=grid_spec,
        out_shape=jax.ShapeDtypeStruct((n_blk * rows, d), F32),
        compiler_params=_cparams(1),
        name="expert_ffn",
    )(block_e, n_used, x_rows, w_gu, b_gu.reshape(n_exp, 1, f2), w_down, b_down.reshape(n_exp, 1, d))


def _combine_kernel(idx_ref, y_hbm, x_ref, gate_ref, g_ref, b_ref, o_ref, ybuf, sems, *, alpha):
    s = pl.program_id(0)
    n_steps = pl.num_programs(0) - 1
    rows = x_ref.shape[0]
    n_copy = TOP_K * rows

    @pl.when(s < n_steps)
    def _fetch():
        _row_gather(y_hbm, idx_ref, ybuf.at[s % 2], sems.at[s % 2], n_copy)

    @pl.when(s >= 1)
    def _compute():
        slot = (s - 1) % 2
        _row_gather_wait(y_hbm, ybuf.at[slot], sems.at[slot], n_copy)
        gates = gate_ref[...]
        ffn = jnp.zeros(x_ref.shape, F32)
        for k in range(TOP_K):
            ffn = ffn + gates[:, k:k + 1] * ybuf[slot, k * rows:(k + 1) * rows, :]
        o_ref[...] = _layer_norm(alpha * x_ref[...] + ffn, g_ref[...], b_ref[...])


def _combine(y_rows, dest, x, gates, ln_g, ln_b, alpha):
    t, d = x.shape
    n_steps, _, n_copy = dest.shape
    rows = n_copy // TOP_K
    idx_spec = pl.BlockSpec((1, 1, n_copy), lambda s: (jnp.minimum(s, n_steps - 1), 0, 0),
                            memory_space=pltpu.SMEM)
    row = lambda w: pl.BlockSpec((rows, w), lambda s: (jnp.maximum(s - 1, 0), 0))
    full = lambda a: pl.BlockSpec(a.shape, lambda s: (0, 0))
    return pl.pallas_call(
        functools.partial(_combine_kernel, alpha=alpha),
        grid=(n_steps + 1,),
        in_specs=[idx_spec, pl.BlockSpec(memory_space=pl.ANY), row(d), row(TOP_K), full(ln_g), full(ln_b)],
        out_specs=row(d),
        out_shape=jax.ShapeDtypeStruct((t, d), F32),
        scratch_shapes=[pltpu.VMEM((2, n_copy, d), F32), pltpu.SemaphoreType.DMA((2,))],
        compiler_params=_cparams(1),
        name="moe_combine",
    )(dest, y_rows, x, gates, ln_g, ln_b)


def _dispatch_plan(top_e, rank, counts, rows):
    n_assign = top_e.size
    n_exp = counts.shape[0]
    flat_e = top_e.reshape(n_assign)
    padded = (counts + rows - 1) // rows * rows
    pad_end = jnp.cumsum(padded)
    pad_start = pad_end - padded
    dest = (pad_start[flat_e] + rank.reshape(n_assign)).astype(jnp.int32)
    n_blk = -(-(n_assign + n_exp * (rows - 1)) // rows)
    block_start = jnp.arange(n_blk, dtype=jnp.int32) * rows
    block_e = jnp.minimum(jnp.sum((pad_end[None, :] <= block_start[:, None]).astype(jnp.int32), axis=1), n_exp - 1)
    n_used = (pad_end[-1] // rows).astype(jnp.int32).reshape(1)
    return dest, block_e, n_used, n_blk


def _moe(tok, w_router, b_router, w_gu, b_gu, w_down, b_down, ln_g, ln_b, alpha):
    t, d = tok.shape
    n_exp = w_router.shape[1]
    top_e, gates, rank, counts = _router(tok, w_router, b_router.reshape(1, n_exp))
    dest, block_e, n_used, n_blk = _dispatch_plan(top_e, rank, counts[0].astype(jnp.int32), EXPERT_ROWS)
    steps = t // COMBINE_ROWS
    dest_steps = dest.reshape(steps, COMBINE_ROWS, TOP_K).transpose(0, 2, 1).reshape(steps, 1, TOP_K * COMBINE_ROWS)
    x_rows = _dispatch(tok, dest_steps, n_blk * EXPERT_ROWS)
    y_rows = _expert_ffn(x_rows, block_e, n_used, w_gu, b_gu, w_down, b_down, EXPERT_ROWS)
    return _combine(y_rows, dest_steps, tok, gates, ln_g, ln_b, alpha)


def _rope_tables(pos, n_heads):
    half = HEAD_DIM // 2
    inv_freq = ROPE_THETA ** (-jnp.arange(half, dtype=F32) / half)
    ang = pos.astype(F32)[:, None] * inv_freq[None, :]
    cos = jnp.cos(ang)
    sin = jnp.sin(ang)
    return (jnp.tile(jnp.concatenate([cos, cos], axis=1), (1, n_heads)),
            jnp.tile(jnp.concatenate([-sin, sin], axis=1), (1, n_heads)), cos.T, sin.T)


def _block_diag_queries(q, nq, n_heads):
    n_dec = q.shape[0] // nq
    width = n_heads * HEAD_DIM
    rows = jnp.tile(q.reshape(n_dec, 1, nq, width), (1, n_heads, 1, 1)).reshape(n_dec, n_heads * nq, width)
    r = jnp.arange(n_heads * nq)[:, None] // nq
    c = jnp.arange(width)[None, :] // HEAD_DIM
    return jnp.where((r == c)[None], rows, 0.0)


def kernel(x_prompt, x_sample, mem_prompt, cache_k, cache_v, cache_mem_k, cache_mem_v, page_table,
           w_in, g_moba, g_sb, w_out, ln1_g, ln1_b, w_mq, w_mkv, w_mo, ln2_g, ln2_b,
           w_router, b_router, w_gu, b_gu, w_down, b_down, ln3_g, ln3_b):
    n_seq, seq_len, d = x_prompt.shape
    n_dec, nq, _ = x_sample.shape
    depth = w_in.shape[0]
    gw = w_in.shape[2] // 6
    g_heads = gw // HEAD_DIM
    heads = 2 * g_heads
    n_pages = page_table.shape[1]
    page = cache_k.shape[2]
    past_len = n_pages * page
    n_mem = mem_prompt.shape[1]
    alpha = (2 * depth) ** 0.25
    n_tok_p = n_seq * seq_len
    n_tok_s = n_dec * nq
    assert seq_len % MOBA_BLOCK == 0 and past_len % MOBA_BLOCK == 0 and MOBA_BLOCK % page == 0
    assert nq <= NEW_PAD and n_dec % MEM_GROUP == 0
    assert (n_tok_p + n_tok_s) % COMBINE_ROWS == 0 and n_tok_s % PROJ_ROWS == 0 and seq_len % PROJ_ROWS == 0

    tables_p = _rope_tables(jnp.arange(seq_len), g_heads)
    tables_s = _rope_tables(jnp.tile(past_len + jnp.arange(nq), n_dec), g_heads)
    xp = x_prompt.reshape(n_tok_p, d)
    xs = x_sample.reshape(n_tok_s, d)
    row2 = lambda a: a.reshape(1, -1)

    def to_positions(at, n, length):
        return at.reshape(n, heads, HEAD_DIM, length).transpose(0, 3, 1, 2)

    def new_columns(at):
        a = at.reshape(2, gw, n_dec, nq).transpose(2, 0, 1, 3)
        return jnp.pad(a, ((0, 0), (0, 0), (0, 0), (0, NEW_PAD - nq)))

    outs = [[] for _ in range(6)]
    for l in range(depth):
        w = w_in[l].astype(BF16)
        col = lambda g: w[:, g * gw:(g + 1) * gw]
        wq16 = jnp.concatenate([col(0), col(3)], axis=1)
        wkvt16 = jnp.concatenate([col(1), col(4), col(2), col(5)], axis=1).T
        w_out16 = w_out[l].astype(BF16)
        q_p, kt_p, vt_p, kt16_p, vt16_p = _qkv_project(xp, wq16, wkvt16, *tables_p, PROJ_ROWS, seq_len)
        oa_p = _moba_prompt(q_p, kt16_p, vt16_p, kt_p, n_seq, seq_len, gw)
        ob_p = _sb_prompt(q_p, kt16_p, vt16_p, n_seq, seq_len, gw)
        outs[0].append(to_positions(kt_p, n_seq, seq_len))
        outs[1].append(to_positions(vt_p, n_seq, seq_len))
        q_s, kt_s, vt_s, _, _ = _qkv_project(xs, wq16, wkvt16, *tables_s, PROJ_ROWS, n_tok_s)
        oa_s, ob_s = _decode_attention(
            page_table, _block_diag_queries(q_s, nq, heads), new_columns(kt_s), new_columns(vt_s),
            cache_k[l].transpose(0, 2, 3, 1), cache_v[l].transpose(0, 2, 3, 1), nq)
        outs[2].append(to_positions(kt_s, 1, n_tok_s).reshape(n_dec, nq, heads, HEAD_DIM))
        outs[3].append(to_positions(vt_s, 1, n_tok_s).reshape(n_dec, nq, heads, HEAD_DIM))
        merge = functools.partial(_merge, g_a=row2(g_moba[l]), g_b=row2(g_sb[l]), w16=w_out16,
                                  ln_g=row2(ln1_g[l]), ln_b=row2(ln1_b[l]), alpha=alpha)
        xp = merge(oa_p, ob_p, xp)
        xs = merge(oa_s[:, :nq].reshape(n_tok_s, gw), ob_s[:, :nq].reshape(n_tok_s, gw), xs)
        mk, mv, mk16, mv16 = _mem_kv(mem_prompt.reshape(n_seq * n_mem, d), w_mkv[l].astype(BF16))
        outs[4].append(mk.reshape(n_seq, n_mem, MEM_HEADS, d // MEM_HEADS))
        outs[5].append(mv.reshape(n_seq, n_mem, MEM_HEADS, d // MEM_HEADS))
        wmq16 = w_mq[l].astype(BF16)
        wmo16 = w_mo[l].astype(BF16)
        xp = _mem_attend_prompt(xp, wmq16, mk16, mv16, wmo16, row2(ln2_g[l]), row2(ln2_b[l]), n_seq, alpha)
        xs = _mem_attend_sample(xs, wmq16, cache_mem_k[l].reshape(n_dec, n_mem, d),
                                cache_mem_v[l].reshape(n_dec, n_mem, d), wmo16,
                                row2(ln2_g[l]), row2(ln2_b[l]), nq, alpha)
        tok = jnp.concatenate([xp, xs], axis=0)
        tok = _moe(tok, w_router[l], b_router[l], w_gu[l], b_gu[l], w_down[l],
                   b_down[l], row2(ln3_g[l]), row2(ln3_b[l]), alpha)
        xp = tok[:n_tok_p]
        xs = tok[n_tok_p:]
    return (xp.reshape(n_seq, seq_len, d), xs.reshape(n_dec, nq, d)) + tuple(jnp.stack(o) for o in outs)
```

```python
import functools

import jax
import jax.numpy as jnp
from jax import lax
from jax.experimental import pallas as pl
from jax.experimental.pallas import tpu as pltpu

F32 = jnp.float32
BF16 = jnp.bfloat16
HIGHEST = lax.Precision.HIGHEST

HEAD_DIM = 64
MOBA_BLOCK = 256
MOBA_TOP_K = 3
Q_BLOCK = 128
ROPE_THETA = 10000.0
MEM_HEADS = 4
TOP_K = 4
SWIGLU_LIMIT = 7.0
SWIGLU_ALPHA = 1.702
LN_EPS = 1e-5
RMS_EPS = 1e-6
ATT_SCALE = HEAD_DIM ** -0.5
NEG_INF = float("-inf")
SB_NEGLIGIBLE = -150.0

PROJ_ROWS = 256
SB_KEYS = 128
MOBA_HEADS = 4
SB_HEADS = 8
DECODE_PAGES = 8
TOKEN_ROWS = 512
EXPERT_ROWS = 256
COMBINE_ROWS = 128
NEW_PAD = 16
OUT_ROWS = 8
MEM_GROUP = 4
DMA_UNROLL = 8
VMEM_LIMIT = 56 * 1024 * 1024

_NT = (((1,), (1,)), ((), ()))


def _cparams(n_axes):
    return pltpu.CompilerParams(dimension_semantics=("arbitrary",) * n_axes,
                                vmem_limit_bytes=VMEM_LIMIT)


def _row_tile(n_rows):
    rows = TOKEN_ROWS
    while n_rows % rows:
        rows //= 2
    assert rows >= 8, n_rows
    return rows


def _idiv(x, n):
    return x >> (n.bit_length() - 1) if n & (n - 1) == 0 else x // n


def _imod(x, n):
    return x & (n - 1) if n & (n - 1) == 0 else x % n


def _layer_norm(x, g, b):
    mu = jnp.mean(x, axis=-1, keepdims=True)
    xc = x - mu
    var = jnp.mean(xc * xc, axis=-1, keepdims=True)
    return xc * lax.rsqrt(var + LN_EPS) * g + b


def _rms_norm(x, g):
    return x * lax.rsqrt(jnp.mean(x * x, axis=-1, keepdims=True) + RMS_EPS) * g


def _log_sigmoid_pair(z):
    t = jnp.log1p(jnp.exp(-jnp.abs(z)))
    return -(jnp.maximum(z, 0.0) + t), jnp.minimum(z, 0.0) - t


def _suffix_sums(xs, upper):
    rows = xs[0].shape[0]
    hi = [x.astype(BF16) for x in xs]
    lo = [(x - xh.astype(F32)).astype(BF16) for x, xh in zip(xs, hi)]
    s = jnp.dot(jnp.concatenate(hi + lo, axis=0), upper, preferred_element_type=F32)
    n = len(xs)
    return [s[i * rows:(i + 1) * rows] + s[(n + i) * rows:(n + i + 1) * rows] for i in range(n)]


def _strict_lower(n):
    r = lax.broadcasted_iota(jnp.int32, (n, n), 0)
    c = lax.broadcasted_iota(jnp.int32, (n, n), 1)
    return jnp.where(r > c, 1.0, 0.0).astype(BF16)


def _top_block_bias(gates, n_valid):
    nb = gates.shape[1]
    n_iota = lax.broadcasted_iota(jnp.int32, gates.shape, 1)
    valid = n_iota < n_valid
    g = jnp.where(valid, gates, NEG_INF)
    cnt = jnp.zeros(gates.shape, jnp.int32)
    for m in range(nb):
        gm = g[:, m:m + 1]
        beats = jnp.where(gm > g, 1, jnp.where(gm == g, jnp.where(n_iota > m, 1, 0), 0))
        cnt = cnt + beats
    sel = jnp.where(valid, jnp.where(cnt < MOBA_TOP_K, 1, 0), 0)
    return jnp.where(sel == 1, 0.0, NEG_INF)


def _top_block_select_t(gates_t, n_valid):
    nb = gates_t.shape[0]
    n_iota = lax.broadcasted_iota(jnp.int32, gates_t.shape, 0)
    valid = n_iota < n_valid
    g = jnp.where(valid, gates_t, NEG_INF)
    cnt = jnp.zeros(gates_t.shape, jnp.int32)
    for m in range(nb):
        gm = g[m:m + 1, :]
        cnt = cnt + jnp.where(gm > g, 1, jnp.where(gm == g, jnp.where(n_iota > m, 1, 0), 0))
    return jnp.where(valid, jnp.where(cnt < MOBA_TOP_K, 1.0, 0.0), 0.0)


def _columns(cols, width):
    lane = lax.broadcasted_iota(jnp.int32, (cols[0].shape[0], width), 1)
    out = jnp.zeros((cols[0].shape[0], width), F32)
    for n, c in enumerate(cols):
        out = jnp.where(lane == n, c, out)
    return out


def _qkv_kernel(x_ref, wq_ref, wkv_ref, cos_ref, sin_ref, cost_ref, sint_ref,
                q_ref, kt_ref, vt_ref, kt16_ref, vt16_ref, *, gw):
    half = HEAD_DIM // 2
    x16 = x_ref[...].astype(BF16)
    cos = cos_ref[...]
    sin = sin_ref[...]
    lane = lax.broadcasted_iota(jnp.int32, cos.shape, 1)
    first_half = (lane & (HEAD_DIM - 1)) < half
    qa = jnp.dot(x16, wq_ref[:, :gw], preferred_element_type=F32)
    rot = jnp.where(first_half, pltpu.roll(qa, gw - half, 1), pltpu.roll(qa, half, 1))
    q_ref[:, :gw] = qa * cos + rot * sin
    q_ref[:, gw:] = jnp.dot(x16, wq_ref[:, gw:], preferred_element_type=F32)

    def proj_t(g):
        return lax.dot_general(wkv_ref[g * gw:(g + 1) * gw, :], x16, _NT, preferred_element_type=F32)

    def put(ref, ref16, r0, val):
        ref[r0:r0 + val.shape[0], :] = val
        ref16[r0:r0 + val.shape[0], :] = val.astype(BF16)

    kat = proj_t(0)
    cost = cost_ref[...]
    sint = sint_ref[...]
    for h in range(gw // HEAD_DIM):
        x1 = kat[h * HEAD_DIM:h * HEAD_DIM + half]
        x2 = kat[h * HEAD_DIM + half:(h + 1) * HEAD_DIM]
        put(kt_ref, kt16_ref, h * HEAD_DIM, x1 * cost - x2 * sint)
        put(kt_ref, kt16_ref, h * HEAD_DIM + half, x2 * cost + x1 * sint)
    put(kt_ref, kt16_ref, gw, proj_t(1))
    put(vt_ref, vt16_ref, 0, proj_t(2))
    put(vt_ref, vt16_ref, gw, proj_t(3))


def _qkv_project(x, wq16, wkvt16, cos, sin, cost, sint, rows, seq_len):
    t, d = x.shape
    gw = wq16.shape[1] // 2
    tiles = seq_len // rows
    full = lambda a: pl.BlockSpec(a.shape, lambda i: (0, 0))
    row_spec = lambda width: pl.BlockSpec((rows, width), lambda i: (i, 0))
    t_spec = pl.BlockSpec((2 * gw, rows), lambda i: (i // tiles, i % tiles))
    t_shape = lambda dt: jax.ShapeDtypeStruct((t // seq_len * 2 * gw, seq_len), dt)
    return pl.pallas_call(
        functools.partial(_qkv_kernel, gw=gw),
        grid=(t // rows,),
        in_specs=[row_spec(d), full(wq16), full(wkvt16),
                  pl.BlockSpec((rows, gw), lambda i: (i % tiles, 0)),
                  pl.BlockSpec((rows, gw), lambda i: (i % tiles, 0)),
                  pl.BlockSpec((HEAD_DIM // 2, rows), lambda i: (0, i % tiles)),
                  pl.BlockSpec((HEAD_DIM // 2, rows), lambda i: (0, i % tiles))],
        out_specs=[row_spec(2 * gw), t_spec, t_spec, t_spec, t_spec],
        out_shape=[jax.ShapeDtypeStruct((t, 2 * gw), F32), t_shape(F32), t_shape(F32), t_shape(BF16), t_shape(BF16)],
        compiler_params=_cparams(1),
        name="qkv_rope",
    )(x, wq16, wkvt16, cos, sin, cost, sint)


def _moba_prompt_kernel(q_ref, kt_ref, vt_ref, ktf_ref, o_ref, kbar_sc):
    c = pl.program_id(2)
    own = c // (MOBA_BLOCK // Q_BLOCK)
    tq = q_ref.shape[0]
    heads = q_ref.shape[1] // HEAD_DIM
    nb = kt_ref.shape[1] // MOBA_BLOCK

    nbp = -(-nb // 8) * 8

    @pl.when(c == 0)
    def _block_means():
        kbar_sc[...] = _columns([jnp.mean(ktf_ref[:, n * MOBA_BLOCK:(n + 1) * MOBA_BLOCK], axis=1, keepdims=True)
                                 for n in range(nb)], kbar_sc.shape[0]).T

    row = lax.broadcasted_iota(jnp.int32, (tq, MOBA_BLOCK), 0)
    col = lax.broadcasted_iota(jnp.int32, (tq, MOBA_BLOCK), 1)
    causal = own * MOBA_BLOCK + col <= c * tq + row
    n_iota = lax.broadcasted_iota(jnp.int32, (tq, nbp), 1)
    hs = [slice(h * HEAD_DIM, (h + 1) * HEAD_DIM) for h in range(heads)]
    q16, bias = [], []
    for h in range(heads):
        q = q_ref[:, hs[h]]
        gates_t = lax.dot_general(kbar_sc[:nbp, hs[h]], q, _NT, precision=HIGHEST, preferred_element_type=F32)
        bias.append(jnp.where(_top_block_select_t(gates_t, own).T > 0.5, 0.0, NEG_INF))
        q16.append((q * ATT_SCALE).astype(BF16))

    def scores(j, h):
        keys = pl.ds(pl.multiple_of(j * MOBA_BLOCK, MOBA_BLOCK), MOBA_BLOCK)
        return jnp.dot(q16[h], kt_ref[hs[h], keys], preferred_element_type=F32), vt_ref[hs[h], keys]

    state = []
    for h in range(heads):
        s, vb = scores(own, h)
        s = jnp.where(causal, s, NEG_INF)
        m = jnp.max(s, axis=1, keepdims=True)
        p = jnp.exp(s - m)
        state.append((m, jnp.sum(p, axis=1, keepdims=True),
                      lax.dot_general(p.astype(BF16), vb, _NT, preferred_element_type=F32)))

    def body(j, state):
        sv = [scores(j, h) for h in range(heads)]
        s = [sv[h][0] + jnp.sum(jnp.where(n_iota == j, bias[h], 0.0), axis=1, keepdims=True) for h in range(heads)]
        m_new = [jnp.maximum(state[h][0], jnp.max(s[h], axis=1, keepdims=True)) for h in range(heads)]
        p = [jnp.exp(s[h] - m_new[h]) for h in range(heads)]
        new = []
        for h in range(heads):
            m, l, acc = state[h]
            alpha = jnp.exp(m - m_new[h])
            new.append((m_new[h], alpha * l + jnp.sum(p[h], axis=1, keepdims=True),
                        alpha * acc + lax.dot_general(p[h].astype(BF16), sv[h][1], _NT, preferred_element_type=F32)))
        return tuple(new)

    state = lax.fori_loop(0, own, body, tuple(state))
    for h in range(heads):
        o_ref[:, hs[h]] = state[h][2] / state[h][1]


def _moba_prompt(q, kt16, vt16, kt, n_seq, seq_len, gw):
    width = min(MOBA_HEADS * HEAD_DIM, gw)
    n_groups = gw // width
    nc = seq_len // Q_BLOCK
    kv_spec = pl.BlockSpec((width, seq_len), lambda b, hg, c: (b * 2 * n_groups + hg, 0))
    return pl.pallas_call(
        _moba_prompt_kernel,
        grid=(n_seq, n_groups, nc),
        in_specs=[pl.BlockSpec((Q_BLOCK, width), lambda b, hg, c: (b * nc + c, hg)), kv_spec, kv_spec, kv_spec],
        out_specs=pl.BlockSpec((Q_BLOCK, width), lambda b, hg, c: (b * nc + c, hg)),
        out_shape=jax.ShapeDtypeStruct((n_seq * seq_len, gw), F32),
        scratch_shapes=[pltpu.VMEM((128, width), F32)],
        compiler_params=_cparams(3),
        name="moba_prompt",
    )(q, kt16, vt16, kt)


def _sb_prompt_kernel(q_ref, kt_ref, vt_ref, o_ref):
    c = pl.program_id(2)
    tq = q_ref.shape[0]
    heads = q_ref.shape[1] // HEAD_DIM
    upper = _strict_lower(SB_KEYS)
    row = lax.broadcasted_iota(jnp.int32, (tq, SB_KEYS), 0)
    col = lax.broadcasted_iota(jnp.int32, (tq, SB_KEYS), 1)
    strict = col < row
    hs = [slice(h * HEAD_DIM, (h + 1) * HEAD_DIM) for h in range(heads)]
    q16 = [(q_ref[:, hs[h]] * ATT_SCALE).astype(BF16) for h in range(heads)]

    def tile(j, state, mask):
        keys = pl.ds(pl.multiple_of(j * SB_KEYS, SB_KEYS), SB_KEYS)
        z = [jnp.dot(q16[h], kt_ref[hs[h], keys], preferred_element_type=F32) for h in range(heads)]
        pairs = [_log_sigmoid_pair(zh) for zh in z]
        lk = [pr[0] if mask is None else jnp.where(mask, pr[0], 0.0) for pr in pairs]
        after = _suffix_sums(lk, upper)
        new = []
        for h in range(heads):
            r, acc = state[h]
            a = jnp.exp(pairs[h][1] + after[h] + r)
            if mask is not None:
                a = jnp.where(mask, a, 0.0)
            acc = acc + lax.dot_general(a.astype(BF16), vt_ref[hs[h], keys], _NT, preferred_element_type=F32)
            new.append((r + jnp.sum(lk[h], axis=1, keepdims=True), acc))
        return tuple(new)

    state = tile(c, tuple((jnp.zeros((tq, 1), F32), jnp.zeros((tq, HEAD_DIM), F32)) for _ in range(heads)), strict)

    def cond(carry):
        i, live, _ = carry
        return jnp.logical_and(i < c, live > 0)

    def body(carry):
        i, _, state = carry
        r_max = state[0][0]
        for h in range(1, heads):
            r_max = jnp.maximum(r_max, state[h][0])
        live = (jnp.max(r_max) > SB_NEGLIGIBLE).astype(jnp.int32)
        return i + 1, live, tile(c - 1 - i, state, None)

    _, _, state = lax.while_loop(cond, body, (jnp.int32(0), jnp.int32(1), state))
    for h in range(heads):
        o_ref[:, hs[h]] = state[h][1]


def _sb_prompt(q, kt16, vt16, n_seq, seq_len, gw):
    width = min(SB_HEADS * HEAD_DIM, gw)
    n_groups = gw // width
    nc = seq_len // SB_KEYS
    kv_spec = pl.BlockSpec((width, seq_len), lambda b, hg, c: (b * 2 * n_groups + n_groups + hg, 0))
    return pl.pallas_call(
        _sb_prompt_kernel,
        grid=(n_seq, n_groups, nc),
        in_specs=[pl.BlockSpec((SB_KEYS, width), lambda b, hg, c: (b * nc + c, n_groups + hg)), kv_spec, kv_spec],
        out_specs=pl.BlockSpec((SB_KEYS, width), lambda b, hg, c: (b * nc + c, hg)),
        out_shape=jax.ShapeDtypeStruct((n_seq * seq_len, gw), F32),
        compiler_params=_cparams(3),
        name="sb_prompt",
    )(q, kt16, vt16)


def _decode_kernel(pt_ref, q_ref, knew_ref, vnew_ref, *refs, n_pages, nq):
    del pt_ref
    kv_refs = refs[:2 * DECODE_PAGES]
    oa_ref, ob_ref, g_sc, m_sc, l_sc, o_sc, r_sc, accb_sc = refs[2 * DECODE_PAGES:]
    p = pl.program_id(1)
    n_steps = n_pages // DECODE_PAGES
    rows = q_ref.shape[1] // 2
    heads, hd, page = kv_refs[0].shape[1:]
    gh = heads // 2
    gw = gh * hd
    page_refs = [(kv_refs[2 * j + t // 2], t % 2) for j in range(DECODE_PAGES) for t in range(4)]
    pn = knew_ref.shape[2]
    pages_per_block = MOBA_BLOCK // page
    blocks_per_step = DECODE_PAGES // pages_per_block
    n_blocks = n_pages // pages_per_block
    lane_w = m_sc.shape[2]

    qa16 = (q_ref[0, :rows, :gw] * ATT_SCALE).astype(BF16)
    qb16 = (q_ref[0, rows:, gw:] * ATT_SCALE).astype(BF16)
    new_query = _imod(lax.broadcasted_iota(jnp.int32, (rows, pn), 0), nq)
    new_key = lax.broadcasted_iota(jnp.int32, (rows, pn), 1)

    def mat(ref_group):
        ref, g = ref_group
        return ref[0, g * gh:(g + 1) * gh].reshape(gw, page).astype(BF16)

    @pl.when(p == 0)
    def _init():
        z = lax.dot_general(qb16, knew_ref[0, 1].astype(BF16), _NT, preferred_element_type=F32)
        mask = new_key < new_query
        lk, ls = _log_sigmoid_pair(z)
        lk = jnp.where(mask, lk, 0.0)
        a = jnp.where(mask, jnp.exp(ls + _suffix_sums([lk], _strict_lower(pn))[0]), 0.0)
        accb_sc[...] = jnp.dot(a.astype(BF16), vnew_ref[0, 1].astype(BF16), preferred_element_type=F32)
        r_sc[...] = jnp.broadcast_to(jnp.sum(lk, axis=1, keepdims=True), r_sc.shape)

    all_pages = range(DECODE_PAGES)
    blocks = [range(b * pages_per_block, (b + 1) * pages_per_block) for b in range(blocks_per_step)]
    block_of = lambda vals, op: [functools.reduce(op, [vals[j] for j in pages]) for pages in blocks]
    s = [jnp.dot(qa16, mat(page_refs[4 * j]), preferred_element_type=F32) for j in all_pages]
    z = [jnp.dot(qb16, mat(page_refs[4 * j + 1]), preferred_element_type=F32) for j in all_pages]
    m_blk = block_of([jnp.max(sj, axis=1, keepdims=True) for sj in s], jnp.maximum)
    g_blk = block_of([jnp.sum(sj, axis=1, keepdims=True) for sj in s], jnp.add)
    pr = [jnp.exp(s[j] - m_blk[j // pages_per_block]) for j in all_pages]
    l_blk = block_of([jnp.sum(pj, axis=1, keepdims=True) for pj in pr], jnp.add)
    pv = [lax.dot_general(pr[j].astype(BF16), mat(page_refs[4 * j + 2]), _NT, preferred_element_type=F32)
          for j in all_pages]
    o_blk = block_of(pv, jnp.add)
    for b in range(blocks_per_step):
        n = n_blocks - 1 - (p * blocks_per_step + b)
        m_sc[n] = jnp.broadcast_to(m_blk[b], (rows, lane_w))
        l_sc[n] = jnp.broadcast_to(l_blk[b], (rows, lane_w))
        g_sc[n] = jnp.broadcast_to(g_blk[b], (rows, lane_w))
        o_sc[n] = o_blk[b]

    pairs = [_log_sigmoid_pair(zj) for zj in z]
    after = _suffix_sums([pr_[0] for pr_ in pairs], _strict_lower(page))
    lk_sum = [jnp.sum(pr_[0], axis=1, keepdims=True) for pr_ in pairs]
    r = [r_sc[:, :1]]
    for j in all_pages:
        r.append(r[j] + lk_sum[j])
    a = [jnp.exp(pairs[j][1] + after[j] + r[j]).astype(BF16) for j in all_pages]
    accb_sc[...] = functools.reduce(jnp.add, [accb_sc[...]] + [
        lax.dot_general(a[j], mat(page_refs[4 * j + 3]), _NT, preferred_element_type=F32) for j in all_pages])
    r_sc[...] = jnp.broadcast_to(r[DECODE_PAGES], r_sc.shape)

    @pl.when(p == n_steps - 1)
    def _finish():
        gates = _columns([g_sc[b][:, :1] for b in range(n_blocks)], n_blocks)
        bias = _top_block_bias(gates, n_blocks)
        sn = lax.dot_general(qa16, knew_ref[0, 0].astype(BF16), _NT, preferred_element_type=F32)
        sn = jnp.where(new_key <= new_query, sn, NEG_INF)
        m_own = jnp.max(sn, axis=1, keepdims=True)
        pn_ = jnp.exp(sn - m_own)
        l_own = jnp.sum(pn_, axis=1, keepdims=True)
        o_own = jnp.dot(pn_.astype(BF16), vnew_ref[0, 0].astype(BF16), preferred_element_type=F32)
        m_all = m_own
        for b in range(n_blocks):
            m_all = jnp.maximum(m_all, m_sc[b][:, :1] + bias[:, b:b + 1])
        w_own = jnp.exp(m_own - m_all)
        num = w_own * o_own
        den = w_own * l_own
        for b in range(n_blocks):
            w = jnp.exp(m_sc[b][:, :1] + bias[:, b:b + 1] - m_all)
            num = num + w * o_sc[b]
            den = den + w * l_sc[b][:, :1]
        outa = num / den
        rr = lax.broadcasted_iota(jnp.int32, (rows, gw), 0)
        cc = lax.broadcasted_iota(jnp.int32, (rows, gw), 1)
        diag = _idiv(rr, nq) == _idiv(cc, hd)
        si = lax.broadcasted_iota(jnp.int32, (oa_ref.shape[1], rows), 0)
        sr = lax.broadcasted_iota(jnp.int32, (oa_ref.shape[1], rows), 1)
        pick = jnp.where(_imod(sr, nq) == si, 1.0, 0.0)
        oa_ref[0] = jnp.dot(pick, jnp.where(diag, outa, 0.0), precision=HIGHEST, preferred_element_type=F32)
        ob_ref[0] = jnp.dot(pick, jnp.where(diag, accb_sc[...], 0.0), precision=HIGHEST,
                            preferred_element_type=F32)


def _decode_attention(page_table, qbd, knew, vnew, cache_kt, cache_vt, nq):
    n_dec, n_pages = page_table.shape
    rows = qbd.shape[1] // 2
    _, heads, hd, page = cache_kt.shape
    gh = heads // 2
    gw = gh * hd
    n_blocks = n_pages * page // MOBA_BLOCK
    per_seq = lambda a: pl.BlockSpec((1,) + a.shape[1:], lambda b, p, pt: (b,) + (0,) * (a.ndim - 1))
    page_spec = lambda j: pl.BlockSpec(
        (1, heads, hd, page), lambda b, p, pt: (pt[b, n_pages - 1 - (p * DECODE_PAGES + j)], 0, 0, 0))
    out_spec = pl.BlockSpec((1, OUT_ROWS, gw), lambda b, p, pt: (b, 0, 0))
    assert n_pages % DECODE_PAGES == 0 and DECODE_PAGES % (MOBA_BLOCK // page) == 0 and nq <= OUT_ROWS
    grid_spec = pltpu.PrefetchScalarGridSpec(
        num_scalar_prefetch=1,
        grid=(n_dec, n_pages // DECODE_PAGES),
        in_specs=[per_seq(qbd), per_seq(knew), per_seq(vnew)]
        + [page_spec(j) for j in range(DECODE_PAGES) for _ in range(2)],
        out_specs=[out_spec, out_spec],
        scratch_shapes=[pltpu.VMEM((n_blocks, rows, 128), F32),
                        pltpu.VMEM((n_blocks, rows, 128), F32),
                        pltpu.VMEM((n_blocks, rows, 128), F32),
                        pltpu.VMEM((n_blocks, rows, gw), F32),
                        pltpu.VMEM((rows, 128), F32),
                        pltpu.VMEM((rows, gw), F32)],
    )
    return pl.pallas_call(
        functools.partial(_decode_kernel, n_pages=n_pages, nq=nq),
        grid_spec=grid_spec,
        out_shape=[jax.ShapeDtypeStruct((n_dec, OUT_ROWS, gw), F32)] * 2,
        compiler_params=_cparams(2),
        name="decode_attention",
    )(page_table, qbd, knew, vnew, *([cache_kt, cache_vt] * DECODE_PAGES))


def _merge_kernel(oa_ref, ob_ref, x_ref, ga_ref, gb_ref, w_ref, g_ref, b_ref, y_ref, *, alpha):
    gw = oa_ref.shape[1]
    ya = _rms_norm(oa_ref[...], ga_ref[...]).astype(BF16)
    yb = _rms_norm(ob_ref[...], gb_ref[...]).astype(BF16)
    mix = (jnp.dot(ya, w_ref[:gw, :], preferred_element_type=F32)
           + jnp.dot(yb, w_ref[gw:, :], preferred_element_type=F32))
    y_ref[...] = _layer_norm(alpha * x_ref[...] + mix, g_ref[...], b_ref[...])


def _merge(oa, ob, x, g_a, g_b, w16, ln_g, ln_b, alpha):
    t, d = x.shape
    gw = oa.shape[1]
    rows = _row_tile(t)
    row = lambda width: pl.BlockSpec((rows, width), lambda i: (i, 0))
    full = lambda a: pl.BlockSpec(a.shape, lambda i: (0, 0))
    return pl.pallas_call(
        functools.partial(_merge_kernel, alpha=alpha),
        grid=(t // rows,),
        in_specs=[row(gw), row(gw), row(d), full(g_a), full(g_b), full(w16), full(ln_g), full(ln_b)],
        out_specs=row(d),
        out_shape=jax.ShapeDtypeStruct((t, d), F32),
        compiler_params=_cparams(1),
        name="merge_out_proj",
    )(oa, ob, x, g_a, g_b, w16, ln_g, ln_b)


def _mem_kv_kernel(m_ref, w_ref, k_ref, v_ref, k16_ref, v16_ref):
    width = k_ref.shape[1]
    m16 = m_ref[...].astype(BF16)
    k = jnp.dot(m16, w_ref[:, :width], preferred_element_type=F32)
    v = jnp.dot(m16, w_ref[:, width:], preferred_element_type=F32)
    k_ref[...] = k
    v_ref[...] = v
    k16_ref[...] = k.astype(BF16)
    v16_ref[...] = v.astype(BF16)


def _mem_kv(mem, w16):
    t, d = mem.shape
    width = w16.shape[1] // 2
    rows = _row_tile(t)
    row = lambda w: pl.BlockSpec((rows, w), lambda i: (i, 0))
    return pl.pallas_call(
        _mem_kv_kernel,
        grid=(t // rows,),
        in_specs=[row(d), pl.BlockSpec(w16.shape, lambda i: (0, 0))],
        out_specs=[row(width)] * 4,
        out_shape=[jax.ShapeDtypeStruct((t, width), F32)] * 2 + [jax.ShapeDtypeStruct((t, width), BF16)] * 2,
        compiler_params=_cparams(1),
        name="mem_kv",
    )(mem, w16)


def _mem_heads(q16, head_kv, row_mask=None):
    hd = q16.shape[1] // MEM_HEADS
    outs = []
    for h in range(MEM_HEADS):
        k16, v16 = head_kv(h)
        s = lax.dot_general(q16[:, h * hd:(h + 1) * hd], k16, _NT, preferred_element_type=F32)
        m = jnp.max(s, axis=1, keepdims=True)
        p = jnp.exp(s - m)
        l = jnp.sum(p, axis=1, keepdims=True)
        o = jnp.dot(p.astype(BF16), v16, preferred_element_type=F32) / l
        outs.append(o if row_mask is None else jnp.where(row_mask, o, 0.0))
    return jnp.concatenate(outs, axis=1)


def _mem_attend_kernel(x_ref, wq_ref, mk_ref, mv_ref, wo_ref, g_ref, b_ref, y_ref, *, alpha):
    x = x_ref[...]
    hd = x.shape[1] // MEM_HEADS
    q16 = (jnp.dot(x.astype(BF16), wq_ref[...], preferred_element_type=F32) * hd ** -0.5).astype(BF16)
    o = _mem_heads(q16, lambda h: (mk_ref[:, h * hd:(h + 1) * hd], mv_ref[:, h * hd:(h + 1) * hd]))
    y = jnp.dot(o.astype(BF16), wo_ref[...], preferred_element_type=F32)
    y_ref[...] = _layer_norm(alpha * x + y, g_ref[...], b_ref[...])


def _mem_attend_prompt(x, wq16, mk16, mv16, wo16, ln_g, ln_b, n_seq, alpha):
    t, d = x.shape
    seq_len = t // n_seq
    n_mem = mk16.shape[0] // n_seq
    rows = _row_tile(seq_len)
    tiles = seq_len // rows
    row = pl.BlockSpec((rows, d), lambda b, i: (b * tiles + i, 0))
    full = lambda a: pl.BlockSpec(a.shape, lambda b, i: (0, 0))
    mem = pl.BlockSpec((n_mem, d), lambda b, i: (b, 0))
    return pl.pallas_call(
        functools.partial(_mem_attend_kernel, alpha=alpha),
        grid=(n_seq, tiles),
        in_specs=[row, full(wq16), mem, mem, full(wo16), full(ln_g), full(ln_b)],
        out_specs=row,
        out_shape=jax.ShapeDtypeStruct((t, d), F32),
        compiler_params=_cparams(2),
        name="mem_attend_prompt",
    )(x, wq16, mk16, mv16, wo16, ln_g, ln_b)


def _mem_attend_sample_kernel(x_ref, wq_ref, mk_ref, mv_ref, wo_ref, g_ref, b_ref, y_ref, *, alpha, nq):
    x = x_ref[...]
    hd = x.shape[1] // MEM_HEADS
    q16 = (jnp.dot(x.astype(BF16), wq_ref[...], preferred_element_type=F32) * hd ** -0.5).astype(BF16)
    seq_of_row = _idiv(lax.broadcasted_iota(jnp.int32, (x.shape[0], 1), 0), nq)
    o = jnp.zeros(x.shape, F32)
    for g in range(mk_ref.shape[0]):
        head_kv = lambda h, g=g: (mk_ref[g, :, h * hd:(h + 1) * hd].astype(BF16),
                                  mv_ref[g, :, h * hd:(h + 1) * hd].astype(BF16))
        o = o + _mem_heads(q16, head_kv, seq_of_row == g)
    y = jnp.dot(o.astype(BF16), wo_ref[...], preferred_element_type=F32)
    y_ref[...] = _layer_norm(alpha * x + y, g_ref[...], b_ref[...])


def _mem_attend_sample(x, wq16, cache_mk, cache_mv, wo16, ln_g, ln_b, nq, alpha):
    t, d = x.shape
    n_dec, n_mem, _ = cache_mk.shape
    rows = MEM_GROUP * nq
    row = pl.BlockSpec((rows, d), lambda i: (i, 0))
    full = lambda a: pl.BlockSpec(a.shape, lambda i: (0, 0))
    mem = pl.BlockSpec((MEM_GROUP, n_mem, d), lambda i: (i, 0, 0))
    return pl.pallas_call(
        functools.partial(_mem_attend_sample_kernel, alpha=alpha, nq=nq),
        grid=(n_dec // MEM_GROUP,),
        in_specs=[row, full(wq16), mem, mem, full(wo16), full(ln_g), full(ln_b)],
        out_specs=row,
        out_shape=jax.ShapeDtypeStruct((t, d), F32),
        compiler_params=_cparams(1),
        name="mem_attend_sample",
    )(x, wq16, cache_mk, cache_mv, wo16, ln_g, ln_b)


def _router_kernel(x_ref, w_ref, b_ref, e_ref, g_ref, rank_ref, cnt_ref, run_sc):
    rows = x_ref.shape[0]
    logits = jnp.dot(x_ref[...], w_ref[...], precision=HIGHEST, preferred_element_type=F32) + b_ref[...]
    n_exp = logits.shape[1]
    e_iota = lax.broadcasted_iota(jnp.int32, logits.shape, 1)
    k_iota = lax.broadcasted_iota(jnp.int32, e_ref.shape, 1)
    top_e = jnp.zeros(e_ref.shape, jnp.int32)
    top_v = jnp.zeros(e_ref.shape, F32)
    onehot = []
    for k in range(TOP_K):
        mx = jnp.max(logits, axis=1, keepdims=True)
        idx = jnp.min(jnp.where(logits == mx, e_iota, n_exp), axis=1, keepdims=True)
        top_e = jnp.where(k_iota == k, idx, top_e)
        top_v = jnp.where(k_iota == k, mx, top_v)
        onehot.append(jnp.where(e_iota == idx, 1.0, 0.0))
        logits = jnp.where(e_iota == idx, NEG_INF, logits)
    w = jnp.exp(top_v - top_v[:, :1])
    e_ref[...] = top_e
    g_ref[...] = w / jnp.sum(w, axis=1, keepdims=True)

    @pl.when(pl.program_id(0) == 0)
    def _zero():
        run_sc[...] = jnp.zeros(run_sc.shape, F32)

    chosen = functools.reduce(jnp.add, onehot)
    before = jnp.dot(_strict_lower(rows), chosen.astype(BF16), preferred_element_type=F32) + run_sc[...]
    rank = jnp.zeros(e_ref.shape, F32)
    for k in range(TOP_K):
        rank = jnp.where(k_iota == k, jnp.sum(onehot[k] * before, axis=1, keepdims=True), rank)
    rank_ref[...] = rank.astype(jnp.int32)
    run_sc[...] = run_sc[...] + jnp.sum(chosen, axis=0, keepdims=True)
    cnt_ref[...] = run_sc[...]


def _router(x, w_router, b_router):
    t, d = x.shape
    n_exp = w_router.shape[1]
    rows = _row_tile(t)
    out = pl.BlockSpec((rows, TOP_K), lambda i: (i, 0))
    return pl.pallas_call(
        _router_kernel,
        grid=(t // rows,),
        in_specs=[pl.BlockSpec((rows, d), lambda i: (i, 0)),
                  pl.BlockSpec(w_router.shape, lambda i: (0, 0)),
                  pl.BlockSpec(b_router.shape, lambda i: (0, 0))],
        out_specs=[out, out, out, pl.BlockSpec((1, n_exp), lambda i: (0, 0))],
        out_shape=[jax.ShapeDtypeStruct((t, TOP_K), jnp.int32), jax.ShapeDtypeStruct((t, TOP_K), F32),
                   jax.ShapeDtypeStruct((t, TOP_K), jnp.int32), jax.ShapeDtypeStruct((1, n_exp), F32)],
        scratch_shapes=[pltpu.VMEM((1, n_exp), F32)],
        compiler_params=_cparams(1),
        name="router_top4",
    )(x, w_router, b_router)


def _row_copy(src_hbm, src_row, dst, r, sem):
    return pltpu.make_async_copy(src_hbm.at[pl.ds(src_row, 1)], dst.at[pl.ds(r, 1)], sem)


def _row_gather(src_hbm, idx_ref, dst, sem, n_rows):
    for r in range(n_rows):
        _row_copy(src_hbm, idx_ref[0, 0, r], dst, r, sem).start()


def _row_gather_wait(src_hbm, dst, sem, n_rows):
    def wait(r, carry):
        _row_copy(src_hbm, 0, dst, r, sem).wait()
        return carry
    lax.fori_loop(0, n_rows, wait, 0, unroll=DMA_UNROLL)


def _dispatch_kernel(idx_ref, x_ref, rows_in, rows_out, stage, sems):
    del rows_in
    s = pl.program_id(0)
    n_steps = pl.num_programs(0)
    rows = x_ref.shape[0]
    slot = s % 2

    def copy(buf, t, dst_row):
        return pltpu.make_async_copy(stage.at[buf, pl.ds(t, 1)], rows_out.at[pl.ds(dst_row, 1)], sems.at[buf])

    def drain(buf):
        def wait(r, carry):
            copy(buf, 0, 0).wait()
            return carry
        lax.fori_loop(0, TOP_K * rows, wait, 0, unroll=DMA_UNROLL)

    @pl.when(s >= 2)
    def _reuse():
        drain(slot)

    stage[slot] = x_ref[...]
    for k in range(TOP_K):
        for t in range(rows):
            copy(slot, t, idx_ref[0, 0, k * rows + t]).start()

    @pl.when(s == n_steps - 1)
    def _finish():
        @pl.when(s >= 1)
        def _other():
            drain(1 - slot)
        drain(slot)


def _dispatch(x, dest, n_rows):
    t, d = x.shape
    n_steps, _, n_copy = dest.shape
    rows = n_copy // TOP_K
    return pl.pallas_call(
        _dispatch_kernel,
        grid=(n_steps,),
        in_specs=[pl.BlockSpec((1, 1, n_copy), lambda s: (s, 0, 0), memory_space=pltpu.SMEM),
                  pl.BlockSpec((rows, d), lambda s: (s, 0)),
                  pl.BlockSpec(memory_space=pl.ANY)],
        out_specs=pl.BlockSpec(memory_space=pl.ANY),
        out_shape=jax.ShapeDtypeStruct((n_rows, d), F32),
        input_output_aliases={2: 0},
        scratch_shapes=[pltpu.VMEM((2, rows, d), F32), pltpu.SemaphoreType.DMA((2,))],
        compiler_params=_cparams(1),
        name="moe_dispatch",
    )(dest, x, jnp.zeros((n_rows, d), F32))


def _expert_ffn_kernel(be_ref, nu_ref, x_ref, wgu_ref, bgu_ref, wd_ref, bd_ref, y_ref, wgu16, wd16):
    s = pl.program_id(0)
    active = s < nu_ref[0]
    new_expert = jnp.logical_or(s == 0, be_ref[s] != be_ref[jnp.maximum(s - 1, 0)])

    @pl.when(jnp.logical_and(active, new_expert))
    def _cast_weights():
        wgu16[...] = wgu_ref[0].astype(BF16)
        wd16[...] = wd_ref[0].astype(BF16)

    @pl.when(active)
    def _compute():
        f = wd_ref.shape[1]
        hgu = jnp.dot(x_ref[...].astype(BF16), wgu16[...], preferred_element_type=F32) + bgu_ref[0]
        gate = jnp.minimum(hgu[:, :f], SWIGLU_LIMIT)
        up = jnp.clip(hgu[:, f:], -SWIGLU_LIMIT, SWIGLU_LIMIT)
        act = (up + 1.0) * gate * jax.nn.sigmoid(SWIGLU_ALPHA * gate)
        y_ref[...] = jnp.dot(act.astype(BF16), wd16[...], preferred_element_type=F32) + bd_ref[0]

    @pl.when(jnp.logical_not(active))
    def _unused():
        y_ref[...] = jnp.zeros(y_ref.shape, F32)


def _expert_ffn(x_rows, block_e, n_used, w_gu, b_gu, w_down, b_down, rows):
    n_exp, d, f2 = w_gu.shape
    f = f2 // 2
    n_blk = x_rows.shape[0] // rows
    by_expert = lambda shape: pl.BlockSpec(shape, lambda s, be, nu: (be[s], 0, 0))
    row_spec = pl.BlockSpec((rows, d), lambda s, be, nu: (s, 0))
    grid_spec = pltpu.PrefetchScalarGridSpec(
        num_scalar_prefetch=2,
        grid=(n_blk,),
        in_specs=[row_spec, by_expert((1, d, f2)), by_expert((1, 1, f2)), by_expert((1, f, d)),
                  by_expert((1, 1, d))],
        out_specs=row_spec,
        scratch_shapes=[pltpu.VMEM((d, f2), BF16), pltpu.VMEM((f, d), BF16)],
    )
    return pl.pallas_call(
        _expert_ffn_kernel,
        grid_spec=grid_spec,
        out_shape=jax.ShapeDtypeStruct((n_blk * rows, d), F32),
        compiler_params=_cparams(1),
        name="expert_ffn",
    )(block_e, n_used, x_rows, w_gu, b_gu.reshape(n_exp, 1, f2), w_down, b_down.reshape(n_exp, 1, d))


def _combine_kernel(idx_ref, y_hbm, x_ref, gate_ref, g_ref, b_ref, o_ref, ybuf, sems, *, alpha):
    s = pl.program_id(0)
    n_steps = pl.num_programs(0) - 1
    rows = x_ref.shape[0]
    n_copy = TOP_K * rows

    @pl.when(s < n_steps)
    def _fetch():
        _row_gather(y_hbm, idx_ref, ybuf.at[s % 2], sems.at[s % 2], n_copy)

    @pl.when(s >= 1)
    def _compute():
        slot = (s - 1) % 2
        _row_gather_wait(y_hbm, ybuf.at[slot], sems.at[slot], n_copy)
        gates = gate_ref[...]
        ffn = jnp.zeros(x_ref.shape, F32)
        for k in range(TOP_K):
            ffn = ffn + gates[:, k:k + 1] * ybuf[slot, k * rows:(k + 1) * rows, :]
        o_ref[...] = _layer_norm(alpha * x_ref[...] + ffn, g_ref[...], b_ref[...])


def _combine(y_rows, dest, x, gates, ln_g, ln_b, alpha):
    t, d = x.shape
    n_steps, _, n_copy = dest.shape
    rows = n_copy // TOP_K
    idx_spec = pl.BlockSpec((1, 1, n_copy), lambda s: (jnp.minimum(s, n_steps - 1), 0, 0),
                            memory_space=pltpu.SMEM)
    row = lambda w: pl.BlockSpec((rows, w), lambda s: (jnp.maximum(s - 1, 0), 0))
    full = lambda a: pl.BlockSpec(a.shape, lambda s: (0, 0))
    return pl.pallas_call(
        functools.partial(_combine_kernel, alpha=alpha),
        grid=(n_steps + 1,),
        in_specs=[idx_spec, pl.BlockSpec(memory_space=pl.ANY), row(d), row(TOP_K), full(ln_g), full(ln_b)],
        out_specs=row(d),
        out_shape=jax.ShapeDtypeStruct((t, d), F32),
        scratch_shapes=[pltpu.VMEM((2, n_copy, d), F32), pltpu.SemaphoreType.DMA((2,))],
        compiler_params=_cparams(1),
        name="moe_combine",
    )(dest, y_rows, x, gates, ln_g, ln_b)


def _dispatch_plan(top_e, rank, counts, rows):
    n_assign = top_e.size
    n_exp = counts.shape[0]
    flat_e = top_e.reshape(n_assign)
    padded = (counts + rows - 1) // rows * rows
    pad_end = jnp.cumsum(padded)
    pad_start = pad_end - padded
    dest = (pad_start[flat_e] + rank.reshape(n_assign)).astype(jnp.int32)
    n_blk = -(-(n_assign + n_exp * (rows - 1)) // rows)
    block_start = jnp.arange(n_blk, dtype=jnp.int32) * rows
    block_e = jnp.minimum(jnp.sum((pad_end[None, :] <= block_start[:, None]).astype(jnp.int32), axis=1), n_exp - 1)
    n_used = (pad_end[-1] // rows).astype(jnp.int32).reshape(1)
    return dest, block_e, n_used, n_blk


def _moe(tok, w_router, b_router, w_gu, b_gu, w_down, b_down, ln_g, ln_b, alpha):
    t, d = tok.shape
    n_exp = w_router.shape[1]
    top_e, gates, rank, counts = _router(tok, w_router, b_router.reshape(1, n_exp))
    dest, block_e, n_used, n_blk = _dispatch_plan(top_e, rank, counts[0].astype(jnp.int32), EXPERT_ROWS)
    steps = t // COMBINE_ROWS
    dest_steps = dest.reshape(steps, COMBINE_ROWS, TOP_K).transpose(0, 2, 1).reshape(steps, 1, TOP_K * COMBINE_ROWS)
    x_rows = _dispatch(tok, dest_steps, n_blk * EXPERT_ROWS)
    y_rows = _expert_ffn(x_rows, block_e, n_used, w_gu, b_gu, w_down, b_down, EXPERT_ROWS)
    return _combine(y_rows, dest_steps, tok, gates, ln_g, ln_b, alpha)


def _rope_tables(pos, n_heads):
    half = HEAD_DIM // 2
    inv_freq = ROPE_THETA ** (-jnp.arange(half, dtype=F32) / half)
    ang = pos.astype(F32)[:, None] * inv_freq[None, :]
    cos = jnp.cos(ang)
    sin = jnp.sin(ang)
    return (jnp.tile(jnp.concatenate([cos, cos], axis=1), (1, n_heads)),
            jnp.tile(jnp.concatenate([-sin, sin], axis=1), (1, n_heads)), cos.T, sin.T)


def _block_diag_queries(q, nq, n_heads):
    n_dec = q.shape[0] // nq
    width = n_heads * HEAD_DIM
    rows = jnp.tile(q.reshape(n_dec, 1, nq, width), (1, n_heads, 1, 1)).reshape(n_dec, n_heads * nq, width)
    r = jnp.arange(n_heads * nq)[:, None] // nq
    c = jnp.arange(width)[None, :] // HEAD_DIM
    return jnp.where((r == c)[None], rows, 0.0)


def kernel(x_prompt, x_sample, mem_prompt, cache_k, cache_v, cache_mem_k, cache_mem_v, page_table,
           w_in, g_moba, g_sb, w_out, ln1_g, ln1_b, w_mq, w_mkv, w_mo, ln2_g, ln2_b,
           w_router, b_router, w_gu, b_gu, w_down, b_down, ln3_g, ln3_b):
    n_seq, seq_len, d = x_prompt.shape
    n_dec, nq, _ = x_sample.shape
    depth = w_in.shape[0]
    gw = w_in.shape[2] // 6
    g_heads = gw // HEAD_DIM
    heads = 2 * g_heads
    n_pages = page_table.shape[1]
    page = cache_k.shape[2]
    past_len = n_pages * page
    n_mem = mem_prompt.shape[1]
    alpha = (2 * depth) ** 0.25
    n_tok_p = n_seq * seq_len
    n_tok_s = n_dec * nq
    assert seq_len % MOBA_BLOCK == 0 and past_len % MOBA_BLOCK == 0 and MOBA_BLOCK % page == 0
    assert nq <= NEW_PAD and n_dec % MEM_GROUP == 0
    assert (n_tok_p + n_tok_s) % COMBINE_ROWS == 0 and n_tok_s % PROJ_ROWS == 0 and seq_len % PROJ_ROWS == 0

    tables_p = _rope_tables(jnp.arange(seq_len), g_heads)
    tables_s = _rope_tables(jnp.tile(past_len + jnp.arange(nq), n_dec), g_heads)
    xp = x_prompt.reshape(n_tok_p, d)
    xs = x_sample.reshape(n_tok_s, d)
    row2 = lambda a: a.reshape(1, -1)

    def to_positions(at, n, length):
        return at.reshape(n, heads, HEAD_DIM, length).transpose(0, 3, 1, 2)

    def new_rows(at):
        a = at.reshape(2, gw, n_dec, nq).transpose(2, 0, 3, 1)
        return jnp.pad(a, ((0, 0), (0, 0), (0, NEW_PAD - nq), (0, 0)))

    outs = [[] for _ in range(6)]
    for l in range(depth):
        w = w_in[l].astype(BF16)
        col = lambda g: w[:, g * gw:(g + 1) * gw]
        wq16 = jnp.concatenate([col(0), col(3)], axis=1)
        wkvt16 = jnp.concatenate([col(1), col(4), col(2), col(5)], axis=1).T
        w_out16 = w_out[l].astype(BF16)
        q_p, kt_p, vt_p, kt16_p, vt16_p = _qkv_project(xp, wq16, wkvt16, *tables_p, PROJ_ROWS, seq_len)
        oa_p = _moba_prompt(q_p, kt16_p, vt16_p, kt_p, n_seq, seq_len, gw)
        ob_p = _sb_prompt(q_p, kt16_p, vt16_p, n_seq, seq_len, gw)
        outs[0].append(to_positions(kt_p, n_seq, seq_len))
        outs[1].append(to_positions(vt_p, n_seq, seq_len))
        q_s, kt_s, vt_s, _, _ = _qkv_project(xs, wq16, wkvt16, *tables_s, PROJ_ROWS, n_tok_s)
        oa_s, ob_s = _decode_attention(
            page_table, _block_diag_queries(q_s, nq, heads), new_rows(kt_s), new_rows(vt_s),
            cache_k[l].transpose(0, 2, 3, 1), cache_v[l].transpose(0, 2, 3, 1), nq)
        outs[2].append(to_positions(kt_s, 1, n_tok_s).reshape(n_dec, nq, heads, HEAD_DIM))
        outs[3].append(to_positions(vt_s, 1, n_tok_s).reshape(n_dec, nq, heads, HEAD_DIM))
        merge = functools.partial(_merge, g_a=row2(g_moba[l]), g_b=row2(g_sb[l]), w16=w_out16,
                                  ln_g=row2(ln1_g[l]), ln_b=row2(ln1_b[l]), alpha=alpha)
        xp = merge(oa_p, ob_p, xp)
        xs = merge(oa_s[:, :nq].reshape(n_tok_s, gw), ob_s[:, :nq].reshape(n_tok_s, gw), xs)
        mk, mv, mk16, mv16 = _mem_kv(mem_prompt.reshape(n_seq * n_mem, d), w_mkv[l].astype(BF16))
        outs[4].append(mk.reshape(n_seq, n_mem, MEM_HEADS, d // MEM_HEADS))
        outs[5].append(mv.reshape(n_seq, n_mem, MEM_HEADS, d // MEM_HEADS))
        wmq16 = w_mq[l].astype(BF16)
        wmo16 = w_mo[l].astype(BF16)
        xp = _mem_attend_prompt(xp, wmq16, mk16, mv16, wmo16, row2(ln2_g[l]), row2(ln2_b[l]), n_seq, alpha)
        xs = _mem_attend_sample(xs, wmq16, cache_mem_k[l].reshape(n_dec, n_mem, d),
                                cache_mem_v[l].reshape(n_dec, n_mem, d), wmo16,
                                row2(ln2_g[l]), row2(ln2_b[l]), nq, alpha)
        tok = jnp.concatenate([xp, xs], axis=0)
        tok = _moe(tok, w_router[l], b_router[l], w_gu[l], b_gu[l], w_down[l],
                   b_down[l], row2(ln3_g[l]), row2(ln3_b[l]), alpha)
        xp = tok[:n_tok_p]
        xs = tok[n_tok_p:]
    return (xp.reshape(n_seq, seq_len, d), xs.reshape(n_dec, nq, d)) + tuple(jnp.stack(o) for o in outs)
```

```python
import functools

import jax
import jax.numpy as jnp
from jax import lax
from jax.experimental import pallas as pl
from jax.experimental.pallas import tpu as pltpu

F32 = jnp.float32
BF16 = jnp.bfloat16
HIGHEST = lax.Precision.HIGHEST

HEAD_DIM = 64
MOBA_BLOCK = 256
MOBA_TOP_K = 3
ROPE_THETA = 10000.0
MEM_HEADS = 4
TOP_K = 4
SWIGLU_LIMIT = 7.0
SWIGLU_ALPHA = 1.702
LN_EPS = 1e-5
RMS_EPS = 1e-6
ATT_SCALE = HEAD_DIM ** -0.5
NEG_INF = float("-inf")
SB_NEGLIGIBLE = -105.0

PROJ_ROWS = 256
MOBA_QUERIES = 256
SB_KEYS = 128
MOBA_HEADS = 4
SB_HEADS = 8
DECODE_PAGES = 8
TOKEN_ROWS = 512
EXPERT_ROWS = 256
COMBINE_ROWS = 128
NEW_PAD = 16
OUT_ROWS = 8
MEM_GROUP = 4
DMA_UNROLL = 8
VMEM_LIMIT = 56 * 1024 * 1024

_NT = (((1,), (1,)), ((), ()))


def _cparams(n_axes):
    return pltpu.CompilerParams(dimension_semantics=("arbitrary",) * n_axes,
                                vmem_limit_bytes=VMEM_LIMIT)


def _row_tile(n_rows):
    rows = TOKEN_ROWS
    while n_rows % rows:
        rows //= 2
    assert rows >= 8, n_rows
    return rows


def _idiv(x, n):
    return x >> (n.bit_length() - 1) if n & (n - 1) == 0 else x // n


def _imod(x, n):
    return x & (n - 1) if n & (n - 1) == 0 else x % n


def _layer_norm(x, g, b):
    mu = jnp.mean(x, axis=-1, keepdims=True)
    xc = x - mu
    var = jnp.mean(xc * xc, axis=-1, keepdims=True)
    return xc * lax.rsqrt(var + LN_EPS) * g + b


def _rms_norm(x, g):
    return x * lax.rsqrt(jnp.mean(x * x, axis=-1, keepdims=True) + RMS_EPS) * g


def _log_sigmoid_pair(z):
    t = jnp.log1p(jnp.exp(-jnp.abs(z)))
    return -(jnp.maximum(z, 0.0) + t), jnp.minimum(z, 0.0) - t


def _suffix_sums(xs, upper):
    rows = xs[0].shape[0]
    hi = [x.astype(BF16) for x in xs]
    lo = [(x - xh.astype(F32)).astype(BF16) for x, xh in zip(xs, hi)]
    s = jnp.dot(jnp.concatenate(hi + lo, axis=0), upper, preferred_element_type=F32)
    n = len(xs)
    return [s[i * rows:(i + 1) * rows] + s[(n + i) * rows:(n + i + 1) * rows] for i in range(n)]


def _strict_lower(n):
    r = lax.broadcasted_iota(jnp.int32, (n, n), 0)
    c = lax.broadcasted_iota(jnp.int32, (n, n), 1)
    return jnp.where(r > c, 1.0, 0.0).astype(BF16)


def _top_block_bias(gates, n_valid):
    nb = gates.shape[1]
    n_iota = lax.broadcasted_iota(jnp.int32, gates.shape, 1)
    valid = n_iota < n_valid
    g = jnp.where(valid, gates, NEG_INF)
    cnt = jnp.zeros(gates.shape, jnp.int32)
    for m in range(nb):
        gm = g[:, m:m + 1]
        beats = jnp.where(gm > g, 1, jnp.where(gm == g, jnp.where(n_iota > m, 1, 0), 0))
        cnt = cnt + beats
    sel = jnp.where(valid, jnp.where(cnt < MOBA_TOP_K, 1, 0), 0)
    return jnp.where(sel == 1, 0.0, NEG_INF)


def _top_block_select_t(gates_t, n_valid):
    nb = gates_t.shape[0]
    n_iota = lax.broadcasted_iota(jnp.int32, gates_t.shape, 0)
    valid = n_iota < n_valid
    g = jnp.where(valid, gates_t, NEG_INF)
    cnt = jnp.zeros(gates_t.shape, jnp.int32)
    for m in range(nb):
        gm = g[m:m + 1, :]
        cnt = cnt + jnp.where(gm > g, 1, jnp.where(gm == g, jnp.where(n_iota > m, 1, 0), 0))
    return jnp.where(valid, jnp.where(cnt < MOBA_TOP_K, 1.0, 0.0), 0.0)


def _columns(cols, width):
    lane = lax.broadcasted_iota(jnp.int32, (cols[0].shape[0], width), 1)
    out = jnp.zeros((cols[0].shape[0], width), F32)
    for n, c in enumerate(cols):
        out = jnp.where(lane == n, c, out)
    return out


def _qkv_kernel(x_ref, wq_ref, wkv_ref, cos_ref, sin_ref, cost_ref, sint_ref,
                q_ref, kt_ref, vt_ref, kt16_ref, vt16_ref, *, gw):
    half = HEAD_DIM // 2
    x16 = x_ref[...].astype(BF16)
    cos = cos_ref[...]
    sin = sin_ref[...]
    lane = lax.broadcasted_iota(jnp.int32, cos.shape, 1)
    first_half = (lane & (HEAD_DIM - 1)) < half
    qa = jnp.dot(x16, wq_ref[:, :gw], preferred_element_type=F32)
    rot = jnp.where(first_half, pltpu.roll(qa, gw - half, 1), pltpu.roll(qa, half, 1))
    q_ref[:, :gw] = qa * cos + rot * sin
    q_ref[:, gw:] = jnp.dot(x16, wq_ref[:, gw:], preferred_element_type=F32)

    def proj_t(g):
        return lax.dot_general(wkv_ref[g * gw:(g + 1) * gw, :], x16, _NT, preferred_element_type=F32)

    def put(ref, ref16, r0, val):
        ref[r0:r0 + val.shape[0], :] = val
        ref16[r0:r0 + val.shape[0], :] = val.astype(BF16)

    kat = proj_t(0)
    cost = cost_ref[...]
    sint = sint_ref[...]
    for h in range(gw // HEAD_DIM):
        x1 = kat[h * HEAD_DIM:h * HEAD_DIM + half]
        x2 = kat[h * HEAD_DIM + half:(h + 1) * HEAD_DIM]
        put(kt_ref, kt16_ref, h * HEAD_DIM, x1 * cost - x2 * sint)
        put(kt_ref, kt16_ref, h * HEAD_DIM + half, x2 * cost + x1 * sint)
    put(kt_ref, kt16_ref, gw, proj_t(1))
    put(vt_ref, vt16_ref, 0, proj_t(2))
    put(vt_ref, vt16_ref, gw, proj_t(3))


def _qkv_project(x, wq16, wkvt16, cos, sin, cost, sint, rows, seq_len):
    t, d = x.shape
    gw = wq16.shape[1] // 2
    tiles = seq_len // rows
    full = lambda a: pl.BlockSpec(a.shape, lambda i: (0, 0))
    row_spec = lambda width: pl.BlockSpec((rows, width), lambda i: (i, 0))
    t_spec = pl.BlockSpec((2 * gw, rows), lambda i: (i // tiles, i % tiles))
    t_shape = lambda dt: jax.ShapeDtypeStruct((t // seq_len * 2 * gw, seq_len), dt)
    return pl.pallas_call(
        functools.partial(_qkv_kernel, gw=gw),
        grid=(t // rows,),
        in_specs=[row_spec(d), full(wq16), full(wkvt16),
                  pl.BlockSpec((rows, gw), lambda i: (i % tiles, 0)),
                  pl.BlockSpec((rows, gw), lambda i: (i % tiles, 0)),
                  pl.BlockSpec((HEAD_DIM // 2, rows), lambda i: (0, i % tiles)),
                  pl.BlockSpec((HEAD_DIM // 2, rows), lambda i: (0, i % tiles))],
        out_specs=[row_spec(2 * gw), t_spec, t_spec, t_spec, t_spec],
        out_shape=[jax.ShapeDtypeStruct((t, 2 * gw), F32), t_shape(F32), t_shape(F32), t_shape(BF16), t_shape(BF16)],
        compiler_params=_cparams(1),
        name="qkv_rope",
    )(x, wq16, wkvt16, cos, sin, cost, sint)


def _moba_prompt_kernel(q_ref, kt_ref, vt_ref, ktf_ref, o_ref, kbar_sc):
    c = pl.program_id(2)
    own = c // (MOBA_BLOCK // MOBA_QUERIES)
    tq = q_ref.shape[0]
    heads = q_ref.shape[1] // HEAD_DIM
    nb = kt_ref.shape[1] // MOBA_BLOCK

    nbp = -(-nb // 8) * 8

    @pl.when(c == 0)
    def _block_means():
        kbar_sc[...] = _columns([jnp.mean(ktf_ref[:, n * MOBA_BLOCK:(n + 1) * MOBA_BLOCK], axis=1, keepdims=True)
                                 for n in range(nb)], kbar_sc.shape[0]).T

    row = lax.broadcasted_iota(jnp.int32, (tq, MOBA_BLOCK), 0)
    col = lax.broadcasted_iota(jnp.int32, (tq, MOBA_BLOCK), 1)
    causal = own * MOBA_BLOCK + col <= c * tq + row
    n_iota = lax.broadcasted_iota(jnp.int32, (tq, nbp), 1)
    hs = [slice(h * HEAD_DIM, (h + 1) * HEAD_DIM) for h in range(heads)]
    q16, bias = [], []
    for h in range(heads):
        q = q_ref[:, hs[h]]
        gates_t = lax.dot_general(kbar_sc[:nbp, hs[h]], q, _NT, precision=HIGHEST, preferred_element_type=F32)
        bias.append(jnp.where(_top_block_select_t(gates_t, own).T > 0.5, 0.0, NEG_INF))
        q16.append((q * ATT_SCALE).astype(BF16))

    def scores(j, h):
        keys = pl.ds(pl.multiple_of(j * MOBA_BLOCK, MOBA_BLOCK), MOBA_BLOCK)
        return jnp.dot(q16[h], kt_ref[hs[h], keys], preferred_element_type=F32), vt_ref[hs[h], keys]

    state = []
    for h in range(heads):
        s, vb = scores(own, h)
        s = jnp.where(causal, s, NEG_INF)
        m = jnp.max(s, axis=1, keepdims=True)
        p = jnp.exp(s - m)
        state.append((m, jnp.sum(p, axis=1, keepdims=True),
                      lax.dot_general(p.astype(BF16), vb, _NT, preferred_element_type=F32)))

    def body(j, state):
        sv = [scores(j, h) for h in range(heads)]
        s = [sv[h][0] + jnp.sum(jnp.where(n_iota == j, bias[h], 0.0), axis=1, keepdims=True) for h in range(heads)]
        m_new = [jnp.maximum(state[h][0], jnp.max(s[h], axis=1, keepdims=True)) for h in range(heads)]
        p = [jnp.exp(s[h] - m_new[h]) for h in range(heads)]
        new = []
        for h in range(heads):
            m, l, acc = state[h]
            alpha = jnp.exp(m - m_new[h])
            new.append((m_new[h], alpha * l + jnp.sum(p[h], axis=1, keepdims=True),
                        alpha * acc + lax.dot_general(p[h].astype(BF16), sv[h][1], _NT, preferred_element_type=F32)))
        return tuple(new)

    state = lax.fori_loop(0, own, body, tuple(state))
    for h in range(heads):
        o_ref[:, hs[h]] = state[h][2] / state[h][1]


def _moba_prompt(q, kt16, vt16, kt, n_seq, seq_len, gw):
    width = min(MOBA_HEADS * HEAD_DIM, gw)
    n_groups = gw // width
    nc = seq_len // MOBA_QUERIES
    kv_spec = pl.BlockSpec((width, seq_len), lambda b, hg, c: (b * 2 * n_groups + hg, 0))
    q_spec = pl.BlockSpec((MOBA_QUERIES, width), lambda b, hg, c: (b * nc + c, hg))
    return pl.pallas_call(
        _moba_prompt_kernel,
        grid=(n_seq, n_groups, nc),
        in_specs=[q_spec, kv_spec, kv_spec, kv_spec],
        out_specs=q_spec,
        out_shape=jax.ShapeDtypeStruct((n_seq * seq_len, gw), F32),
        scratch_shapes=[pltpu.VMEM((128, width), F32)],
        compiler_params=_cparams(3),
        name="moba_prompt",
    )(q, kt16, vt16, kt)


def _sb_prompt_kernel(q_ref, kt_ref, vt_ref, o_ref):
    c = pl.program_id(2)
    tq = q_ref.shape[0]
    heads = q_ref.shape[1] // HEAD_DIM
    upper = _strict_lower(SB_KEYS)
    row = lax.broadcasted_iota(jnp.int32, (tq, SB_KEYS), 0)
    col = lax.broadcasted_iota(jnp.int32, (tq, SB_KEYS), 1)
    strict = col < row
    hs = [slice(h * HEAD_DIM, (h + 1) * HEAD_DIM) for h in range(heads)]
    q16 = [(q_ref[:, hs[h]] * ATT_SCALE).astype(BF16) for h in range(heads)]

    def tile(j, state, mask):
        keys = pl.ds(pl.multiple_of(j * SB_KEYS, SB_KEYS), SB_KEYS)
        z = [jnp.dot(q16[h], kt_ref[hs[h], keys], preferred_element_type=F32) for h in range(heads)]
        pairs = [_log_sigmoid_pair(zh) for zh in z]
        lk = [pr[0] if mask is None else jnp.where(mask, pr[0], 0.0) for pr in pairs]
        after = _suffix_sums(lk, upper)
        new = []
        for h in range(heads):
            r, acc = state[h]
            a = jnp.exp(pairs[h][1] + after[h] + r)
            if mask is not None:
                a = jnp.where(mask, a, 0.0)
            acc = acc + lax.dot_general(a.astype(BF16), vt_ref[hs[h], keys], _NT, preferred_element_type=F32)
            new.append((r + jnp.sum(lk[h], axis=1, keepdims=True), acc))
        return tuple(new)

    state = tile(c, tuple((jnp.zeros((tq, 1), F32), jnp.zeros((tq, HEAD_DIM), F32)) for _ in range(heads)), strict)

    def live(state):
        r_max = functools.reduce(jnp.maximum, [st[0] for st in state])
        return (jnp.max(r_max) > SB_NEGLIGIBLE).astype(jnp.int32)

    def cond(carry):
        i, alive, _ = carry
        return jnp.logical_and(i < c, alive > 0)

    def body(carry):
        i, _, state = carry
        state = tile(c - 1 - i, state, None)
        return i + 1, live(state), state

    _, _, state = lax.while_loop(cond, body, (jnp.int32(0), live(state), state))
    for h in range(heads):
        o_ref[:, hs[h]] = state[h][1]


def _sb_prompt(q, kt16, vt16, n_seq, seq_len, gw):
    width = min(SB_HEADS * HEAD_DIM, gw)
    n_groups = gw // width
    nc = seq_len // SB_KEYS
    kv_spec = pl.BlockSpec((width, seq_len), lambda b, hg, c: (b * 2 * n_groups + n_groups + hg, 0))
    return pl.pallas_call(
        _sb_prompt_kernel,
        grid=(n_seq, n_groups, nc),
        in_specs=[pl.BlockSpec((SB_KEYS, width), lambda b, hg, c: (b * nc + c, n_groups + hg)), kv_spec, kv_spec],
        out_specs=pl.BlockSpec((SB_KEYS, width), lambda b, hg, c: (b * nc + c, hg)),
        out_shape=jax.ShapeDtypeStruct((n_seq * seq_len, gw), F32),
        compiler_params=_cparams(3),
        name="sb_prompt",
    )(q, kt16, vt16)


def _decode_kernel(pt_ref, q_ref, knew_ref, vnew_ref, *refs, n_pages, nq):
    del pt_ref
    kv_refs = refs[:2 * DECODE_PAGES]
    oa_ref, ob_ref, g_sc, m_sc, l_sc, o_sc, r_sc, accb_sc = refs[2 * DECODE_PAGES:]
    p = pl.program_id(1)
    n_steps = n_pages // DECODE_PAGES
    rows = q_ref.shape[1] // 2
    heads, hd, page = kv_refs[0].shape[1:]
    gh = heads // 2
    gw = gh * hd
    page_refs = [(kv_refs[2 * j + t // 2], t % 2) for j in range(DECODE_PAGES) for t in range(4)]
    pn = knew_ref.shape[2]
    pages_per_block = MOBA_BLOCK // page
    blocks_per_step = DECODE_PAGES // pages_per_block
    n_blocks = n_pages // pages_per_block
    lane_w = m_sc.shape[2]

    qa16 = (q_ref[0, :rows, :gw] * ATT_SCALE).astype(BF16)
    qb16 = (q_ref[0, rows:, gw:] * ATT_SCALE).astype(BF16)
    new_query = _imod(lax.broadcasted_iota(jnp.int32, (rows, pn), 0), nq)
    new_key = lax.broadcasted_iota(jnp.int32, (rows, pn), 1)

    def mat(ref_group):
        ref, g = ref_group
        return ref[0, g * gh:(g + 1) * gh].reshape(gw, page).astype(BF16)

    @pl.when(p == 0)
    def _init():
        z = lax.dot_general(qb16, knew_ref[0, 1].astype(BF16), _NT, preferred_element_type=F32)
        mask = new_key < new_query
        lk, ls = _log_sigmoid_pair(z)
        lk = jnp.where(mask, lk, 0.0)
        a = jnp.where(mask, jnp.exp(ls + _suffix_sums([lk], _strict_lower(pn))[0]), 0.0)
        accb_sc[...] = jnp.dot(a.astype(BF16), vnew_ref[0, 1].astype(BF16), preferred_element_type=F32)
        r_sc[...] = jnp.broadcast_to(jnp.sum(lk, axis=1, keepdims=True), r_sc.shape)

    all_pages = range(DECODE_PAGES)
    blocks = [range(b * pages_per_block, (b + 1) * pages_per_block) for b in range(blocks_per_step)]
    block_of = lambda vals, op: [functools.reduce(op, [vals[j] for j in pages]) for pages in blocks]
    s = [jnp.dot(qa16, mat(page_refs[4 * j]), preferred_element_type=F32) for j in all_pages]
    z = [jnp.dot(qb16, mat(page_refs[4 * j + 1]), preferred_element_type=F32) for j in all_pages]
    m_blk = block_of([jnp.max(sj, axis=1, keepdims=True) for sj in s], jnp.maximum)
    g_blk = block_of([jnp.sum(sj, axis=1, keepdims=True) for sj in s], jnp.add)
    pr = [jnp.exp(s[j] - m_blk[j // pages_per_block]) for j in all_pages]
    l_blk = block_of([jnp.sum(pj, axis=1, keepdims=True) for pj in pr], jnp.add)
    pv = [lax.dot_general(pr[j].astype(BF16), mat(page_refs[4 * j + 2]), _NT, preferred_element_type=F32)
          for j in all_pages]
    o_blk = block_of(pv, jnp.add)
    for b in range(blocks_per_step):
        n = n_blocks - 1 - (p * blocks_per_step + b)
        m_sc[n] = jnp.broadcast_to(m_blk[b], (rows, lane_w))
        l_sc[n] = jnp.broadcast_to(l_blk[b], (rows, lane_w))
        g_sc[n] = jnp.broadcast_to(g_blk[b], (rows, lane_w))
        o_sc[n] = o_blk[b]

    pairs = [_log_sigmoid_pair(zj) for zj in z]
    after = _suffix_sums([pr_[0] for pr_ in pairs], _strict_lower(page))
    lk_sum = [jnp.sum(pr_[0], axis=1, keepdims=True) for pr_ in pairs]
    r = [r_sc[:, :1]]
    for j in all_pages:
        r.append(r[j] + lk_sum[j])
    a = [jnp.exp(pairs[j][1] + after[j] + r[j]).astype(BF16) for j in all_pages]
    accb_sc[...] = functools.reduce(jnp.add, [accb_sc[...]] + [
        lax.dot_general(a[j], mat(page_refs[4 * j + 3]), _NT, preferred_element_type=F32) for j in all_pages])
    r_sc[...] = jnp.broadcast_to(r[DECODE_PAGES], r_sc.shape)

    @pl.when(p == n_steps - 1)
    def _finish():
        gates = _columns([g_sc[b][:, :1] for b in range(n_blocks)], n_blocks)
        bias = _top_block_bias(gates, n_blocks)
        sn = lax.dot_general(qa16, knew_ref[0, 0].astype(BF16), _NT, preferred_element_type=F32)
        sn = jnp.where(new_key <= new_query, sn, NEG_INF)
        m_own = jnp.max(sn, axis=1, keepdims=True)
        pn_ = jnp.exp(sn - m_own)
        l_own = jnp.sum(pn_, axis=1, keepdims=True)
        o_own = jnp.dot(pn_.astype(BF16), vnew_ref[0, 0].astype(BF16), preferred_element_type=F32)
        m_all = m_own
        for b in range(n_blocks):
            m_all = jnp.maximum(m_all, m_sc[b][:, :1] + bias[:, b:b + 1])
        w_own = jnp.exp(m_own - m_all)
        num = w_own * o_own
        den = w_own * l_own
        for b in range(n_blocks):
            w = jnp.exp(m_sc[b][:, :1] + bias[:, b:b + 1] - m_all)
            num = num + w * o_sc[b]
            den = den + w * l_sc[b][:, :1]
        outa = num / den
        rr = lax.broadcasted_iota(jnp.int32, (rows, gw), 0)
        cc = lax.broadcasted_iota(jnp.int32, (rows, gw), 1)
        diag = _idiv(rr, nq) == _idiv(cc, hd)
        si = lax.broadcasted_iota(jnp.int32, (oa_ref.shape[1], rows), 0)
        sr = lax.broadcasted_iota(jnp.int32, (oa_ref.shape[1], rows), 1)
        pick = jnp.where(_imod(sr, nq) == si, 1.0, 0.0)
        oa_ref[0] = jnp.dot(pick, jnp.where(diag, outa, 0.0), precision=HIGHEST, preferred_element_type=F32)
        ob_ref[0] = jnp.dot(pick, jnp.where(diag, accb_sc[...], 0.0), precision=HIGHEST,
                            preferred_element_type=F32)


def _decode_attention(page_table, qbd, knew, vnew, cache_kt, cache_vt, nq):
    n_dec, n_pages = page_table.shape
    rows = qbd.shape[1] // 2
    _, heads, hd, page = cache_kt.shape
    gh = heads // 2
    gw = gh * hd
    n_blocks = n_pages * page // MOBA_BLOCK
    per_seq = lambda a: pl.BlockSpec((1,) + a.shape[1:], lambda b, p, pt: (b,) + (0,) * (a.ndim - 1))
    page_spec = lambda j: pl.BlockSpec(
        (1, heads, hd, page), lambda b, p, pt: (pt[b, n_pages - 1 - (p * DECODE_PAGES + j)], 0, 0, 0))
    out_spec = pl.BlockSpec((1, OUT_ROWS, gw), lambda b, p, pt: (b, 0, 0))
    assert n_pages % DECODE_PAGES == 0 and DECODE_PAGES % (MOBA_BLOCK // page) == 0 and nq <= OUT_ROWS
    grid_spec = pltpu.PrefetchScalarGridSpec(
        num_scalar_prefetch=1,
        grid=(n_dec, n_pages // DECODE_PAGES),
        in_specs=[per_seq(qbd), per_seq(knew), per_seq(vnew)]
        + [page_spec(j) for j in range(DECODE_PAGES) for _ in range(2)],
        out_specs=[out_spec, out_spec],
        scratch_shapes=[pltpu.VMEM((n_blocks, rows, 128), F32),
                        pltpu.VMEM((n_blocks, rows, 128), F32),
                        pltpu.VMEM((n_blocks, rows, 128), F32),
                        pltpu.VMEM((n_blocks, rows, gw), F32),
                        pltpu.VMEM((rows, 128), F32),
                        pltpu.VMEM((rows, gw), F32)],
    )
    return pl.pallas_call(
        functools.partial(_decode_kernel, n_pages=n_pages, nq=nq),
        grid_spec=grid_spec,
        out_shape=[jax.ShapeDtypeStruct((n_dec, OUT_ROWS, gw), F32)] * 2,
        compiler_params=_cparams(2),
        name="decode_attention",
    )(page_table, qbd, knew, vnew, *([cache_kt, cache_vt] * DECODE_PAGES))


def _merge_kernel(oa_ref, ob_ref, x_ref, ga_ref, gb_ref, w_ref, g_ref, b_ref, y_ref, *, alpha):
    gw = oa_ref.shape[1]
    ya = _rms_norm(oa_ref[...], ga_ref[...]).astype(BF16)
    yb = _rms_norm(ob_ref[...], gb_ref[...]).astype(BF16)
    mix = (jnp.dot(ya, w_ref[:gw, :], preferred_element_type=F32)
           + jnp.dot(yb, w_ref[gw:, :], preferred_element_type=F32))
    y_ref[...] = _layer_norm(alpha * x_ref[...] + mix, g_ref[...], b_ref[...])


def _merge(oa, ob, x, g_a, g_b, w16, ln_g, ln_b, alpha):
    t, d = x.shape
    gw = oa.shape[1]
    rows = _row_tile(t)
    row = lambda width: pl.BlockSpec((rows, width), lambda i: (i, 0))
    full = lambda a: pl.BlockSpec(a.shape, lambda i: (0, 0))
    return pl.pallas_call(
        functools.partial(_merge_kernel, alpha=alpha),
        grid=(t // rows,),
        in_specs=[row(gw), row(gw), row(d), full(g_a), full(g_b), full(w16), full(ln_g), full(ln_b)],
        out_specs=row(d),
        out_shape=jax.ShapeDtypeStruct((t, d), F32),
        compiler_params=_cparams(1),
        name="merge_out_proj",
    )(oa, ob, x, g_a, g_b, w16, ln_g, ln_b)


def _mem_kv_kernel(m_ref, w_ref, k_ref, v_ref, k16_ref, v16_ref):
    width = k_ref.shape[1]
    m16 = m_ref[...].astype(BF16)
    k = jnp.dot(m16, w_ref[:, :width], preferred_element_type=F32)
    v = jnp.dot(m16, w_ref[:, width:], preferred_element_type=F32)
    k_ref[...] = k
    v_ref[...] = v
    k16_ref[...] = k.astype(BF16)
    v16_ref[...] = v.astype(BF16)


def _mem_kv(mem, w16):
    t, d = mem.shape
    width = w16.shape[1] // 2
    rows = _row_tile(t)
    row = lambda w: pl.BlockSpec((rows, w), lambda i: (i, 0))
    return pl.pallas_call(
        _mem_kv_kernel,
        grid=(t // rows,),
        in_specs=[row(d), pl.BlockSpec(w16.shape, lambda i: (0, 0))],
        out_specs=[row(width)] * 4,
        out_shape=[jax.ShapeDtypeStruct((t, width), F32)] * 2 + [jax.ShapeDtypeStruct((t, width), BF16)] * 2,
        compiler_params=_cparams(1),
        name="mem_kv",
    )(mem, w16)


def _mem_heads(q16, head_kv, row_mask=None):
    hd = q16.shape[1] // MEM_HEADS
    outs = []
    for h in range(MEM_HEADS):
        k16, v16 = head_kv(h)
        s = lax.dot_general(q16[:, h * hd:(h + 1) * hd], k16, _NT, preferred_element_type=F32)
        m = jnp.max(s, axis=1, keepdims=True)
        p = jnp.exp(s - m)
        l = jnp.sum(p, axis=1, keepdims=True)
        o = jnp.dot(p.astype(BF16), v16, preferred_element_type=F32) / l
        outs.append(o if row_mask is None else jnp.where(row_mask, o, 0.0))
    return jnp.concatenate(outs, axis=1)


def _mem_attend_kernel(x_ref, wq_ref, mk_ref, mv_ref, wo_ref, g_ref, b_ref, y_ref, *, alpha):
    x = x_ref[...]
    hd = x.shape[1] // MEM_HEADS
    q16 = (jnp.dot(x.astype(BF16), wq_ref[...], preferred_element_type=F32) * hd ** -0.5).astype(BF16)
    o = _mem_heads(q16, lambda h: (mk_ref[:, h * hd:(h + 1) * hd], mv_ref[:, h * hd:(h + 1) * hd]))
    y = jnp.dot(o.astype(BF16), wo_ref[...], preferred_element_type=F32)
    y_ref[...] = _layer_norm(alpha * x + y, g_ref[...], b_ref[...])


def _mem_attend_prompt(x, wq16, mk16, mv16, wo16, ln_g, ln_b, n_seq, alpha):
    t, d = x.shape
    seq_len = t // n_seq
    n_mem = mk16.shape[0] // n_seq
    rows = _row_tile(seq_len)
    tiles = seq_len // rows
    row = pl.BlockSpec((rows, d), lambda b, i: (b * tiles + i, 0))
    full = lambda a: pl.BlockSpec(a.shape, lambda b, i: (0, 0))
    mem = pl.BlockSpec((n_mem, d), lambda b, i: (b, 0))
    return pl.pallas_call(
        functools.partial(_mem_attend_kernel, alpha=alpha),
        grid=(n_seq, tiles),
        in_specs=[row, full(wq16), mem, mem, full(wo16), full(ln_g), full(ln_b)],
        out_specs=row,
        out_shape=jax.ShapeDtypeStruct((t, d), F32),
        compiler_params=_cparams(2),
        name="mem_attend_prompt",
    )(x, wq16, mk16, mv16, wo16, ln_g, ln_b)


def _mem_attend_sample_kernel(x_ref, wq_ref, mk_ref, mv_ref, wo_ref, g_ref, b_ref, y_ref, *, alpha, nq):
    x = x_ref[...]
    hd = x.shape[1] // MEM_HEADS
    q16 = (jnp.dot(x.astype(BF16), wq_ref[...], preferred_element_type=F32) * hd ** -0.5).astype(BF16)
    seq_of_row = _idiv(lax.broadcasted_iota(jnp.int32, (x.shape[0], 1), 0), nq)
    o = jnp.zeros(x.shape, F32)
    for g in range(mk_ref.shape[0]):
        head_kv = lambda h, g=g: (mk_ref[g, :, h * hd:(h + 1) * hd].astype(BF16),
                                  mv_ref[g, :, h * hd:(h + 1) * hd].astype(BF16))
        o = o + _mem_heads(q16, head_kv, seq_of_row == g)
    y = jnp.dot(o.astype(BF16), wo_ref[...], preferred_element_type=F32)
    y_ref[...] = _layer_norm(alpha * x + y, g_ref[...], b_ref[...])


def _mem_attend_sample(x, wq16, cache_mk, cache_mv, wo16, ln_g, ln_b, nq, alpha):
    t, d = x.shape
    n_dec, n_mem, _ = cache_mk.shape
    rows = MEM_GROUP * nq
    row = pl.BlockSpec((rows, d), lambda i: (i, 0))
    full = lambda a: pl.BlockSpec(a.shape, lambda i: (0, 0))
    mem = pl.BlockSpec((MEM_GROUP, n_mem, d), lambda i: (i, 0, 0))
    return pl.pallas_call(
        functools.partial(_mem_attend_sample_kernel, alpha=alpha, nq=nq),
        grid=(n_dec // MEM_GROUP,),
        in_specs=[row, full(wq16), mem, mem, full(wo16), full(ln_g), full(ln_b)],
        out_specs=row,
        out_shape=jax.ShapeDtypeStruct((t, d), F32),
        compiler_params=_cparams(1),
        name="mem_attend_sample",
    )(x, wq16, cache_mk, cache_mv, wo16, ln_g, ln_b)


def _router_kernel(x_ref, w_ref, b_ref, e_ref, g_ref, rank_ref, cnt_ref, run_sc):
    rows = x_ref.shape[0]
    logits = jnp.dot(x_ref[...], w_ref[...], precision=HIGHEST, preferred_element_type=F32) + b_ref[...]
    n_exp = logits.shape[1]
    e_iota = lax.broadcasted_iota(jnp.int32, logits.shape, 1)
    k_iota = lax.broadcasted_iota(jnp.int32, e_ref.shape, 1)
    top_e = jnp.zeros(e_ref.shape, jnp.int32)
    top_v = jnp.zeros(e_ref.shape, F32)
    onehot = []
    for k in range(TOP_K):
        mx = jnp.max(logits, axis=1, keepdims=True)
        idx = jnp.min(jnp.where(logits == mx, e_iota, n_exp), axis=1, keepdims=True)
        top_e = jnp.where(k_iota == k, idx, top_e)
        top_v = jnp.where(k_iota == k, mx, top_v)
        onehot.append(jnp.where(e_iota == idx, 1.0, 0.0))
        logits = jnp.where(e_iota == idx, NEG_INF, logits)
    w = jnp.exp(top_v - top_v[:, :1])
    e_ref[...] = top_e
    g_ref[...] = w / jnp.sum(w, axis=1, keepdims=True)

    @pl.when(pl.program_id(0) == 0)
    def _zero():
        run_sc[...] = jnp.zeros(run_sc.shape, F32)

    chosen = functools.reduce(jnp.add, onehot)
    before = jnp.dot(_strict_lower(rows), chosen.astype(BF16), preferred_element_type=F32) + run_sc[...]
    rank = jnp.zeros(e_ref.shape, F32)
    for k in range(TOP_K):
        rank = jnp.where(k_iota == k, jnp.sum(onehot[k] * before, axis=1, keepdims=True), rank)
    rank_ref[...] = rank.astype(jnp.int32)
    run_sc[...] = run_sc[...] + jnp.sum(chosen, axis=0, keepdims=True)
    cnt_ref[...] = run_sc[...]


def _router(x, w_router, b_router):
    t, d = x.shape
    n_exp = w_router.shape[1]
    rows = _row_tile(t)
    out = pl.BlockSpec((rows, TOP_K), lambda i: (i, 0))
    return pl.pallas_call(
        _router_kernel,
        grid=(t // rows,),
        in_specs=[pl.BlockSpec((rows, d), lambda i: (i, 0)),
                  pl.BlockSpec(w_router.shape, lambda i: (0, 0)),
                  pl.BlockSpec(b_router.shape, lambda i: (0, 0))],
        out_specs=[out, out, out, pl.BlockSpec((1, n_exp), lambda i: (0, 0))],
        out_shape=[jax.ShapeDtypeStruct((t, TOP_K), jnp.int32), jax.ShapeDtypeStruct((t, TOP_K), F32),
                   jax.ShapeDtypeStruct((t, TOP_K), jnp.int32), jax.ShapeDtypeStruct((1, n_exp), F32)],
        scratch_shapes=[pltpu.VMEM((1, n_exp), F32)],
        compiler_params=_cparams(1),
        name="router_top4",
    )(x, w_router, b_router)


def _row_copy(src_hbm, src_row, dst, r, sem):
    return pltpu.make_async_copy(src_hbm.at[pl.ds(src_row, 1)], dst.at[pl.ds(r, 1)], sem)


def _row_gather(src_hbm, idx_ref, dst, sem, n_rows):
    for r in range(n_rows):
        _row_copy(src_hbm, idx_ref[0, 0, r], dst, r, sem).start()


def _row_gather_wait(src_hbm, dst, sem, n_rows):
    def wait(r, carry):
        _row_copy(src_hbm, 0, dst, r, sem).wait()
        return carry
    lax.fori_loop(0, n_rows, wait, 0, unroll=DMA_UNROLL)


def _dispatch_kernel(idx_ref, x_ref, rows_in, rows_out, stage, sems):
    del rows_in
    s = pl.program_id(0)
    n_steps = pl.num_programs(0)
    rows = x_ref.shape[0]
    slot = s % 2

    def copy(buf, t, dst_row):
        return pltpu.make_async_copy(stage.at[buf, pl.ds(t, 1)], rows_out.at[pl.ds(dst_row, 1)], sems.at[buf])

    def drain(buf):
        def wait(r, carry):
            copy(buf, 0, 0).wait()
            return carry
        lax.fori_loop(0, TOP_K * rows, wait, 0, unroll=DMA_UNROLL)

    @pl.when(s >= 2)
    def _reuse():
        drain(slot)

    stage[slot] = x_ref[...]
    for k in range(TOP_K):
        for t in range(rows):
            copy(slot, t, idx_ref[0, 0, k * rows + t]).start()

    @pl.when(s == n_steps - 1)
    def _finish():
        @pl.when(s >= 1)
        def _other():
            drain(1 - slot)
        drain(slot)


def _dispatch(x, dest, n_rows):
    t, d = x.shape
    n_steps, _, n_copy = dest.shape
    rows = n_copy // TOP_K
    return pl.pallas_call(
        _dispatch_kernel,
        grid=(n_steps,),
        in_specs=[pl.BlockSpec((1, 1, n_copy), lambda s: (s, 0, 0), memory_space=pltpu.SMEM),
                  pl.BlockSpec((rows, d), lambda s: (s, 0)),
                  pl.BlockSpec(memory_space=pl.ANY)],
        out_specs=pl.BlockSpec(memory_space=pl.ANY),
        out_shape=jax.ShapeDtypeStruct((n_rows, d), F32),
        input_output_aliases={2: 0},
        scratch_shapes=[pltpu.VMEM((2, rows, d), F32), pltpu.SemaphoreType.DMA((2,))],
        compiler_params=_cparams(1),
        name="moe_dispatch",
    )(dest, x, jnp.zeros((n_rows, d), F32))


def _expert_ffn_kernel(be_ref, nu_ref, x_ref, wgu_ref, bgu_ref, wd_ref, bd_ref, y_ref, wgu16, wd16):
    s = pl.program_id(0)
    active = s < nu_ref[0]
    new_expert = jnp.logical_or(s == 0, be_ref[s] != be_ref[jnp.maximum(s - 1, 0)])

    @pl.when(jnp.logical_and(active, new_expert))
    def _cast_weights():
        wgu16[...] = wgu_ref[0].astype(BF16)
        wd16[...] = wd_ref[0].astype(BF16)

    @pl.when(active)
    def _compute():
        f = wd_ref.shape[1]
        hgu = jnp.dot(x_ref[...].astype(BF16), wgu16[...], preferred_element_type=F32) + bgu_ref[0]
        gate = jnp.minimum(hgu[:, :f], SWIGLU_LIMIT)
        up = jnp.clip(hgu[:, f:], -SWIGLU_LIMIT, SWIGLU_LIMIT)
        act = (up + 1.0) * gate * jax.nn.sigmoid(SWIGLU_ALPHA * gate)
        y_ref[...] = jnp.dot(act.astype(BF16), wd16[...], preferred_element_type=F32) + bd_ref[0]

    @pl.when(jnp.logical_not(active))
    def _unused():
        y_ref[...] = jnp.zeros(y_ref.shape, F32)


def _expert_ffn(x_rows, block_e, n_used, w_gu, b_gu, w_down, b_down, rows):
    n_exp, d, f2 = w_gu.shape
    f = f2 // 2
    n_blk = x_rows.shape[0] // rows
    by_expert = lambda shape: pl.BlockSpec(shape, lambda s, be, nu: (be[s], 0, 0))
    row_spec = pl.BlockSpec((rows, d), lambda s, be, nu: (s, 0))
    grid_spec = pltpu.PrefetchScalarGridSpec(
        num_scalar_prefetch=2,
        grid=(n_blk,),
        in_specs=[row_spec, by_expert((1, d, f2)), by_expert((1, 1, f2)), by_expert((1, f, d)),
                  by_expert((1, 1, d))],
        out_specs=row_spec,
        scratch_shapes=[pltpu.VMEM((d, f2), BF16), pltpu.VMEM((f, d), BF16)],
    )
    return pl.pallas_call(
        _expert_ffn_kernel,
        grid_spec=grid_spec,
        out_shape=jax.ShapeDtypeStruct((n_blk * rows, d), F32),
        compiler_params=_cparams(1),
        name="expert_ffn",
    )(block_e, n_used, x_rows, w_gu, b_gu.reshape(n_exp, 1, f2), w_down, b_down.reshape(n_exp, 1, d))


def _combine_kernel(idx_ref, y_hbm, x_ref, gate_ref, g_ref, b_ref, o_ref, ybuf, sems, *, alpha):
    s = pl.program_id(0)
    n_steps = pl.num_programs(0) - 1
    rows = x_ref.shape[0]
    n_copy = TOP_K * rows

    @pl.when(s < n_steps)
    def _fetch():
        _row_gather(y_hbm, idx_ref, ybuf.at[s % 2], sems.at[s % 2], n_copy)

    @pl.when(s >= 1)
    def _compute():
        slot = (s - 1) % 2
        _row_gather_wait(y_hbm, ybuf.at[slot], sems.at[slot], n_copy)
        gates = gate_ref[...]
        ffn = jnp.zeros(x_ref.shape, F32)
        for k in range(TOP_K):
            ffn = ffn + gates[:, k:k + 1] * ybuf[slot, k * rows:(k + 1) * rows, :]
        o_ref[...] = _layer_norm(alpha * x_ref[...] + ffn, g_ref[...], b_ref[...])


def _combine(y_rows, dest, x, gates, ln_g, ln_b, alpha):
    t, d = x.shape
    n_steps, _, n_copy = dest.shape
    rows = n_copy // TOP_K
    idx_spec = pl.BlockSpec((1, 1, n_copy), lambda s: (jnp.minimum(s, n_steps - 1), 0, 0),
                            memory_space=pltpu.SMEM)
    row = lambda w: pl.BlockSpec((rows, w), lambda s: (jnp.maximum(s - 1, 0), 0))
    full = lambda a: pl.BlockSpec(a.shape, lambda s: (0, 0))
    return pl.pallas_call(
        functools.partial(_combine_kernel, alpha=alpha),
        grid=(n_steps + 1,),
        in_specs=[idx_spec, pl.BlockSpec(memory_space=pl.ANY), row(d), row(TOP_K), full(ln_g), full(ln_b)],
        out_specs=row(d),
        out_shape=jax.ShapeDtypeStruct((t, d), F32),
        scratch_shapes=[pltpu.VMEM((2, n_copy, d), F32), pltpu.SemaphoreType.DMA((2,))],
        compiler_params=_cparams(1),
        name="moe_combine",
    )(dest, y_rows, x, gates, ln_g, ln_b)


def _dispatch_plan(top_e, rank, counts, rows):
    n_assign = top_e.size
    n_exp = counts.shape[0]
    flat_e = top_e.reshape(n_assign)
    padded = (counts + rows - 1) // rows * rows
    pad_end = jnp.cumsum(padded)
    pad_start = pad_end - padded
    dest = (pad_start[flat_e] + rank.reshape(n_assign)).astype(jnp.int32)
    n_blk = -(-(n_assign + n_exp * (rows - 1)) // rows)
    block_start = jnp.arange(n_blk, dtype=jnp.int32) * rows
    block_e = jnp.minimum(jnp.sum((pad_end[None, :] <= block_start[:, None]).astype(jnp.int32), axis=1), n_exp - 1)
    n_used = (pad_end[-1] // rows).astype(jnp.int32).reshape(1)
    return dest, block_e, n_used, n_blk


def _moe(tok, w_router, b_router, w_gu, b_gu, w_down, b_down, ln_g, ln_b, alpha):
    t, d = tok.shape
    n_exp = w_router.shape[1]
    top_e, gates, rank, counts = _router(tok, w_router, b_router.reshape(1, n_exp))
    dest, block_e, n_used, n_blk = _dispatch_plan(top_e, rank, counts[0].astype(jnp.int32), EXPERT_ROWS)
    steps = t // COMBINE_ROWS
    dest_steps = dest.reshape(steps, COMBINE_ROWS, TOP_K).transpose(0, 2, 1).reshape(steps, 1, TOP_K * COMBINE_ROWS)
    x_rows = _dispatch(tok, dest_steps, n_blk * EXPERT_ROWS)
    y_rows = _expert_ffn(x_rows, block_e, n_used, w_gu, b_gu, w_down, b_down, EXPERT_ROWS)
    return _combine(y_rows, dest_steps, tok, gates, ln_g, ln_b, alpha)


def _rope_tables(pos, n_heads):
    half = HEAD_DIM // 2
    inv_freq = ROPE_THETA ** (-jnp.arange(half, dtype=F32) / half)
    ang = pos.astype(F32)[:, None] * inv_freq[None, :]
    cos = jnp.cos(ang)
    sin = jnp.sin(ang)
    return (jnp.tile(jnp.concatenate([cos, cos], axis=1), (1, n_heads)),
            jnp.tile(jnp.concatenate([-sin, sin], axis=1), (1, n_heads)), cos.T, sin.T)


def _block_diag_queries(q, nq, n_heads):
    n_dec = q.shape[0] // nq
    width = n_heads * HEAD_DIM
    rows = jnp.tile(q.reshape(n_dec, 1, nq, width), (1, n_heads, 1, 1)).reshape(n_dec, n_heads * nq, width)
    r = jnp.arange(n_heads * nq)[:, None] // nq
    c = jnp.arange(width)[None, :] // HEAD_DIM
    return jnp.where((r == c)[None], rows, 0.0)


def kernel(x_prompt, x_sample, mem_prompt, cache_k, cache_v, cache_mem_k, cache_mem_v, page_table,
           w_in, g_moba, g_sb, w_out, ln1_g, ln1_b, w_mq, w_mkv, w_mo, ln2_g, ln2_b,
           w_router, b_router, w_gu, b_gu, w_down, b_down, ln3_g, ln3_b):
    n_seq, seq_len, d = x_prompt.shape
    n_dec, nq, _ = x_sample.shape
    depth = w_in.shape[0]
    gw = w_in.shape[2] // 6
    g_heads = gw // HEAD_DIM
    heads = 2 * g_heads
    n_pages = page_table.shape[1]
    page = cache_k.shape[2]
    past_len = n_pages * page
    n_mem = mem_prompt.shape[1]
    alpha = (2 * depth) ** 0.25
    n_tok_p = n_seq * seq_len
    n_tok_s = n_dec * nq
    assert seq_len % MOBA_BLOCK == 0 and past_len % MOBA_BLOCK == 0 and MOBA_BLOCK % page == 0
    assert nq <= NEW_PAD and n_dec % MEM_GROUP == 0
    assert (n_tok_p + n_tok_s) % COMBINE_ROWS == 0 and n_tok_s % PROJ_ROWS == 0 and seq_len % PROJ_ROWS == 0

    tables_p = _rope_tables(jnp.arange(seq_len), g_heads)
    tables_s = _rope_tables(jnp.tile(past_len + jnp.arange(nq), n_dec), g_heads)
    xp = x_prompt.reshape(n_tok_p, d)
    xs = x_sample.reshape(n_tok_s, d)
    row2 = lambda a: a.reshape(1, -1)

    def to_positions(at, n, length):
        return at.reshape(n, heads, HEAD_DIM, length).transpose(0, 3, 1, 2)

    def new_rows(at):
        a = at.reshape(2, gw, n_dec, nq).transpose(2, 0, 3, 1)
        return jnp.pad(a, ((0, 0), (0, 0), (0, NEW_PAD - nq), (0, 0)))

    outs = [[] for _ in range(6)]
    for l in range(depth):
        w = w_in[l].astype(BF16)
        col = lambda g: w[:, g * gw:(g + 1) * gw]
        wq16 = jnp.concatenate([col(0), col(3)], axis=1)
        wkvt16 = jnp.concatenate([col(1), col(4), col(2), col(5)], axis=1).T
        w_out16 = w_out[l].astype(BF16)
        q_p, kt_p, vt_p, kt16_p, vt16_p = _qkv_project(xp, wq16, wkvt16, *tables_p, PROJ_ROWS, seq_len)
        oa_p = _moba_prompt(q_p, kt16_p, vt16_p, kt_p, n_seq, seq_len, gw)
        ob_p = _sb_prompt(q_p, kt16_p, vt16_p, n_seq, seq_len, gw)
        outs[0].append(to_positions(kt_p, n_seq, seq_len))
        outs[1].append(to_positions(vt_p, n_seq, seq_len))
        q_s, kt_s, vt_s, _, _ = _qkv_project(xs, wq16, wkvt16, *tables_s, PROJ_ROWS, n_tok_s)
        oa_s, ob_s = _decode_attention(
            page_table, _block_diag_queries(q_s, nq, heads), new_rows(kt_s), new_rows(vt_s),
            cache_k[l].transpose(0, 2, 3, 1), cache_v[l].transpose(0, 2, 3, 1), nq)
        outs[2].append(to_positions(kt_s, 1, n_tok_s).reshape(n_dec, nq, heads, HEAD_DIM))
        outs[3].append(to_positions(vt_s, 1, n_tok_s).reshape(n_dec, nq, heads, HEAD_DIM))
        merge = functools.partial(_merge, g_a=row2(g_moba[l]), g_b=row2(g_sb[l]), w16=w_out16,
                                  ln_g=row2(ln1_g[l]), ln_b=row2(ln1_b[l]), alpha=alpha)
        xp = merge(oa_p, ob_p, xp)
        xs = merge(oa_s[:, :nq].reshape(n_tok_s, gw), ob_s[:, :nq].reshape(n_tok_s, gw), xs)
        mk, mv, mk16, mv16 = _mem_kv(mem_prompt.reshape(n_seq * n_mem, d), w_mkv[l].astype(BF16))
        outs[4].append(mk.reshape(n_seq, n_mem, MEM_HEADS, d // MEM_HEADS))
        outs[5].append(mv.reshape(n_seq, n_mem, MEM_HEADS, d // MEM_HEADS))
        wmq16 = w_mq[l].astype(BF16)
        wmo16 = w_mo[l].astype(BF16)
        xp = _mem_attend_prompt(xp, wmq16, mk16, mv16, wmo16, row2(ln2_g[l]), row2(ln2_b[l]), n_seq, alpha)
        xs = _mem_attend_sample(xs, wmq16, cache_mem_k[l].reshape(n_dec, n_mem, d),
                                cache_mem_v[l].reshape(n_dec, n_mem, d), wmo16,
                                row2(ln2_g[l]), row2(ln2_b[l]), nq, alpha)
        tok = jnp.concatenate([xp, xs], axis=0)
        tok = _moe(tok, w_router[l], b_router[l], w_gu[l], b_gu[l], w_down[l],
                   b_down[l], row2(ln3_g[l]), row2(ln3_b[l]), alpha)
        xp = tok[:n_tok_p]
        xs = tok[n_tok_p:]
    return (xp.reshape(n_seq, seq_len, d), xs.reshape(n_dec, nq, d)) + tuple(jnp.stack(o) for o in outs)
```

```python
import functools

import jax
import jax.numpy as jnp
from jax import lax
from jax.experimental import pallas as pl
from jax.experimental.pallas import tpu as pltpu

F32 = jnp.float32
BF16 = jnp.bfloat16
HIGHEST = lax.Precision.HIGHEST

HEAD_DIM = 64
MOBA_BLOCK = 256
MOBA_TOP_K = 3
ROPE_THETA = 10000.0
MEM_HEADS = 4
TOP_K = 4
SWIGLU_LIMIT = 7.0
SWIGLU_ALPHA = 1.702
LN_EPS = 1e-5
RMS_EPS = 1e-6
ATT_SCALE = HEAD_DIM ** -0.5
NEG_INF = float("-inf")
SB_NEGLIGIBLE = -105.0

PROJ_ROWS = 256
MOBA_QUERIES = 256
SB_KEYS = 128
MOBA_HEADS = 4
SB_HEADS = 8
DECODE_PAGES = 8
TOKEN_ROWS = 512
EXPERT_ROWS = 256
COMBINE_ROWS = 128
NEW_PAD = 16
OUT_ROWS = 8
MEM_GROUP = 4
DMA_UNROLL = 8
VMEM_LIMIT = 56 * 1024 * 1024

_NT = (((1,), (1,)), ((), ()))


def _cparams(n_axes):
    return pltpu.CompilerParams(dimension_semantics=("arbitrary",) * n_axes,
                                vmem_limit_bytes=VMEM_LIMIT)


def _row_tile(n_rows):
    rows = TOKEN_ROWS
    while n_rows % rows:
        rows //= 2
    assert rows >= 8, n_rows
    return rows


def _idiv(x, n):
    return x >> (n.bit_length() - 1) if n & (n - 1) == 0 else x // n


def _imod(x, n):
    return x & (n - 1) if n & (n - 1) == 0 else x % n


def _layer_norm(x, g, b):
    mu = jnp.mean(x, axis=-1, keepdims=True)
    xc = x - mu
    var = jnp.mean(xc * xc, axis=-1, keepdims=True)
    return xc * lax.rsqrt(var + LN_EPS) * g + b


def _rms_norm(x, g):
    return x * lax.rsqrt(jnp.mean(x * x, axis=-1, keepdims=True) + RMS_EPS) * g


def _log_sigmoid_pair(z):
    t = jnp.log1p(jnp.exp(-jnp.abs(z)))
    return -(jnp.maximum(z, 0.0) + t), jnp.minimum(z, 0.0) - t


def _suffix_sums(xs, upper):
    rows = xs[0].shape[0]
    hi = [x.astype(BF16) for x in xs]
    lo = [(x - xh.astype(F32)).astype(BF16) for x, xh in zip(xs, hi)]
    s = jnp.dot(jnp.concatenate(hi + lo, axis=0), upper, preferred_element_type=F32)
    n = len(xs)
    return [s[i * rows:(i + 1) * rows] + s[(n + i) * rows:(n + i + 1) * rows] for i in range(n)]


def _strict_lower(n):
    r = lax.broadcasted_iota(jnp.int32, (n, n), 0)
    c = lax.broadcasted_iota(jnp.int32, (n, n), 1)
    return jnp.where(r > c, 1.0, 0.0).astype(BF16)


def _top_block_bias(gates, n_valid):
    nb = gates.shape[1]
    n_iota = lax.broadcasted_iota(jnp.int32, gates.shape, 1)
    valid = n_iota < n_valid
    g = jnp.where(valid, gates, NEG_INF)
    cnt = jnp.zeros(gates.shape, jnp.int32)
    for m in range(nb):
        gm = g[:, m:m + 1]
        beats = jnp.where(gm > g, 1, jnp.where(gm == g, jnp.where(n_iota > m, 1, 0), 0))
        cnt = cnt + beats
    sel = jnp.where(valid, jnp.where(cnt < MOBA_TOP_K, 1, 0), 0)
    return jnp.where(sel == 1, 0.0, NEG_INF)


def _top_block_select_t(gates_t, n_valid):
    nb = gates_t.shape[0]
    n_iota = lax.broadcasted_iota(jnp.int32, gates_t.shape, 0)
    valid = n_iota < n_valid
    g = jnp.where(valid, gates_t, NEG_INF)
    cnt = jnp.zeros(gates_t.shape, jnp.int32)
    for m in range(nb):
        gm = g[m:m + 1, :]
        cnt = cnt + jnp.where(gm > g, 1, jnp.where(gm == g, jnp.where(n_iota > m, 1, 0), 0))
    return jnp.where(valid, jnp.where(cnt < MOBA_TOP_K, 1.0, 0.0), 0.0)


def _columns(cols, width):
    lane = lax.broadcasted_iota(jnp.int32, (cols[0].shape[0], width), 1)
    out = jnp.zeros((cols[0].shape[0], width), F32)
    for n, c in enumerate(cols):
        out = jnp.where(lane == n, c, out)
    return out


def _qkv_kernel(x_ref, wq_ref, wkv_ref, cos_ref, sin_ref, cost_ref, sint_ref,
                q_ref, kt_ref, vt_ref, kt16_ref, vt16_ref, *, gw):
    half = HEAD_DIM // 2
    x16 = x_ref[...].astype(BF16)
    cos = cos_ref[...]
    sin = sin_ref[...]
    lane = lax.broadcasted_iota(jnp.int32, cos.shape, 1)
    first_half = (lane & (HEAD_DIM - 1)) < half
    qa = jnp.dot(x16, wq_ref[:, :gw], preferred_element_type=F32)
    rot = jnp.where(first_half, pltpu.roll(qa, gw - half, 1), pltpu.roll(qa, half, 1))
    q_ref[:, :gw] = qa * cos + rot * sin
    q_ref[:, gw:] = jnp.dot(x16, wq_ref[:, gw:], preferred_element_type=F32)

    def proj_t(g):
        return lax.dot_general(wkv_ref[g * gw:(g + 1) * gw, :], x16, _NT, preferred_element_type=F32)

    def put(ref, ref16, r0, val):
        ref[r0:r0 + val.shape[0], :] = val
        ref16[r0:r0 + val.shape[0], :] = val.astype(BF16)

    kat = proj_t(0)
    cost = cost_ref[...]
    sint = sint_ref[...]
    for h in range(gw // HEAD_DIM):
        x1 = kat[h * HEAD_DIM:h * HEAD_DIM + half]
        x2 = kat[h * HEAD_DIM + half:(h + 1) * HEAD_DIM]
        put(kt_ref, kt16_ref, h * HEAD_DIM, x1 * cost - x2 * sint)
        put(kt_ref, kt16_ref, h * HEAD_DIM + half, x2 * cost + x1 * sint)
    put(kt_ref, kt16_ref, gw, proj_t(1))
    put(vt_ref, vt16_ref, 0, proj_t(2))
    put(vt_ref, vt16_ref, gw, proj_t(3))


def _qkv_project(x, wq16, wkvt16, cos, sin, cost, sint, rows, seq_len):
    t, d = x.shape
    gw = wq16.shape[1] // 2
    tiles = seq_len // rows
    full = lambda a: pl.BlockSpec(a.shape, lambda i: (0, 0))
    row_spec = lambda width: pl.BlockSpec((rows, width), lambda i: (i, 0))
    t_spec = pl.BlockSpec((2 * gw, rows), lambda i: (i // tiles, i % tiles))
    t_shape = lambda dt: jax.ShapeDtypeStruct((t // seq_len * 2 * gw, seq_len), dt)
    return pl.pallas_call(
        functools.partial(_qkv_kernel, gw=gw),
        grid=(t // rows,),
        in_specs=[row_spec(d), full(wq16), full(wkvt16),
                  pl.BlockSpec((rows, gw), lambda i: (i % tiles, 0)),
                  pl.BlockSpec((rows, gw), lambda i: (i % tiles, 0)),
                  pl.BlockSpec((HEAD_DIM // 2, rows), lambda i: (0, i % tiles)),
                  pl.BlockSpec((HEAD_DIM // 2, rows), lambda i: (0, i % tiles))],
        out_specs=[row_spec(2 * gw), t_spec, t_spec, t_spec, t_spec],
        out_shape=[jax.ShapeDtypeStruct((t, 2 * gw), F32), t_shape(F32), t_shape(F32), t_shape(BF16), t_shape(BF16)],
        compiler_params=_cparams(1),
        name="qkv_rope",
    )(x, wq16, wkvt16, cos, sin, cost, sint)


def _moba_prompt_kernel(q_ref, kt_ref, vt_ref, ktf_ref, o_ref, kbar_sc):
    c = pl.program_id(2)
    own = c // (MOBA_BLOCK // MOBA_QUERIES)
    tq = q_ref.shape[0]
    heads = q_ref.shape[1] // HEAD_DIM
    nb = kt_ref.shape[1] // MOBA_BLOCK

    nbp = -(-nb // 8) * 8

    @pl.when(c == 0)
    def _block_means():
        kbar_sc[...] = _columns([jnp.mean(ktf_ref[:, n * MOBA_BLOCK:(n + 1) * MOBA_BLOCK], axis=1, keepdims=True)
                                 for n in range(nb)], kbar_sc.shape[0]).T

    row = lax.broadcasted_iota(jnp.int32, (tq, MOBA_BLOCK), 0)
    col = lax.broadcasted_iota(jnp.int32, (tq, MOBA_BLOCK), 1)
    causal = own * MOBA_BLOCK + col <= c * tq + row
    n_iota = lax.broadcasted_iota(jnp.int32, (tq, nbp), 1)
    hs = [slice(h * HEAD_DIM, (h + 1) * HEAD_DIM) for h in range(heads)]
    q16, bias = [], []
    for h in range(heads):
        q = q_ref[:, hs[h]]
        gates_t = lax.dot_general(kbar_sc[:nbp, hs[h]], q, _NT, precision=HIGHEST, preferred_element_type=F32)
        bias.append(jnp.where(_top_block_select_t(gates_t, own).T > 0.5, 0.0, NEG_INF))
        q16.append((q * ATT_SCALE).astype(BF16))

    def scores(j, h):
        keys = pl.ds(pl.multiple_of(j * MOBA_BLOCK, MOBA_BLOCK), MOBA_BLOCK)
        return jnp.dot(q16[h], kt_ref[hs[h], keys], preferred_element_type=F32), vt_ref[hs[h], keys]

    state = []
    for h in range(heads):
        s, vb = scores(own, h)
        s = jnp.where(causal, s, NEG_INF)
        m = jnp.max(s, axis=1, keepdims=True)
        p = jnp.exp(s - m)
        state.append((m, jnp.sum(p, axis=1, keepdims=True),
                      lax.dot_general(p.astype(BF16), vb, _NT, preferred_element_type=F32)))

    def body(j, state):
        sv = [scores(j, h) for h in range(heads)]
        s = [sv[h][0] + jnp.sum(jnp.where(n_iota == j, bias[h], 0.0), axis=1, keepdims=True) for h in range(heads)]
        m_new = [jnp.maximum(state[h][0], jnp.max(s[h], axis=1, keepdims=True)) for h in range(heads)]
        p = [jnp.exp(s[h] - m_new[h]) for h in range(heads)]
        new = []
        for h in range(heads):
            m, l, acc = state[h]
            alpha = jnp.exp(m - m_new[h])
            new.append((m_new[h], alpha * l + jnp.sum(p[h], axis=1, keepdims=True),
                        alpha * acc + lax.dot_general(p[h].astype(BF16), sv[h][1], _NT, preferred_element_type=F32)))
        return tuple(new)

    state = lax.fori_loop(0, own, body, tuple(state))
    for h in range(heads):
        o_ref[:, hs[h]] = state[h][2] / state[h][1]


def _moba_prompt(q, kt16, vt16, kt, n_seq, seq_len, gw):
    width = min(MOBA_HEADS * HEAD_DIM, gw)
    n_groups = gw // width
    nc = seq_len // MOBA_QUERIES
    kv_spec = pl.BlockSpec((width, seq_len), lambda b, hg, c: (b * 2 * n_groups + hg, 0))
    q_spec = pl.BlockSpec((MOBA_QUERIES, width), lambda b, hg, c: (b * nc + c, hg))
    return pl.pallas_call(
        _moba_prompt_kernel,
        grid=(n_seq, n_groups, nc),
        in_specs=[q_spec, kv_spec, kv_spec, kv_spec],
        out_specs=q_spec,
        out_shape=jax.ShapeDtypeStruct((n_seq * seq_len, gw), F32),
        scratch_shapes=[pltpu.VMEM((128, width), F32)],
        compiler_params=_cparams(3),
        name="moba_prompt",
    )(q, kt16, vt16, kt)


def _sb_prompt_kernel(q_ref, kt_ref, vt_ref, o_ref):
    c = pl.program_id(2)
    tq = q_ref.shape[0]
    heads = q_ref.shape[1] // HEAD_DIM
    upper = _strict_lower(SB_KEYS)
    row = lax.broadcasted_iota(jnp.int32, (tq, SB_KEYS), 0)
    col = lax.broadcasted_iota(jnp.int32, (tq, SB_KEYS), 1)
    strict = col < row
    hs = [slice(h * HEAD_DIM, (h + 1) * HEAD_DIM) for h in range(heads)]
    q16 = [(q_ref[:, hs[h]] * ATT_SCALE).astype(BF16) for h in range(heads)]

    def tile(j, state, mask):
        keys = pl.ds(pl.multiple_of(j * SB_KEYS, SB_KEYS), SB_KEYS)
        z = [jnp.dot(q16[h], kt_ref[hs[h], keys], preferred_element_type=F32) for h in range(heads)]
        pairs = [_log_sigmoid_pair(zh) for zh in z]
        lk = [pr[0] if mask is None else jnp.where(mask, pr[0], 0.0) for pr in pairs]
        after = _suffix_sums(lk, upper)
        new = []
        for h in range(heads):
            r, acc = state[h]
            a = jnp.exp(pairs[h][1] + after[h] + r)
            if mask is not None:
                a = jnp.where(mask, a, 0.0)
            acc = acc + lax.dot_general(a.astype(BF16), vt_ref[hs[h], keys], _NT, preferred_element_type=F32)
            new.append((r + jnp.sum(lk[h], axis=1, keepdims=True), acc))
        return tuple(new)

    state = tile(c, tuple((jnp.zeros((tq, 1), F32), jnp.zeros((tq, HEAD_DIM), F32)) for _ in range(heads)), strict)

    def live(state):
        r_max = functools.reduce(jnp.maximum, [st[0] for st in state])
        return (jnp.max(r_max) > SB_NEGLIGIBLE).astype(jnp.int32)

    def cond(carry):
        i, alive, _ = carry
        return jnp.logical_and(i < c, alive > 0)

    def body(carry):
        i, _, state = carry
        state = tile(c - 1 - i, state, None)
        return i + 1, live(state), state

    _, _, state = lax.while_loop(cond, body, (jnp.int32(0), live(state), state))
    for h in range(heads):
        o_ref[:, hs[h]] = state[h][1]


def _sb_prompt(q, kt16, vt16, n_seq, seq_len, gw):
    width = min(SB_HEADS * HEAD_DIM, gw)
    n_groups = gw // width
    nc = seq_len // SB_KEYS
    kv_spec = pl.BlockSpec((width, seq_len), lambda b, hg, c: (b * 2 * n_groups + n_groups + hg, 0))
    return pl.pallas_call(
        _sb_prompt_kernel,
        grid=(n_seq, n_groups, nc),
        in_specs=[pl.BlockSpec((SB_KEYS, width), lambda b, hg, c: (b * nc + c, n_groups + hg)), kv_spec, kv_spec],
        out_specs=pl.BlockSpec((SB_KEYS, width), lambda b, hg, c: (b * nc + c, hg)),
        out_shape=jax.ShapeDtypeStruct((n_seq * seq_len, gw), F32),
        compiler_params=_cparams(3),
        name="sb_prompt",
    )(q, kt16, vt16)


def _decode_kernel(pt_ref, q_ref, knew_ref, vnew_ref, *refs, n_pages, nq):
    del pt_ref
    kv_refs = refs[:2 * DECODE_PAGES]
    oa_ref, ob_ref, qbd_sc, g_sc, m_sc, l_sc, o_sc, r_sc, accb_sc = refs[2 * DECODE_PAGES:]
    p = pl.program_id(1)
    n_steps = n_pages // DECODE_PAGES
    rows = qbd_sc.shape[0] // 2
    heads, hd, page = kv_refs[0].shape[1:]
    gh = heads // 2
    gw = gh * hd
    page_refs = [(kv_refs[2 * j + t // 2], t % 2) for j in range(DECODE_PAGES) for t in range(4)]
    pn = knew_ref.shape[2]
    pages_per_block = MOBA_BLOCK // page
    blocks_per_step = DECODE_PAGES // pages_per_block
    n_blocks = n_pages // pages_per_block
    lane_w = m_sc.shape[2]

    @pl.when(p == 0)
    def _block_diagonal_queries():
        shape = qbd_sc.shape
        rr = lax.broadcasted_iota(jnp.int32, shape, 0)
        cc = lax.broadcasted_iota(jnp.int32, shape, 1)
        sr = lax.broadcasted_iota(jnp.int32, (shape[0], q_ref.shape[1]), 0)
        si = lax.broadcasted_iota(jnp.int32, (shape[0], q_ref.shape[1]), 1)
        spread = jnp.dot(jnp.where(_imod(sr, nq) == si, 1.0, 0.0), q_ref[0], precision=HIGHEST,
                         preferred_element_type=F32)
        qbd_sc[...] = (jnp.where(_idiv(rr, nq) == _idiv(cc, hd), spread, 0.0) * ATT_SCALE).astype(BF16)

    qa16 = qbd_sc[:rows, :gw]
    qb16 = qbd_sc[rows:, gw:]
    new_query = _imod(lax.broadcasted_iota(jnp.int32, (rows, pn), 0), nq)
    new_key = lax.broadcasted_iota(jnp.int32, (rows, pn), 1)

    def mat(ref_group):
        ref, g = ref_group
        return ref[0, g * gh:(g + 1) * gh].reshape(gw, page).astype(BF16)

    @pl.when(p == 0)
    def _init():
        z = lax.dot_general(qb16, knew_ref[0, 1].astype(BF16), _NT, preferred_element_type=F32)
        mask = new_key < new_query
        lk, ls = _log_sigmoid_pair(z)
        lk = jnp.where(mask, lk, 0.0)
        a = jnp.where(mask, jnp.exp(ls + _suffix_sums([lk], _strict_lower(pn))[0]), 0.0)
        accb_sc[...] = jnp.dot(a.astype(BF16), vnew_ref[0, 1].astype(BF16), preferred_element_type=F32)
        r_sc[...] = jnp.broadcast_to(jnp.sum(lk, axis=1, keepdims=True), r_sc.shape)

    all_pages = range(DECODE_PAGES)
    blocks = [range(b * pages_per_block, (b + 1) * pages_per_block) for b in range(blocks_per_step)]
    block_of = lambda vals, op: [functools.reduce(op, [vals[j] for j in pages]) for pages in blocks]
    s = [jnp.dot(qa16, mat(page_refs[4 * j]), preferred_element_type=F32) for j in all_pages]
    z = [jnp.dot(qb16, mat(page_refs[4 * j + 1]), preferred_element_type=F32) for j in all_pages]
    m_blk = block_of([jnp.max(sj, axis=1, keepdims=True) for sj in s], jnp.maximum)
    g_blk = block_of([jnp.sum(sj, axis=1, keepdims=True) for sj in s], jnp.add)
    pr = [jnp.exp(s[j] - m_blk[j // pages_per_block]) for j in all_pages]
    l_blk = block_of([jnp.sum(pj, axis=1, keepdims=True) for pj in pr], jnp.add)
    pv = [lax.dot_general(pr[j].astype(BF16), mat(page_refs[4 * j + 2]), _NT, preferred_element_type=F32)
          for j in all_pages]
    o_blk = block_of(pv, jnp.add)
    for b in range(blocks_per_step):
        n = n_blocks - 1 - (p * blocks_per_step + b)
        m_sc[n] = jnp.broadcast_to(m_blk[b], (rows, lane_w))
        l_sc[n] = jnp.broadcast_to(l_blk[b], (rows, lane_w))
        g_sc[n] = jnp.broadcast_to(g_blk[b], (rows, lane_w))
        o_sc[n] = o_blk[b]

    pairs = [_log_sigmoid_pair(zj) for zj in z]
    after = _suffix_sums([pr_[0] for pr_ in pairs], _strict_lower(page))
    lk_sum = [jnp.sum(pr_[0], axis=1, keepdims=True) for pr_ in pairs]
    r = [r_sc[:, :1]]
    for j in all_pages:
        r.append(r[j] + lk_sum[j])
    a = [jnp.exp(pairs[j][1] + after[j] + r[j]).astype(BF16) for j in all_pages]
    accb_sc[...] = functools.reduce(jnp.add, [accb_sc[...]] + [
        lax.dot_general(a[j], mat(page_refs[4 * j + 3]), _NT, preferred_element_type=F32) for j in all_pages])
    r_sc[...] = jnp.broadcast_to(r[DECODE_PAGES], r_sc.shape)

    @pl.when(p == n_steps - 1)
    def _finish():
        gates = _columns([g_sc[b][:, :1] for b in range(n_blocks)], n_blocks)
        bias = _top_block_bias(gates, n_blocks)
        sn = lax.dot_general(qa16, knew_ref[0, 0].astype(BF16), _NT, preferred_element_type=F32)
        sn = jnp.where(new_key <= new_query, sn, NEG_INF)
        m_own = jnp.max(sn, axis=1, keepdims=True)
        pn_ = jnp.exp(sn - m_own)
        l_own = jnp.sum(pn_, axis=1, keepdims=True)
        o_own = jnp.dot(pn_.astype(BF16), vnew_ref[0, 0].astype(BF16), preferred_element_type=F32)
        m_all = m_own
        for b in range(n_blocks):
            m_all = jnp.maximum(m_all, m_sc[b][:, :1] + bias[:, b:b + 1])
        w_own = jnp.exp(m_own - m_all)
        num = w_own * o_own
        den = w_own * l_own
        for b in range(n_blocks):
            w = jnp.exp(m_sc[b][:, :1] + bias[:, b:b + 1] - m_all)
            num = num + w * o_sc[b]
            den = den + w * l_sc[b][:, :1]
        outa = num / den
        rr = lax.broadcasted_iota(jnp.int32, (rows, gw), 0)
        cc = lax.broadcasted_iota(jnp.int32, (rows, gw), 1)
        diag = _idiv(rr, nq) == _idiv(cc, hd)
        si = lax.broadcasted_iota(jnp.int32, (oa_ref.shape[1], rows), 0)
        sr = lax.broadcasted_iota(jnp.int32, (oa_ref.shape[1], rows), 1)
        pick = jnp.where(_imod(sr, nq) == si, 1.0, 0.0)
        oa_ref[0] = jnp.dot(pick, jnp.where(diag, outa, 0.0), precision=HIGHEST, preferred_element_type=F32)
        ob_ref[0] = jnp.dot(pick, jnp.where(diag, accb_sc[...], 0.0), precision=HIGHEST,
                            preferred_element_type=F32)


def _decode_attention(page_table, q, knew, vnew, cache_kt, cache_vt, nq):
    n_dec, n_pages = page_table.shape
    _, heads, hd, page = cache_kt.shape
    gh = heads // 2
    gw = gh * hd
    rows = gh * nq
    n_blocks = n_pages * page // MOBA_BLOCK
    per_seq = lambda a: pl.BlockSpec((1,) + a.shape[1:], lambda b, p, pt: (b,) + (0,) * (a.ndim - 1))
    page_spec = lambda j: pl.BlockSpec(
        (1, heads, hd, page), lambda b, p, pt: (pt[b, n_pages - 1 - (p * DECODE_PAGES + j)], 0, 0, 0))
    out_spec = pl.BlockSpec((1, OUT_ROWS, gw), lambda b, p, pt: (b, 0, 0))
    assert n_pages % DECODE_PAGES == 0 and DECODE_PAGES % (MOBA_BLOCK // page) == 0 and nq <= OUT_ROWS
    grid_spec = pltpu.PrefetchScalarGridSpec(
        num_scalar_prefetch=1,
        grid=(n_dec, n_pages // DECODE_PAGES),
        in_specs=[per_seq(q), per_seq(knew), per_seq(vnew)]
        + [page_spec(j) for j in range(DECODE_PAGES) for _ in range(2)],
        out_specs=[out_spec, out_spec],
        scratch_shapes=[pltpu.VMEM((2 * rows, 2 * gw), BF16),
                        pltpu.VMEM((n_blocks, rows, 128), F32),
                        pltpu.VMEM((n_blocks, rows, 128), F32),
                        pltpu.VMEM((n_blocks, rows, 128), F32),
                        pltpu.VMEM((n_blocks, rows, gw), F32),
                        pltpu.VMEM((rows, 128), F32),
                        pltpu.VMEM((rows, gw), F32)],
    )
    return pl.pallas_call(
        functools.partial(_decode_kernel, n_pages=n_pages, nq=nq),
        grid_spec=grid_spec,
        out_shape=[jax.ShapeDtypeStruct((n_dec, OUT_ROWS, gw), F32)] * 2,
        compiler_params=_cparams(2),
        name="decode_attention",
    )(page_table, q, knew, vnew, *([cache_kt, cache_vt] * DECODE_PAGES))


def _merge_kernel(oa_ref, ob_ref, x_ref, ga_ref, gb_ref, w_ref, g_ref, b_ref, y_ref, *, alpha):
    gw = oa_ref.shape[1]
    ya = _rms_norm(oa_ref[...], ga_ref[...]).astype(BF16)
    yb = _rms_norm(ob_ref[...], gb_ref[...]).astype(BF16)
    mix = (jnp.dot(ya, w_ref[:gw, :], preferred_element_type=F32)
           + jnp.dot(yb, w_ref[gw:, :], preferred_element_type=F32))
    y_ref[...] = _layer_norm(alpha * x_ref[...] + mix, g_ref[...], b_ref[...])


def _merge(oa, ob, x, g_a, g_b, w16, ln_g, ln_b, alpha):
    t, d = x.shape
    gw = oa.shape[1]
    rows = _row_tile(t)
    row = lambda width: pl.BlockSpec((rows, width), lambda i: (i, 0))
    full = lambda a: pl.BlockSpec(a.shape, lambda i: (0, 0))
    return pl.pallas_call(
        functools.partial(_merge_kernel, alpha=alpha),
        grid=(t // rows,),
        in_specs=[row(gw), row(gw), row(d), full(g_a), full(g_b), full(w16), full(ln_g), full(ln_b)],
        out_specs=row(d),
        out_shape=jax.ShapeDtypeStruct((t, d), F32),
        compiler_params=_cparams(1),
        name="merge_out_proj",
    )(oa, ob, x, g_a, g_b, w16, ln_g, ln_b)


def _mem_kv_kernel(m_ref, w_ref, k_ref, v_ref, k16_ref, v16_ref):
    width = k_ref.shape[1]
    m16 = m_ref[...].astype(BF16)
    k = jnp.dot(m16, w_ref[:, :width], preferred_element_type=F32)
    v = jnp.dot(m16, w_ref[:, width:], preferred_element_type=F32)
    k_ref[...] = k
    v_ref[...] = v
    k16_ref[...] = k.astype(BF16)
    v16_ref[...] = v.astype(BF16)


def _mem_kv(mem, w16):
    t, d = mem.shape
    width = w16.shape[1] // 2
    rows = _row_tile(t)
    row = lambda w: pl.BlockSpec((rows, w), lambda i: (i, 0))
    return pl.pallas_call(
        _mem_kv_kernel,
        grid=(t // rows,),
        in_specs=[row(d), pl.BlockSpec(w16.shape, lambda i: (0, 0))],
        out_specs=[row(width)] * 4,
        out_shape=[jax.ShapeDtypeStruct((t, width), F32)] * 2 + [jax.ShapeDtypeStruct((t, width), BF16)] * 2,
        compiler_params=_cparams(1),
        name="mem_kv",
    )(mem, w16)


def _mem_heads(q16, head_kv, row_mask=None):
    hd = q16.shape[1] // MEM_HEADS
    outs = []
    for h in range(MEM_HEADS):
        k16, v16 = head_kv(h)
        s = lax.dot_general(q16[:, h * hd:(h + 1) * hd], k16, _NT, preferred_element_type=F32)
        m = jnp.max(s, axis=1, keepdims=True)
        p = jnp.exp(s - m)
        l = jnp.sum(p, axis=1, keepdims=True)
        o = jnp.dot(p.astype(BF16), v16, preferred_element_type=F32) / l
        outs.append(o if row_mask is None else jnp.where(row_mask, o, 0.0))
    return jnp.concatenate(outs, axis=1)


def _mem_attend_kernel(x_ref, wq_ref, mk_ref, mv_ref, wo_ref, g_ref, b_ref, y_ref, *, alpha, n_tiles):
    @pl.when(pl.program_id(0) < n_tiles)
    def _tile():
        x = x_ref[...]
        hd = x.shape[1] // MEM_HEADS
        q16 = (jnp.dot(x.astype(BF16), wq_ref[...], preferred_element_type=F32) * hd ** -0.5).astype(BF16)
        o = _mem_heads(q16, lambda h: (mk_ref[:, h * hd:(h + 1) * hd], mv_ref[:, h * hd:(h + 1) * hd]))
        y = jnp.dot(o.astype(BF16), wo_ref[...], preferred_element_type=F32)
        y_ref[...] = _layer_norm(alpha * x + y, g_ref[...], b_ref[...])

    @pl.when(pl.program_id(0) >= n_tiles)
    def _tail():
        y_ref[...] = jnp.zeros(y_ref.shape, F32)


def _mem_attend_prompt(x, wq16, mk16, mv16, wo16, ln_g, ln_b, n_seq, alpha, out_rows):
    t, d = x.shape
    seq_len = t // n_seq
    n_mem = mk16.shape[0] // n_seq
    rows = _row_tile(seq_len)
    tiles = seq_len // rows
    n_tiles = n_seq * tiles
    assert (out_rows - t) % rows == 0
    last = lambda i: jnp.minimum(i, n_tiles - 1)
    full = lambda a: pl.BlockSpec(a.shape, lambda i: (0, 0))
    mem = pl.BlockSpec((n_mem, d), lambda i: (last(i) // tiles, 0))
    return pl.pallas_call(
        functools.partial(_mem_attend_kernel, alpha=alpha, n_tiles=n_tiles),
        grid=(out_rows // rows,),
        in_specs=[pl.BlockSpec((rows, d), lambda i: (last(i), 0)), full(wq16), mem, mem, full(wo16),
                  full(ln_g), full(ln_b)],
        out_specs=pl.BlockSpec((rows, d), lambda i: (i, 0)),
        out_shape=jax.ShapeDtypeStruct((out_rows, d), F32),
        compiler_params=_cparams(1),
        name="mem_attend_prompt",
    )(x, wq16, mk16, mv16, wo16, ln_g, ln_b)


def _mem_attend_sample_kernel(x_ref, wq_ref, mk_ref, mv_ref, wo_ref, g_ref, b_ref, all_ref, y_ref, *, alpha, nq):
    del all_ref
    x = x_ref[...]
    hd = x.shape[1] // MEM_HEADS
    q16 = (jnp.dot(x.astype(BF16), wq_ref[...], preferred_element_type=F32) * hd ** -0.5).astype(BF16)
    seq_of_row = _idiv(lax.broadcasted_iota(jnp.int32, (x.shape[0], 1), 0), nq)
    o = jnp.zeros(x.shape, F32)
    for g in range(mk_ref.shape[0]):
        head_kv = lambda h, g=g: (mk_ref[g, :, h * hd:(h + 1) * hd].astype(BF16),
                                  mv_ref[g, :, h * hd:(h + 1) * hd].astype(BF16))
        o = o + _mem_heads(q16, head_kv, seq_of_row == g)
    y = jnp.dot(o.astype(BF16), wo_ref[...], preferred_element_type=F32)
    y_ref[...] = _layer_norm(alpha * x + y, g_ref[...], b_ref[...])


def _mem_attend_sample(x, wq16, cache_mk, cache_mv, wo16, ln_g, ln_b, nq, alpha, all_rows):
    t, d = x.shape
    n_dec, n_mem, _ = cache_mk.shape
    rows = MEM_GROUP * nq
    base = (all_rows.shape[0] - t) // rows
    assert (all_rows.shape[0] - t) % rows == 0
    row = pl.BlockSpec((rows, d), lambda i: (i, 0))
    full = lambda a: pl.BlockSpec(a.shape, lambda i: (0, 0))
    mem = pl.BlockSpec((MEM_GROUP, n_mem, d), lambda i: (i, 0, 0))
    return pl.pallas_call(
        functools.partial(_mem_attend_sample_kernel, alpha=alpha, nq=nq),
        grid=(n_dec // MEM_GROUP,),
        in_specs=[row, full(wq16), mem, mem, full(wo16), full(ln_g), full(ln_b), pl.BlockSpec(memory_space=pl.ANY)],
        out_specs=pl.BlockSpec((rows, d), lambda i: (base + i, 0)),
        out_shape=jax.ShapeDtypeStruct(all_rows.shape, F32),
        input_output_aliases={7: 0},
        compiler_params=_cparams(1),
        name="mem_attend_sample",
    )(x, wq16, cache_mk, cache_mv, wo16, ln_g, ln_b, all_rows)


def _router_kernel(x_ref, w_ref, b_ref, e_ref, g_ref, rank_ref, cnt_ref, run_sc):
    rows = x_ref.shape[0]
    logits = jnp.dot(x_ref[...], w_ref[...], precision=HIGHEST, preferred_element_type=F32) + b_ref[...]
    n_exp = logits.shape[1]
    e_iota = lax.broadcasted_iota(jnp.int32, logits.shape, 1)
    k_iota = lax.broadcasted_iota(jnp.int32, e_ref.shape, 1)
    top_e = jnp.zeros(e_ref.shape, jnp.int32)
    top_v = jnp.zeros(e_ref.shape, F32)
    onehot = []
    for k in range(TOP_K):
        mx = jnp.max(logits, axis=1, keepdims=True)
        idx = jnp.min(jnp.where(logits == mx, e_iota, n_exp), axis=1, keepdims=True)
        top_e = jnp.where(k_iota == k, idx, top_e)
        top_v = jnp.where(k_iota == k, mx, top_v)
        onehot.append(jnp.where(e_iota == idx, 1.0, 0.0))
        logits = jnp.where(e_iota == idx, NEG_INF, logits)
    w = jnp.exp(top_v - top_v[:, :1])
    e_ref[...] = top_e
    g_ref[...] = w / jnp.sum(w, axis=1, keepdims=True)

    @pl.when(pl.program_id(0) == 0)
    def _zero():
        run_sc[...] = jnp.zeros(run_sc.shape, F32)

    chosen = functools.reduce(jnp.add, onehot)
    before = jnp.dot(_strict_lower(rows), chosen.astype(BF16), preferred_element_type=F32) + run_sc[...]
    rank = jnp.zeros(e_ref.shape, F32)
    for k in range(TOP_K):
        rank = jnp.where(k_iota == k, jnp.sum(onehot[k] * before, axis=1, keepdims=True), rank)
    rank_ref[...] = rank.astype(jnp.int32)
    run_sc[...] = run_sc[...] + jnp.sum(chosen, axis=0, keepdims=True)
    cnt_ref[...] = run_sc[...]


def _router(x, w_router, b_router):
    t, d = x.shape
    n_exp = w_router.shape[1]
    rows = _row_tile(t)
    out = pl.BlockSpec((rows, TOP_K), lambda i: (i, 0))
    return pl.pallas_call(
        _router_kernel,
        grid=(t // rows,),
        in_specs=[pl.BlockSpec((rows, d), lambda i: (i, 0)),
                  pl.BlockSpec(w_router.shape, lambda i: (0, 0)),
                  pl.BlockSpec(b_router.shape, lambda i: (0, 0))],
        out_specs=[out, out, out, pl.BlockSpec((1, n_exp), lambda i: (0, 0))],
        out_shape=[jax.ShapeDtypeStruct((t, TOP_K), jnp.int32), jax.ShapeDtypeStruct((t, TOP_K), F32),
                   jax.ShapeDtypeStruct((t, TOP_K), jnp.int32), jax.ShapeDtypeStruct((1, n_exp), F32)],
        scratch_shapes=[pltpu.VMEM((1, n_exp), F32)],
        compiler_params=_cparams(1),
        name="router_top4",
    )(x, w_router, b_router)


def _row_copy(src_hbm, src_row, dst, r, sem):
    return pltpu.make_async_copy(src_hbm.at[pl.ds(src_row, 1)], dst.at[pl.ds(r, 1)], sem)


def _row_gather(src_hbm, idx_ref, dst, sem, n_rows):
    for r in range(n_rows):
        _row_copy(src_hbm, idx_ref[0, 0, r], dst, r, sem).start()


def _row_gather_wait(src_hbm, dst, sem, n_rows):
    def wait(r, carry):
        _row_copy(src_hbm, 0, dst, r, sem).wait()
        return carry
    lax.fori_loop(0, n_rows, wait, 0, unroll=DMA_UNROLL)


def _dispatch_kernel(idx_ref, x_ref, rows_in, rows_out, stage, sems):
    del rows_in
    s = pl.program_id(0)
    n_steps = pl.num_programs(0)
    rows = x_ref.shape[0]
    slot = s % 2

    def copy(buf, t, dst_row):
        return pltpu.make_async_copy(stage.at[buf, pl.ds(t, 1)], rows_out.at[pl.ds(dst_row, 1)], sems.at[buf])

    def drain(buf):
        def wait(r, carry):
            copy(buf, 0, 0).wait()
            return carry
        lax.fori_loop(0, TOP_K * rows, wait, 0, unroll=DMA_UNROLL)

    @pl.when(s >= 2)
    def _reuse():
        drain(slot)

    stage[slot] = x_ref[...]
    for k in range(TOP_K):
        for t in range(rows):
            copy(slot, t, idx_ref[0, 0, k * rows + t]).start()

    @pl.when(s == n_steps - 1)
    def _finish():
        @pl.when(s >= 1)
        def _other():
            drain(1 - slot)
        drain(slot)


def _dispatch(x, dest, n_rows):
    t, d = x.shape
    n_steps, _, n_copy = dest.shape
    rows = n_copy // TOP_K
    return pl.pallas_call(
        _dispatch_kernel,
        grid=(n_steps,),
        in_specs=[pl.BlockSpec((1, 1, n_copy), lambda s: (s, 0, 0), memory_space=pltpu.SMEM),
                  pl.BlockSpec((rows, d), lambda s: (s, 0)),
                  pl.BlockSpec(memory_space=pl.ANY)],
        out_specs=pl.BlockSpec(memory_space=pl.ANY),
        out_shape=jax.ShapeDtypeStruct((n_rows, d), F32),
        input_output_aliases={2: 0},
        scratch_shapes=[pltpu.VMEM((2, rows, d), F32), pltpu.SemaphoreType.DMA((2,))],
        compiler_params=_cparams(1),
        name="moe_dispatch",
    )(dest, x, jnp.zeros((n_rows, d), F32))


def _expert_ffn_kernel(be_ref, nu_ref, x_ref, wgu_ref, bgu_ref, wd_ref, bd_ref, y_ref, wgu16, wd16):
    s = pl.program_id(0)
    active = s < nu_ref[0]
    new_expert = jnp.logical_or(s == 0, be_ref[s] != be_ref[jnp.maximum(s - 1, 0)])

    @pl.when(jnp.logical_and(active, new_expert))
    def _cast_weights():
        wgu16[...] = wgu_ref[0].astype(BF16)
        wd16[...] = wd_ref[0].astype(BF16)

    @pl.when(active)
    def _compute():
        f = wd_ref.shape[1]
        hgu = jnp.dot(x_ref[...].astype(BF16), wgu16[...], preferred_element_type=F32) + bgu_ref[0]
        gate = jnp.minimum(hgu[:, :f], SWIGLU_LIMIT)
        up = jnp.clip(hgu[:, f:], -SWIGLU_LIMIT, SWIGLU_LIMIT)
        act = (up + 1.0) * gate * jax.nn.sigmoid(SWIGLU_ALPHA * gate)
        y_ref[...] = jnp.dot(act.astype(BF16), wd16[...], preferred_element_type=F32) + bd_ref[0]

    @pl.when(jnp.logical_not(active))
    def _unused():
        y_ref[...] = jnp.zeros(y_ref.shape, F32)


def _expert_ffn(x_rows, block_e, n_used, w_gu, b_gu, w_down, b_down, rows):
    n_exp, d, f2 = w_gu.shape
    f = f2 // 2
    n_blk = x_rows.shape[0] // rows
    by_expert = lambda shape: pl.BlockSpec(shape, lambda s, be, nu: (be[s], 0, 0))
    row_spec = pl.BlockSpec((rows, d), lambda s, be, nu: (s, 0))
    grid_spec = pltpu.PrefetchScalarGridSpec(
        num_scalar_prefetch=2,
        grid=(n_blk,),
        in_specs=[row_spec, by_expert((1, d, f2)), by_expert((1, 1, f2)), by_expert((1, f, d)),
                  by_expert((1, 1, d))],
        out_specs=row_spec,
        scratch_shapes=[pltpu.VMEM((d, f2), BF16), pltpu.VMEM((f, d), BF16)],
    )
    return pl.pallas_call(
        _expert_ffn_kernel,
        grid_spec=grid_spec,
        out_shape=jax.ShapeDtypeStruct((n_blk * rows, d), F32),
        compiler_params=_cparams(1),
        name="expert_ffn",
    )(block_e, n_used, x_rows, w_gu, b_gu.reshape(n_exp, 1, f2), w_down, b_down.reshape(n_exp, 1, d))


def _combine_kernel(idx_ref, y_hbm, x_ref, gate_ref, g_ref, b_ref, oa_ref, ob_ref, ybuf, sems, *, alpha, a_tiles):
    s = pl.program_id(0)
    n_steps = pl.num_programs(0) - 1
    rows = x_ref.shape[0]
    n_copy = TOP_K * rows

    @pl.when(s < n_steps)
    def _fetch():
        _row_gather(y_hbm, idx_ref, ybuf.at[s % 2], sems.at[s % 2], n_copy)

    @pl.when(s >= 1)
    def _compute():
        slot = (s - 1) % 2
        _row_gather_wait(y_hbm, ybuf.at[slot], sems.at[slot], n_copy)
        gates = gate_ref[...]
        ffn = jnp.zeros(x_ref.shape, F32)
        for k in range(TOP_K):
            ffn = ffn + gates[:, k:k + 1] * ybuf[slot, k * rows:(k + 1) * rows, :]
        out = _layer_norm(alpha * x_ref[...] + ffn, g_ref[...], b_ref[...])

        @pl.when(s - 1 < a_tiles)
        def _first_group():
            oa_ref[...] = out

        @pl.when(s - 1 >= a_tiles)
        def _second_group():
            ob_ref[...] = out


def _combine(y_rows, dest, x, gates, ln_g, ln_b, alpha, n_first):
    t, d = x.shape
    n_steps, _, n_copy = dest.shape
    rows = n_copy // TOP_K
    a_tiles = n_first // rows
    assert n_first % rows == 0 and 0 < a_tiles < n_steps
    idx_spec = pl.BlockSpec((1, 1, n_copy), lambda s: (jnp.minimum(s, n_steps - 1), 0, 0),
                            memory_space=pltpu.SMEM)
    row = lambda w: pl.BlockSpec((rows, w), lambda s: (jnp.maximum(s - 1, 0), 0))
    full = lambda a: pl.BlockSpec(a.shape, lambda s: (0, 0))
    return pl.pallas_call(
        functools.partial(_combine_kernel, alpha=alpha, a_tiles=a_tiles),
        grid=(n_steps + 1,),
        in_specs=[idx_spec, pl.BlockSpec(memory_space=pl.ANY), row(d), row(TOP_K), full(ln_g), full(ln_b)],
        out_specs=[pl.BlockSpec((rows, d), lambda s: (jnp.clip(s - 1, 0, a_tiles - 1), 0)),
                   pl.BlockSpec((rows, d), lambda s: (jnp.maximum(s - 1 - a_tiles, 0), 0))],
        out_shape=[jax.ShapeDtypeStruct((n_first, d), F32), jax.ShapeDtypeStruct((t - n_first, d), F32)],
        scratch_shapes=[pltpu.VMEM((2, n_copy, d), F32), pltpu.SemaphoreType.DMA((2,))],
        compiler_params=_cparams(1),
        name="moe_combine",
    )(dest, y_rows, x, gates, ln_g, ln_b)


def _dispatch_plan(top_e, rank, counts, rows):
    n_assign = top_e.size
    n_exp = counts.shape[0]
    flat_e = top_e.reshape(n_assign)
    padded = (counts + rows - 1) // rows * rows
    pad_end = jnp.cumsum(padded)
    pad_start = pad_end - padded
    dest = (pad_start[flat_e] + rank.reshape(n_assign)).astype(jnp.int32)
    n_blk = -(-(n_assign + n_exp * (rows - 1)) // rows)
    block_start = jnp.arange(n_blk, dtype=jnp.int32) * rows
    block_e = jnp.minimum(jnp.sum((pad_end[None, :] <= block_start[:, None]).astype(jnp.int32), axis=1), n_exp - 1)
    n_used = (pad_end[-1] // rows).astype(jnp.int32).reshape(1)
    return dest, block_e, n_used, n_blk


def _moe(tok, w_router, b_router, w_gu, b_gu, w_down, b_down, ln_g, ln_b, alpha, n_first):
    t, d = tok.shape
    n_exp = w_router.shape[1]
    top_e, gates, rank, counts = _router(tok, w_router, b_router.reshape(1, n_exp))
    dest, block_e, n_used, n_blk = _dispatch_plan(top_e, rank, counts[0].astype(jnp.int32), EXPERT_ROWS)
    steps = t // COMBINE_ROWS
    dest_steps = dest.reshape(steps, COMBINE_ROWS, TOP_K).transpose(0, 2, 1).reshape(steps, 1, TOP_K * COMBINE_ROWS)
    x_rows = _dispatch(tok, dest_steps, n_blk * EXPERT_ROWS)
    y_rows = _expert_ffn(x_rows, block_e, n_used, w_gu, b_gu, w_down, b_down, EXPERT_ROWS)
    return _combine(y_rows, dest_steps, tok, gates, ln_g, ln_b, alpha, n_first)


def _rope_tables(pos, n_heads):
    half = HEAD_DIM // 2
    inv_freq = ROPE_THETA ** (-jnp.arange(half, dtype=F32) / half)
    ang = pos.astype(F32)[:, None] * inv_freq[None, :]
    cos = jnp.cos(ang)
    sin = jnp.sin(ang)
    return (jnp.tile(jnp.concatenate([cos, cos], axis=1), (1, n_heads)),
            jnp.tile(jnp.concatenate([-sin, sin], axis=1), (1, n_heads)), cos.T, sin.T)


def kernel(x_prompt, x_sample, mem_prompt, cache_k, cache_v, cache_mem_k, cache_mem_v, page_table,
           w_in, g_moba, g_sb, w_out, ln1_g, ln1_b, w_mq, w_mkv, w_mo, ln2_g, ln2_b,
           w_router, b_router, w_gu, b_gu, w_down, b_down, ln3_g, ln3_b):
    n_seq, seq_len, d = x_prompt.shape
    n_dec, nq, _ = x_sample.shape
    depth = w_in.shape[0]
    gw = w_in.shape[2] // 6
    g_heads = gw // HEAD_DIM
    heads = 2 * g_heads
    n_pages = page_table.shape[1]
    page = cache_k.shape[2]
    past_len = n_pages * page
    n_mem = mem_prompt.shape[1]
    alpha = (2 * depth) ** 0.25
    n_tok_p = n_seq * seq_len
    n_tok_s = n_dec * nq
    assert seq_len % MOBA_BLOCK == 0 and past_len % MOBA_BLOCK == 0 and MOBA_BLOCK % page == 0
    assert nq <= NEW_PAD and n_dec % MEM_GROUP == 0
    assert (n_tok_p + n_tok_s) % COMBINE_ROWS == 0 and n_tok_s % PROJ_ROWS == 0 and seq_len % PROJ_ROWS == 0

    tables_p = _rope_tables(jnp.arange(seq_len), g_heads)
    tables_s = _rope_tables(jnp.tile(past_len + jnp.arange(nq), n_dec), g_heads)
    xp = x_prompt.reshape(n_tok_p, d)
    xs = x_sample.reshape(n_tok_s, d)
    row2 = lambda a: a.reshape(1, -1)

    def to_positions(at, n, length):
        return at.reshape(n, heads, HEAD_DIM, length).transpose(0, 3, 1, 2)

    def new_rows(at):
        a = at.reshape(2, gw, n_dec, nq).transpose(2, 0, 3, 1)
        return jnp.pad(a, ((0, 0), (0, 0), (0, NEW_PAD - nq), (0, 0)))

    outs = [[] for _ in range(6)]
    for l in range(depth):
        w = w_in[l].astype(BF16)
        col = lambda g: w[:, g * gw:(g + 1) * gw]
        wq16 = jnp.concatenate([col(0), col(3)], axis=1)
        wkvt16 = jnp.concatenate([col(1), col(4), col(2), col(5)], axis=1).T
        w_out16 = w_out[l].astype(BF16)
        q_p, kt_p, vt_p, kt16_p, vt16_p = _qkv_project(xp, wq16, wkvt16, *tables_p, PROJ_ROWS, seq_len)
        oa_p = _moba_prompt(q_p, kt16_p, vt16_p, kt_p, n_seq, seq_len, gw)
        ob_p = _sb_prompt(q_p, kt16_p, vt16_p, n_seq, seq_len, gw)
        outs[0].append(to_positions(kt_p, n_seq, seq_len))
        outs[1].append(to_positions(vt_p, n_seq, seq_len))
        q_s, kt_s, vt_s, _, _ = _qkv_project(xs, wq16, wkvt16, *tables_s, PROJ_ROWS, n_tok_s)
        oa_s, ob_s = _decode_attention(
            page_table, jnp.pad(q_s.reshape(n_dec, nq, 2 * gw), ((0, 0), (0, OUT_ROWS - nq), (0, 0))),
            new_rows(kt_s), new_rows(vt_s),
            cache_k[l].transpose(0, 2, 3, 1), cache_v[l].transpose(0, 2, 3, 1), nq)
        outs[2].append(to_positions(kt_s, 1, n_tok_s).reshape(n_dec, nq, heads, HEAD_DIM))
        outs[3].append(to_positions(vt_s, 1, n_tok_s).reshape(n_dec, nq, heads, HEAD_DIM))
        merge = functools.partial(_merge, g_a=row2(g_moba[l]), g_b=row2(g_sb[l]), w16=w_out16,
                                  ln_g=row2(ln1_g[l]), ln_b=row2(ln1_b[l]), alpha=alpha)
        xp = merge(oa_p, ob_p, xp)
        xs = merge(oa_s[:, :nq].reshape(n_tok_s, gw), ob_s[:, :nq].reshape(n_tok_s, gw), xs)
        mk, mv, mk16, mv16 = _mem_kv(mem_prompt.reshape(n_seq * n_mem, d), w_mkv[l].astype(BF16))
        outs[4].append(mk.reshape(n_seq, n_mem, MEM_HEADS, d // MEM_HEADS))
        outs[5].append(mv.reshape(n_seq, n_mem, MEM_HEADS, d // MEM_HEADS))
        wmq16 = w_mq[l].astype(BF16)
        wmo16 = w_mo[l].astype(BF16)
        tok = _mem_attend_prompt(xp, wmq16, mk16, mv16, wmo16, row2(ln2_g[l]), row2(ln2_b[l]), n_seq, alpha,
                                 n_tok_p + n_tok_s)
        tok = _mem_attend_sample(xs, wmq16, cache_mem_k[l].reshape(n_dec, n_mem, d),
                                 cache_mem_v[l].reshape(n_dec, n_mem, d), wmo16,
                                 row2(ln2_g[l]), row2(ln2_b[l]), nq, alpha, tok)
        xp, xs = _moe(tok, w_router[l], b_router[l], w_gu[l], b_gu[l], w_down[l],
                      b_down[l], row2(ln3_g[l]), row2(ln3_b[l]), alpha, n_tok_p)
    return (xp.reshape(n_seq, seq_len, d), xs.reshape(n_dec, nq, d)) + tuple(jnp.stack(o) for o in outs)
```

```python
import functools

import jax
import jax.numpy as jnp
from jax import lax
from jax.experimental import pallas as pl
from jax.experimental.pallas import tpu as pltpu

F32 = jnp.float32
BF16 = jnp.bfloat16
HIGHEST = lax.Precision.HIGHEST

HEAD_DIM = 64
MOBA_BLOCK = 256
MOBA_TOP_K = 3
ROPE_THETA = 10000.0
MEM_HEADS = 4
TOP_K = 4
SWIGLU_LIMIT = 7.0
SWIGLU_ALPHA = 1.702
LN_EPS = 1e-5
RMS_EPS = 1e-6
ATT_SCALE = HEAD_DIM ** -0.5
NEG_INF = float("-inf")
SB_NEGLIGIBLE = -105.0

PROJ_ROWS = 256
MOBA_QUERIES = 256
SB_KEYS = 128
MOBA_HEADS = 4
SB_HEADS = 8
DECODE_PAGES = 8
TOKEN_ROWS = 512
EXPERT_ROWS = 256
COMBINE_ROWS = 128
NEW_PAD = 16
OUT_ROWS = 8
MEM_GROUP = 4
DMA_UNROLL = 8
VMEM_LIMIT = 56 * 1024 * 1024

_NT = (((1,), (1,)), ((), ()))


def _cparams(n_axes):
    return pltpu.CompilerParams(dimension_semantics=("arbitrary",) * n_axes,
                                vmem_limit_bytes=VMEM_LIMIT)


def _row_tile(n_rows):
    rows = TOKEN_ROWS
    while n_rows % rows:
        rows //= 2
    assert rows >= 8, n_rows
    return rows


def _idiv(x, n):
    return x >> (n.bit_length() - 1) if n & (n - 1) == 0 else x // n


def _imod(x, n):
    return x & (n - 1) if n & (n - 1) == 0 else x % n


def _layer_norm(x, g, b):
    mu = jnp.mean(x, axis=-1, keepdims=True)
    xc = x - mu
    var = jnp.mean(xc * xc, axis=-1, keepdims=True)
    return xc * lax.rsqrt(var + LN_EPS) * g + b


def _rms_norm(x, g):
    return x * lax.rsqrt(jnp.mean(x * x, axis=-1, keepdims=True) + RMS_EPS) * g


def _log_sigmoid_pair(z):
    t = jnp.log1p(jnp.exp(-jnp.abs(z)))
    return -(jnp.maximum(z, 0.0) + t), jnp.minimum(z, 0.0) - t


def _suffix_sums(xs, upper):
    rows = xs[0].shape[0]
    hi = [x.astype(BF16) for x in xs]
    lo = [(x - xh.astype(F32)).astype(BF16) for x, xh in zip(xs, hi)]
    s = jnp.dot(jnp.concatenate(hi + lo, axis=0), upper, preferred_element_type=F32)
    n = len(xs)
    return [s[i * rows:(i + 1) * rows] + s[(n + i) * rows:(n + i + 1) * rows] for i in range(n)]


def _strict_lower(n):
    r = lax.broadcasted_iota(jnp.int32, (n, n), 0)
    c = lax.broadcasted_iota(jnp.int32, (n, n), 1)
    return jnp.where(r > c, 1.0, 0.0).astype(BF16)


def _top_block_bias(gates, n_valid):
    nb = gates.shape[1]
    n_iota = lax.broadcasted_iota(jnp.int32, gates.shape, 1)
    valid = n_iota < n_valid
    g = jnp.where(valid, gates, NEG_INF)
    cnt = jnp.zeros(gates.shape, jnp.int32)
    for m in range(nb):
        gm = g[:, m:m + 1]
        beats = jnp.where(gm > g, 1, jnp.where(gm == g, jnp.where(n_iota > m, 1, 0), 0))
        cnt = cnt + beats
    sel = jnp.where(valid, jnp.where(cnt < MOBA_TOP_K, 1, 0), 0)
    return jnp.where(sel == 1, 0.0, NEG_INF)


def _top_block_select_t(gates_t, n_valid):
    nb = gates_t.shape[0]
    n_iota = lax.broadcasted_iota(jnp.int32, gates_t.shape, 0)
    valid = n_iota < n_valid
    g = jnp.where(valid, gates_t, NEG_INF)
    cnt = jnp.zeros(gates_t.shape, jnp.int32)
    for m in range(nb):
        gm = g[m:m + 1, :]
        cnt = cnt + jnp.where(gm > g, 1, jnp.where(gm == g, jnp.where(n_iota > m, 1, 0), 0))
    return jnp.where(valid, jnp.where(cnt < MOBA_TOP_K, 1.0, 0.0), 0.0)


def _columns(cols, width):
    lane = lax.broadcasted_iota(jnp.int32, (cols[0].shape[0], width), 1)
    out = jnp.zeros((cols[0].shape[0], width), F32)
    for n, c in enumerate(cols):
        out = jnp.where(lane == n, c, out)
    return out


def _qkv_kernel(x_ref, wq_ref, wkv_ref, cos_ref, sin_ref, cost_ref, sint_ref,
                q_ref, kt_ref, vt_ref, kt16_ref, vt16_ref, *, gw):
    half = HEAD_DIM // 2
    x16 = x_ref[...].astype(BF16)
    cos = cos_ref[...]
    sin = sin_ref[...]
    lane = lax.broadcasted_iota(jnp.int32, cos.shape, 1)
    first_half = (lane & (HEAD_DIM - 1)) < half
    qa = jnp.dot(x16, wq_ref[:, :gw], preferred_element_type=F32)
    rot = jnp.where(first_half, pltpu.roll(qa, gw - half, 1), pltpu.roll(qa, half, 1))
    q_ref[:, :gw] = qa * cos + rot * sin
    q_ref[:, gw:] = jnp.dot(x16, wq_ref[:, gw:], preferred_element_type=F32)

    def proj_t(g):
        return lax.dot_general(wkv_ref[g * gw:(g + 1) * gw, :], x16, _NT, preferred_element_type=F32)

    def put(ref, ref16, r0, val):
        ref[r0:r0 + val.shape[0], :] = val
        ref16[r0:r0 + val.shape[0], :] = val.astype(BF16)

    kat = proj_t(0)
    cost = cost_ref[...]
    sint = sint_ref[...]
    for h in range(gw // HEAD_DIM):
        x1 = kat[h * HEAD_DIM:h * HEAD_DIM + half]
        x2 = kat[h * HEAD_DIM + half:(h + 1) * HEAD_DIM]
        put(kt_ref, kt16_ref, h * HEAD_DIM, x1 * cost - x2 * sint)
        put(kt_ref, kt16_ref, h * HEAD_DIM + half, x2 * cost + x1 * sint)
    put(kt_ref, kt16_ref, gw, proj_t(1))
    put(vt_ref, vt16_ref, 0, proj_t(2))
    put(vt_ref, vt16_ref, gw, proj_t(3))


def _qkv_project(x, wq16, wkvt16, cos, sin, cost, sint, rows, seq_len):
    t, d = x.shape
    gw = wq16.shape[1] // 2
    tiles = seq_len // rows
    full = lambda a: pl.BlockSpec(a.shape, lambda i: (0, 0))
    row_spec = lambda width: pl.BlockSpec((rows, width), lambda i: (i, 0))
    t_spec = pl.BlockSpec((2 * gw, rows), lambda i: (i // tiles, i % tiles))
    t_shape = lambda dt: jax.ShapeDtypeStruct((t // seq_len * 2 * gw, seq_len), dt)
    return pl.pallas_call(
        functools.partial(_qkv_kernel, gw=gw),
        grid=(t // rows,),
        in_specs=[row_spec(d), full(wq16), full(wkvt16),
                  pl.BlockSpec((rows, gw), lambda i: (i % tiles, 0)),
                  pl.BlockSpec((rows, gw), lambda i: (i % tiles, 0)),
                  pl.BlockSpec((HEAD_DIM // 2, rows), lambda i: (0, i % tiles)),
                  pl.BlockSpec((HEAD_DIM // 2, rows), lambda i: (0, i % tiles))],
        out_specs=[row_spec(2 * gw), t_spec, t_spec, t_spec, t_spec],
        out_shape=[jax.ShapeDtypeStruct((t, 2 * gw), F32), t_shape(F32), t_shape(F32), t_shape(BF16), t_shape(BF16)],
        compiler_params=_cparams(1),
        name="qkv_rope",
    )(x, wq16, wkvt16, cos, sin, cost, sint)


def _moba_prompt_kernel(q_ref, kt_ref, vt_ref, ktf_ref, o_ref, kbar_sc):
    c = pl.program_id(2)
    own = c // (MOBA_BLOCK // MOBA_QUERIES)
    tq = q_ref.shape[0]
    heads = q_ref.shape[1] // HEAD_DIM
    nb = kt_ref.shape[1] // MOBA_BLOCK

    nbp = -(-nb // 8) * 8

    @pl.when(c == 0)
    def _block_means():
        kbar_sc[...] = _columns([jnp.mean(ktf_ref[:, n * MOBA_BLOCK:(n + 1) * MOBA_BLOCK], axis=1, keepdims=True)
                                 for n in range(nb)], kbar_sc.shape[0]).T

    row = lax.broadcasted_iota(jnp.int32, (tq, MOBA_BLOCK), 0)
    col = lax.broadcasted_iota(jnp.int32, (tq, MOBA_BLOCK), 1)
    causal = own * MOBA_BLOCK + col <= c * tq + row
    n_iota = lax.broadcasted_iota(jnp.int32, (tq, nbp), 1)
    hs = [slice(h * HEAD_DIM, (h + 1) * HEAD_DIM) for h in range(heads)]
    q16, bias = [], []
    for h in range(heads):
        q = q_ref[:, hs[h]]
        gates_t = lax.dot_general(kbar_sc[:nbp, hs[h]], q, _NT, precision=HIGHEST, preferred_element_type=F32)
        bias.append(jnp.where(_top_block_select_t(gates_t, own).T > 0.5, 0.0, NEG_INF))
        q16.append((q * ATT_SCALE).astype(BF16))

    def scores(j, h):
        keys = pl.ds(pl.multiple_of(j * MOBA_BLOCK, MOBA_BLOCK), MOBA_BLOCK)
        return jnp.dot(q16[h], kt_ref[hs[h], keys], preferred_element_type=F32), vt_ref[hs[h], keys]

    state = []
    for h in range(heads):
        s, vb = scores(own, h)
        s = jnp.where(causal, s, NEG_INF)
        m = jnp.max(s, axis=1, keepdims=True)
        p = jnp.exp(s - m)
        state.append((m, jnp.sum(p, axis=1, keepdims=True),
                      lax.dot_general(p.astype(BF16), vb, _NT, preferred_element_type=F32)))

    def body(j, state):
        sv = [scores(j, h) for h in range(heads)]
        s = [sv[h][0] + jnp.sum(jnp.where(n_iota == j, bias[h], 0.0), axis=1, keepdims=True) for h in range(heads)]
        m_new = [jnp.maximum(state[h][0], jnp.max(s[h], axis=1, keepdims=True)) for h in range(heads)]
        p = [jnp.exp(s[h] - m_new[h]) for h in range(heads)]
        new = []
        for h in range(heads):
            m, l, acc = state[h]
            alpha = jnp.exp(m - m_new[h])
            new.append((m_new[h], alpha * l + jnp.sum(p[h], axis=1, keepdims=True),
                        alpha * acc + lax.dot_general(p[h].astype(BF16), sv[h][1], _NT, preferred_element_type=F32)))
        return tuple(new)

    state = lax.fori_loop(0, own, body, tuple(state))
    for h in range(heads):
        o_ref[:, hs[h]] = state[h][2] / state[h][1]


def _moba_prompt(q, kt16, vt16, kt, n_seq, seq_len, gw):
    width = min(MOBA_HEADS * HEAD_DIM, gw)
    n_groups = gw // width
    nc = seq_len // MOBA_QUERIES
    kv_spec = pl.BlockSpec((width, seq_len), lambda b, hg, c: (b * 2 * n_groups + hg, 0))
    q_spec = pl.BlockSpec((MOBA_QUERIES, width), lambda b, hg, c: (b * nc + c, hg))
    return pl.pallas_call(
        _moba_prompt_kernel,
        grid=(n_seq, n_groups, nc),
        in_specs=[q_spec, kv_spec, kv_spec, kv_spec],
        out_specs=q_spec,
        out_shape=jax.ShapeDtypeStruct((n_seq * seq_len, gw), F32),
        scratch_shapes=[pltpu.VMEM((128, width), F32)],
        compiler_params=_cparams(3),
        name="moba_prompt",
    )(q, kt16, vt16, kt)


def _sb_prompt_kernel(q_ref, kt_ref, vt_ref, o_ref):
    c = pl.program_id(2)
    tq = q_ref.shape[0]
    heads = q_ref.shape[1] // HEAD_DIM
    upper = _strict_lower(SB_KEYS)
    row = lax.broadcasted_iota(jnp.int32, (tq, SB_KEYS), 0)
    col = lax.broadcasted_iota(jnp.int32, (tq, SB_KEYS), 1)
    strict = col < row
    hs = [slice(h * HEAD_DIM, (h + 1) * HEAD_DIM) for h in range(heads)]
    q16 = [(q_ref[:, hs[h]] * ATT_SCALE).astype(BF16) for h in range(heads)]

    def tile(j, state, mask):
        keys = pl.ds(pl.multiple_of(j * SB_KEYS, SB_KEYS), SB_KEYS)
        z = [jnp.dot(q16[h], kt_ref[hs[h], keys], preferred_element_type=F32) for h in range(heads)]
        pairs = [_log_sigmoid_pair(zh) for zh in z]
        lk = [pr[0] if mask is None else jnp.where(mask, pr[0], 0.0) for pr in pairs]
        after = _suffix_sums(lk, upper)
        new = []
        for h in range(heads):
            r, acc = state[h]
            a = jnp.exp(pairs[h][1] + after[h] + r)
            if mask is not None:
                a = jnp.where(mask, a, 0.0)
            acc = acc + lax.dot_general(a.astype(BF16), vt_ref[hs[h], keys], _NT, preferred_element_type=F32)
            new.append((r + jnp.sum(lk[h], axis=1, keepdims=True), acc))
        return tuple(new)

    state = tile(c, tuple((jnp.zeros((tq, 1), F32), jnp.zeros((tq, HEAD_DIM), F32)) for _ in range(heads)), strict)

    def live(state):
        r_max = functools.reduce(jnp.maximum, [st[0] for st in state])
        return (jnp.max(r_max) > SB_NEGLIGIBLE).astype(jnp.int32)

    def cond(carry):
        i, alive, _ = carry
        return jnp.logical_and(i < c, alive > 0)

    def body(carry):
        i, _, state = carry
        state = tile(c - 1 - i, state, None)
        return i + 1, live(state), state

    _, _, state = lax.while_loop(cond, body, (jnp.int32(0), live(state), state))
    for h in range(heads):
        o_ref[:, hs[h]] = state[h][1]


def _sb_prompt(q, kt16, vt16, n_seq, seq_len, gw):
    width = min(SB_HEADS * HEAD_DIM, gw)
    n_groups = gw // width
    nc = seq_len // SB_KEYS
    kv_spec = pl.BlockSpec((width, seq_len), lambda b, hg, c: (b * 2 * n_groups + n_groups + hg, 0))
    return pl.pallas_call(
        _sb_prompt_kernel,
        grid=(n_seq, n_groups, nc),
        in_specs=[pl.BlockSpec((SB_KEYS, width), lambda b, hg, c: (b * nc + c, n_groups + hg)), kv_spec, kv_spec],
        out_specs=pl.BlockSpec((SB_KEYS, width), lambda b, hg, c: (b * nc + c, hg)),
        out_shape=jax.ShapeDtypeStruct((n_seq * seq_len, gw), F32),
        compiler_params=_cparams(3),
        name="sb_prompt",
    )(q, kt16, vt16)


def _decode_kernel(pt_ref, q_ref, knew_ref, vnew_ref, *refs, n_pages, nq):
    del pt_ref
    kv_refs = refs[:2 * DECODE_PAGES]
    oa_ref, ob_ref, qbd_sc, g_sc, m_sc, l_sc, o_sc, r_sc, accb_sc = refs[2 * DECODE_PAGES:]
    p = pl.program_id(1)
    n_steps = n_pages // DECODE_PAGES
    rows = qbd_sc.shape[0] // 2
    heads, hd, page = kv_refs[0].shape[1:]
    gh = heads // 2
    gw = gh * hd
    page_refs = [(kv_refs[2 * j + t // 2], t % 2) for j in range(DECODE_PAGES) for t in range(4)]
    pn = knew_ref.shape[2]
    pages_per_block = MOBA_BLOCK // page
    blocks_per_step = DECODE_PAGES // pages_per_block
    n_blocks = n_pages // pages_per_block
    lane_w = m_sc.shape[2]

    @pl.when(p == 0)
    def _block_diagonal_queries():
        shape = qbd_sc.shape
        rr = lax.broadcasted_iota(jnp.int32, shape, 0)
        cc = lax.broadcasted_iota(jnp.int32, shape, 1)
        sr = lax.broadcasted_iota(jnp.int32, (shape[0], q_ref.shape[1]), 0)
        si = lax.broadcasted_iota(jnp.int32, (shape[0], q_ref.shape[1]), 1)
        spread = jnp.dot(jnp.where(_imod(sr, nq) == si, 1.0, 0.0), q_ref[0], precision=HIGHEST,
                         preferred_element_type=F32)
        qbd_sc[...] = (jnp.where(_idiv(rr, nq) == _idiv(cc, hd), spread, 0.0) * ATT_SCALE).astype(BF16)

    qa16 = qbd_sc[:rows, :gw]
    qb16 = qbd_sc[rows:, gw:]
    new_query = _imod(lax.broadcasted_iota(jnp.int32, (rows, pn), 0), nq)
    new_key = lax.broadcasted_iota(jnp.int32, (rows, pn), 1)

    def mat(ref_group):
        ref, g = ref_group
        return ref[0, g * gh:(g + 1) * gh].reshape(gw, page).astype(BF16)

    @pl.when(p == 0)
    def _init():
        z = lax.dot_general(qb16, knew_ref[0, 1].astype(BF16), _NT, preferred_element_type=F32)
        mask = new_key < new_query
        lk, ls = _log_sigmoid_pair(z)
        lk = jnp.where(mask, lk, 0.0)
        a = jnp.where(mask, jnp.exp(ls + _suffix_sums([lk], _strict_lower(pn))[0]), 0.0)
        accb_sc[...] = jnp.dot(a.astype(BF16), vnew_ref[0, 1].astype(BF16), preferred_element_type=F32)
        r_sc[...] = jnp.broadcast_to(jnp.sum(lk, axis=1, keepdims=True), r_sc.shape)

    all_pages = range(DECODE_PAGES)
    blocks = [range(b * pages_per_block, (b + 1) * pages_per_block) for b in range(blocks_per_step)]
    block_of = lambda vals, op: [functools.reduce(op, [vals[j] for j in pages]) for pages in blocks]
    s = [jnp.dot(qa16, mat(page_refs[4 * j]), preferred_element_type=F32) for j in all_pages]
    z = [jnp.dot(qb16, mat(page_refs[4 * j + 1]), preferred_element_type=F32) for j in all_pages]
    m_blk = block_of([jnp.max(sj, axis=1, keepdims=True) for sj in s], jnp.maximum)
    g_blk = block_of([jnp.sum(sj, axis=1, keepdims=True) for sj in s], jnp.add)
    pr = [jnp.exp(s[j] - m_blk[j // pages_per_block]) for j in all_pages]
    l_blk = block_of([jnp.sum(pj, axis=1, keepdims=True) for pj in pr], jnp.add)
    pv = [lax.dot_general(pr[j].astype(BF16), mat(page_refs[4 * j + 2]), _NT, preferred_element_type=F32)
          for j in all_pages]
    o_blk = block_of(pv, jnp.add)
    for b in range(blocks_per_step):
        n = n_blocks - 1 - (p * blocks_per_step + b)
        m_sc[n] = jnp.broadcast_to(m_blk[b], (rows, lane_w))
        l_sc[n] = jnp.broadcast_to(l_blk[b], (rows, lane_w))
        g_sc[n] = jnp.broadcast_to(g_blk[b], (rows, lane_w))
        o_sc[n] = o_blk[b]

    pairs = [_log_sigmoid_pair(zj) for zj in z]
    after = _suffix_sums([pr_[0] for pr_ in pairs], _strict_lower(page))
    lk_sum = [jnp.sum(pr_[0], axis=1, keepdims=True) for pr_ in pairs]
    r = [r_sc[:, :1]]
    for j in all_pages:
        r.append(r[j] + lk_sum[j])
    a = [jnp.exp(pairs[j][1] + after[j] + r[j]).astype(BF16) for j in all_pages]
    accb_sc[...] = functools.reduce(jnp.add, [accb_sc[...]] + [
        lax.dot_general(a[j], mat(page_refs[4 * j + 3]), _NT, preferred_element_type=F32) for j in all_pages])
    r_sc[...] = jnp.broadcast_to(r[DECODE_PAGES], r_sc.shape)

    @pl.when(p == n_steps - 1)
    def _finish():
        gates = _columns([g_sc[b][:, :1] for b in range(n_blocks)], n_blocks)
        bias = _top_block_bias(gates, n_blocks)
        sn = lax.dot_general(qa16, knew_ref[0, 0].astype(BF16), _NT, preferred_element_type=F32)
        sn = jnp.where(new_key <= new_query, sn, NEG_INF)
        m_own = jnp.max(sn, axis=1, keepdims=True)
        pn_ = jnp.exp(sn - m_own)
        l_own = jnp.sum(pn_, axis=1, keepdims=True)
        o_own = jnp.dot(pn_.astype(BF16), vnew_ref[0, 0].astype(BF16), preferred_element_type=F32)
        m_all = m_own
        for b in range(n_blocks):
            m_all = jnp.maximum(m_all, m_sc[b][:, :1] + bias[:, b:b + 1])
        w_own = jnp.exp(m_own - m_all)
        num = w_own * o_own
        den = w_own * l_own
        for b in range(n_blocks):
            w = jnp.exp(m_sc[b][:, :1] + bias[:, b:b + 1] - m_all)
            num = num + w * o_sc[b]
            den = den + w * l_sc[b][:, :1]
        outa = num / den
        rr = lax.broadcasted_iota(jnp.int32, (rows, gw), 0)
        cc = lax.broadcasted_iota(jnp.int32, (rows, gw), 1)
        diag = _idiv(rr, nq) == _idiv(cc, hd)
        si = lax.broadcasted_iota(jnp.int32, (oa_ref.shape[1], rows), 0)
        sr = lax.broadcasted_iota(jnp.int32, (oa_ref.shape[1], rows), 1)
        pick = jnp.where(_imod(sr, nq) == si, 1.0, 0.0)
        oa_ref[0] = jnp.dot(pick, jnp.where(diag, outa, 0.0), precision=HIGHEST, preferred_element_type=F32)
        ob_ref[0] = jnp.dot(pick, jnp.where(diag, accb_sc[...], 0.0), precision=HIGHEST,
                            preferred_element_type=F32)


def _decode_attention(page_table, q, knew, vnew, cache_kt, cache_vt, nq):
    n_dec, n_pages = page_table.shape
    _, heads, hd, page = cache_kt.shape
    gh = heads // 2
    gw = gh * hd
    rows = gh * nq
    n_blocks = n_pages * page // MOBA_BLOCK
    per_seq = lambda a: pl.BlockSpec((1,) + a.shape[1:], lambda b, p, pt: (b,) + (0,) * (a.ndim - 1))
    page_spec = lambda j: pl.BlockSpec(
        (1, heads, hd, page), lambda b, p, pt: (pt[b, n_pages - 1 - (p * DECODE_PAGES + j)], 0, 0, 0))
    out_spec = pl.BlockSpec((1, OUT_ROWS, gw), lambda b, p, pt: (b, 0, 0))
    assert n_pages % DECODE_PAGES == 0 and DECODE_PAGES % (MOBA_BLOCK // page) == 0 and nq <= OUT_ROWS
    grid_spec = pltpu.PrefetchScalarGridSpec(
        num_scalar_prefetch=1,
        grid=(n_dec, n_pages // DECODE_PAGES),
        in_specs=[per_seq(q), per_seq(knew), per_seq(vnew)]
        + [page_spec(j) for j in range(DECODE_PAGES) for _ in range(2)],
        out_specs=[out_spec, out_spec],
        scratch_shapes=[pltpu.VMEM((2 * rows, 2 * gw), BF16),
                        pltpu.VMEM((n_blocks, rows, 128), F32),
                        pltpu.VMEM((n_blocks, rows, 128), F32),
                        pltpu.VMEM((n_blocks, rows, 128), F32),
                        pltpu.VMEM((n_blocks, rows, gw), F32),
                        pltpu.VMEM((rows, 128), F32),
                        pltpu.VMEM((rows, gw), F32)],
    )
    return pl.pallas_call(
        functools.partial(_decode_kernel, n_pages=n_pages, nq=nq),
        grid_spec=grid_spec,
        out_shape=[jax.ShapeDtypeStruct((n_dec, OUT_ROWS, gw), F32)] * 2,
        compiler_params=_cparams(2),
        name="decode_attention",
    )(page_table, q, knew, vnew, *([cache_kt, cache_vt] * DECODE_PAGES))


def _merge_kernel(oa_ref, ob_ref, x_ref, ga_ref, gb_ref, w_ref, g_ref, b_ref, y_ref, *, alpha):
    gw = oa_ref.shape[1]
    ya = _rms_norm(oa_ref[...], ga_ref[...]).astype(BF16)
    yb = _rms_norm(ob_ref[...], gb_ref[...]).astype(BF16)
    mix = (jnp.dot(ya, w_ref[:gw, :], preferred_element_type=F32)
           + jnp.dot(yb, w_ref[gw:, :], preferred_element_type=F32))
    y_ref[...] = _layer_norm(alpha * x_ref[...] + mix, g_ref[...], b_ref[...])


def _merge(oa, ob, x, g_a, g_b, w16, ln_g, ln_b, alpha):
    t, d = x.shape
    gw = oa.shape[1]
    rows = _row_tile(t)
    row = lambda width: pl.BlockSpec((rows, width), lambda i: (i, 0))
    full = lambda a: pl.BlockSpec(a.shape, lambda i: (0, 0))
    return pl.pallas_call(
        functools.partial(_merge_kernel, alpha=alpha),
        grid=(t // rows,),
        in_specs=[row(gw), row(gw), row(d), full(g_a), full(g_b), full(w16), full(ln_g), full(ln_b)],
        out_specs=row(d),
        out_shape=jax.ShapeDtypeStruct((t, d), F32),
        compiler_params=_cparams(1),
        name="merge_out_proj",
    )(oa, ob, x, g_a, g_b, w16, ln_g, ln_b)


def _mem_kv_kernel(m_ref, w_ref, k_ref, v_ref, k16_ref, v16_ref):
    width = k_ref.shape[1]
    m16 = m_ref[...].astype(BF16)
    k = jnp.dot(m16, w_ref[:, :width], preferred_element_type=F32)
    v = jnp.dot(m16, w_ref[:, width:], preferred_element_type=F32)
    k_ref[...] = k
    v_ref[...] = v
    k16_ref[...] = k.astype(BF16)
    v16_ref[...] = v.astype(BF16)


def _mem_kv(mem, w16):
    t, d = mem.shape
    width = w16.shape[1] // 2
    rows = _row_tile(t)
    row = lambda w: pl.BlockSpec((rows, w), lambda i: (i, 0))
    return pl.pallas_call(
        _mem_kv_kernel,
        grid=(t // rows,),
        in_specs=[row(d), pl.BlockSpec(w16.shape, lambda i: (0, 0))],
        out_specs=[row(width)] * 4,
        out_shape=[jax.ShapeDtypeStruct((t, width), F32)] * 2 + [jax.ShapeDtypeStruct((t, width), BF16)] * 2,
        compiler_params=_cparams(1),
        name="mem_kv",
    )(mem, w16)


def _mem_heads(q16, head_kv, row_mask=None):
    hd = q16.shape[1] // MEM_HEADS
    outs = []
    for h in range(MEM_HEADS):
        k16, v16 = head_kv(h)
        s = lax.dot_general(q16[:, h * hd:(h + 1) * hd], k16, _NT, preferred_element_type=F32)
        m = jnp.max(s, axis=1, keepdims=True)
        p = jnp.exp(s - m)
        l = jnp.sum(p, axis=1, keepdims=True)
        o = jnp.dot(p.astype(BF16), v16, preferred_element_type=F32) / l
        outs.append(o if row_mask is None else jnp.where(row_mask, o, 0.0))
    return jnp.concatenate(outs, axis=1)


def _mem_attend_kernel(x_ref, wq_ref, mk_ref, mv_ref, wo_ref, g_ref, b_ref, y_ref, *, alpha, n_tiles):
    @pl.when(pl.program_id(0) < n_tiles)
    def _tile():
        x = x_ref[...]
        hd = x.shape[1] // MEM_HEADS
        q16 = (jnp.dot(x.astype(BF16), wq_ref[...], preferred_element_type=F32) * hd ** -0.5).astype(BF16)
        o = _mem_heads(q16, lambda h: (mk_ref[:, h * hd:(h + 1) * hd], mv_ref[:, h * hd:(h + 1) * hd]))
        y = jnp.dot(o.astype(BF16), wo_ref[...], preferred_element_type=F32)
        y_ref[...] = _layer_norm(alpha * x + y, g_ref[...], b_ref[...])

    @pl.when(pl.program_id(0) >= n_tiles)
    def _tail():
        y_ref[...] = jnp.zeros(y_ref.shape, F32)


def _mem_attend_prompt(x, wq16, mk16, mv16, wo16, ln_g, ln_b, n_seq, alpha, out_rows):
    t, d = x.shape
    seq_len = t // n_seq
    n_mem = mk16.shape[0] // n_seq
    rows = _row_tile(seq_len)
    tiles = seq_len // rows
    n_tiles = n_seq * tiles
    assert (out_rows - t) % rows == 0
    last = lambda i: jnp.minimum(i, n_tiles - 1)
    full = lambda a: pl.BlockSpec(a.shape, lambda i: (0, 0))
    mem = pl.BlockSpec((n_mem, d), lambda i: (last(i) // tiles, 0))
    return pl.pallas_call(
        functools.partial(_mem_attend_kernel, alpha=alpha, n_tiles=n_tiles),
        grid=(out_rows // rows,),
        in_specs=[pl.BlockSpec((rows, d), lambda i: (last(i), 0)), full(wq16), mem, mem, full(wo16),
                  full(ln_g), full(ln_b)],
        out_specs=pl.BlockSpec((rows, d), lambda i: (i, 0)),
        out_shape=jax.ShapeDtypeStruct((out_rows, d), F32),
        compiler_params=_cparams(1),
        name="mem_attend_prompt",
    )(x, wq16, mk16, mv16, wo16, ln_g, ln_b)


def _mem_attend_sample_kernel(x_ref, wq_ref, mk_ref, mv_ref, wo_ref, g_ref, b_ref, all_ref, y_ref, *, alpha, nq):
    del all_ref
    x = x_ref[...]
    hd = x.shape[1] // MEM_HEADS
    q16 = (jnp.dot(x.astype(BF16), wq_ref[...], preferred_element_type=F32) * hd ** -0.5).astype(BF16)
    seq_of_row = _idiv(lax.broadcasted_iota(jnp.int32, (x.shape[0], 1), 0), nq)
    o = jnp.zeros(x.shape, F32)
    for g in range(mk_ref.shape[0]):
        head_kv = lambda h, g=g: (mk_ref[g, :, h * hd:(h + 1) * hd].astype(BF16),
                                  mv_ref[g, :, h * hd:(h + 1) * hd].astype(BF16))
        o = o + _mem_heads(q16, head_kv, seq_of_row == g)
    y = jnp.dot(o.astype(BF16), wo_ref[...], preferred_element_type=F32)
    y_ref[...] = _layer_norm(alpha * x + y, g_ref[...], b_ref[...])


def _mem_attend_sample(x, wq16, cache_mk, cache_mv, wo16, ln_g, ln_b, nq, alpha, all_rows):
    t, d = x.shape
    n_dec, n_mem, _ = cache_mk.shape
    rows = MEM_GROUP * nq
    base = (all_rows.shape[0] - t) // rows
    assert (all_rows.shape[0] - t) % rows == 0
    row = pl.BlockSpec((rows, d), lambda i: (i, 0))
    full = lambda a: pl.BlockSpec(a.shape, lambda i: (0, 0))
    mem = pl.BlockSpec((MEM_GROUP, n_mem, d), lambda i: (i, 0, 0))
    return pl.pallas_call(
        functools.partial(_mem_attend_sample_kernel, alpha=alpha, nq=nq),
        grid=(n_dec // MEM_GROUP,),
        in_specs=[row, full(wq16), mem, mem, full(wo16), full(ln_g), full(ln_b), pl.BlockSpec(memory_space=pl.ANY)],
        out_specs=pl.BlockSpec((rows, d), lambda i: (base + i, 0)),
        out_shape=jax.ShapeDtypeStruct(all_rows.shape, F32),
        input_output_aliases={7: 0},
        compiler_params=_cparams(1),
        name="mem_attend_sample",
    )(x, wq16, cache_mk, cache_mv, wo16, ln_g, ln_b, all_rows)


def _router_kernel(x_ref, w_ref, b_ref, e_ref, g_ref, rank_ref, cnt_ref, run_sc):
    rows = x_ref.shape[0]
    logits = jnp.dot(x_ref[...], w_ref[...], precision=HIGHEST, preferred_element_type=F32) + b_ref[...]
    n_exp = logits.shape[1]
    e_iota = lax.broadcasted_iota(jnp.int32, logits.shape, 1)
    k_iota = lax.broadcasted_iota(jnp.int32, e_ref.shape, 1)
    top_e = jnp.zeros(e_ref.shape, jnp.int32)
    top_v = jnp.zeros(e_ref.shape, F32)
    onehot = []
    for k in range(TOP_K):
        mx = jnp.max(logits, axis=1, keepdims=True)
        idx = jnp.min(jnp.where(logits == mx, e_iota, n_exp), axis=1, keepdims=True)
        top_e = jnp.where(k_iota == k, idx, top_e)
        top_v = jnp.where(k_iota == k, mx, top_v)
        onehot.append(jnp.where(e_iota == idx, 1.0, 0.0))
        logits = jnp.where(e_iota == idx, NEG_INF, logits)
    w = jnp.exp(top_v - top_v[:, :1])
    e_ref[...] = top_e
    g_ref[...] = w / jnp.sum(w, axis=1, keepdims=True)

    @pl.when(pl.program_id(0) == 0)
    def _zero():
        run_sc[...] = jnp.zeros(run_sc.shape, F32)

    chosen = functools.reduce(jnp.add, onehot)
    before = jnp.dot(_strict_lower(rows), chosen.astype(BF16), preferred_element_type=F32) + run_sc[...]
    rank = jnp.zeros(e_ref.shape, F32)
    for k in range(TOP_K):
        rank = jnp.where(k_iota == k, jnp.sum(onehot[k] * before, axis=1, keepdims=True), rank)
    rank_ref[...] = rank.astype(jnp.int32)
    run_sc[...] = run_sc[...] + jnp.sum(chosen, axis=0, keepdims=True)
    cnt_ref[...] = run_sc[...]


def _router(x, w_router, b_router):
    t, d = x.shape
    n_exp = w_router.shape[1]
    rows = _row_tile(t)
    out = pl.BlockSpec((rows, TOP_K), lambda i: (i, 0))
    return pl.pallas_call(
        _router_kernel,
        grid=(t // rows,),
        in_specs=[pl.BlockSpec((rows, d), lambda i: (i, 0)),
                  pl.BlockSpec(w_router.shape, lambda i: (0, 0)),
                  pl.BlockSpec(b_router.shape, lambda i: (0, 0))],
        out_specs=[out, out, out, pl.BlockSpec((1, n_exp), lambda i: (0, 0))],
        out_shape=[jax.ShapeDtypeStruct((t, TOP_K), jnp.int32), jax.ShapeDtypeStruct((t, TOP_K), F32),
                   jax.ShapeDtypeStruct((t, TOP_K), jnp.int32), jax.ShapeDtypeStruct((1, n_exp), F32)],
        scratch_shapes=[pltpu.VMEM((1, n_exp), F32)],
        compiler_params=_cparams(1),
        name="router_top4",
    )(x, w_router, b_router)


def _row_copy(src_hbm, src_row, dst, r, sem):
    return pltpu.make_async_copy(src_hbm.at[pl.ds(src_row, 1)], dst.at[pl.ds(r, 1)], sem)


def _row_gather(src_hbm, idx_ref, dst, sem, n_rows):
    for r in range(n_rows):
        _row_copy(src_hbm, idx_ref[0, 0, r], dst, r, sem).start()


def _row_gather_wait(src_hbm, dst, sem, n_rows):
    def wait(r, carry):
        _row_copy(src_hbm, 0, dst, r, sem).wait()
        return carry
    lax.fori_loop(0, n_rows, wait, 0, unroll=DMA_UNROLL)


def _dispatch_kernel(idx_ref, x_ref, rows_in, rows_out, stage, sems):
    del rows_in
    s = pl.program_id(0)
    n_steps = pl.num_programs(0)
    rows = x_ref.shape[0]
    slot = s % 2

    def copy(buf, t, dst_row):
        return pltpu.make_async_copy(stage.at[buf, pl.ds(t, 1)], rows_out.at[pl.ds(dst_row, 1)], sems.at[buf])

    def drain(buf):
        def wait(r, carry):
            copy(buf, 0, 0).wait()
            return carry
        lax.fori_loop(0, TOP_K * rows, wait, 0, unroll=DMA_UNROLL)

    @pl.when(s >= 2)
    def _reuse():
        drain(slot)

    stage[slot] = x_ref[...]
    for k in range(TOP_K):
        for t in range(rows):
            copy(slot, t, idx_ref[0, 0, k * rows + t]).start()

    @pl.when(s == n_steps - 1)
    def _finish():
        @pl.when(s >= 1)
        def _other():
            drain(1 - slot)
        drain(slot)


def _dispatch(x, dest, n_rows):
    t, d = x.shape
    n_steps, _, n_copy = dest.shape
    rows = n_copy // TOP_K
    return pl.pallas_call(
        _dispatch_kernel,
        grid=(n_steps,),
        in_specs=[pl.BlockSpec((1, 1, n_copy), lambda s: (s, 0, 0), memory_space=pltpu.SMEM),
                  pl.BlockSpec((rows, d), lambda s: (s, 0)),
                  pl.BlockSpec(memory_space=pl.ANY)],
        out_specs=pl.BlockSpec(memory_space=pl.ANY),
        out_shape=jax.ShapeDtypeStruct((n_rows, d), F32),
        input_output_aliases={2: 0},
        scratch_shapes=[pltpu.VMEM((2, rows, d), F32), pltpu.SemaphoreType.DMA((2,))],
        compiler_params=_cparams(1),
        name="moe_dispatch",
    )(dest, x, jnp.zeros((n_rows, d), F32))


def _expert_ffn_kernel(be_ref, nu_ref, x_ref, wgu_ref, bgu_ref, wd_ref, bd_ref, y_ref, wgu16, wd16):
    s = pl.program_id(0)
    active = s < nu_ref[0]
    new_expert = jnp.logical_or(s == 0, be_ref[s] != be_ref[jnp.maximum(s - 1, 0)])

    @pl.when(jnp.logical_and(active, new_expert))
    def _cast_weights():
        wgu16[...] = wgu_ref[0].astype(BF16)
        wd16[...] = wd_ref[0].astype(BF16)

    @pl.when(active)
    def _compute():
        f = wd_ref.shape[1]
        hgu = jnp.dot(x_ref[...].astype(BF16), wgu16[...], preferred_element_type=F32) + bgu_ref[0]
        gate = jnp.minimum(hgu[:, :f], SWIGLU_LIMIT)
        up = jnp.clip(hgu[:, f:], -SWIGLU_LIMIT, SWIGLU_LIMIT)
        act = (up + 1.0) * gate * jax.nn.sigmoid(SWIGLU_ALPHA * gate)
        y_ref[...] = jnp.dot(act.astype(BF16), wd16[...], preferred_element_type=F32) + bd_ref[0]

    @pl.when(jnp.logical_not(active))
    def _unused():
        y_ref[...] = jnp.zeros(y_ref.shape, F32)


def _expert_ffn(x_rows, block_e, n_used, w_gu, b_gu, w_down, b_down, rows):
    n_exp, d, f2 = w_gu.shape
    f = f2 // 2
    n_blk = x_rows.shape[0] // rows
    by_expert = lambda shape: pl.BlockSpec(shape, lambda s, be, nu: (be[s], 0, 0))
    row_spec = pl.BlockSpec((rows, d), lambda s, be, nu: (s, 0))
    grid_spec = pltpu.PrefetchScalarGridSpec(
        num_scalar_prefetch=2,
        grid=(n_blk,),
        in_specs=[row_spec, by_expert((1, d, f2)), by_expert((1, 1, f2)), by_expert((1, f, d)),
                  by_expert((1, 1, d))],
        out_specs=row_spec,
        scratch_shapes=[pltpu.VMEM((d, f2), BF16), pltpu.VMEM((f, d), BF16)],
    )
    return pl.pallas_call(
        _expert_ffn_kernel,
        grid_spec=grid_spec,
        out_shape=jax.ShapeDtypeStruct((n_blk * rows, d), F32),
        compiler_params=_cparams(1),
        name="expert_ffn",
    )(block_e, n_used, x_rows, w_gu, b_gu.reshape(n_exp, 1, f2), w_down, b_down.reshape(n_exp, 1, d))


def _combine_kernel(idx_ref, y_hbm, x_ref, gate_ref, g_ref, b_ref, oa_ref, ob_ref, ybuf, sems, *, alpha, a_tiles):
    s = pl.program_id(0)
    n_steps = pl.num_programs(0) - 1
    rows = x_ref.shape[0]
    n_copy = TOP_K * rows

    @pl.when(s < n_steps)
    def _fetch():
        _row_gather(y_hbm, idx_ref, ybuf.at[s % 2], sems.at[s % 2], n_copy)

    @pl.when(s >= 1)
    def _compute():
        slot = (s - 1) % 2
        _row_gather_wait(y_hbm, ybuf.at[slot], sems.at[slot], n_copy)
        gates = gate_ref[...]
        ffn = jnp.zeros(x_ref.shape, F32)
        for k in range(TOP_K):
            ffn = ffn + gates[:, k:k + 1] * ybuf[slot, k * rows:(k + 1) * rows, :]
        out = _layer_norm(alpha * x_ref[...] + ffn, g_ref[...], b_ref[...])

        @pl.when(s - 1 < a_tiles)
        def _first_group():
            oa_ref[...] = out

        @pl.when(s - 1 >= a_tiles)
        def _second_group():
            ob_ref[...] = out


def _combine(y_rows, dest, x, gates, ln_g, ln_b, alpha, n_first):
    t, d = x.shape
    n_steps, _, n_copy = dest.shape
    rows = n_copy // TOP_K
    a_tiles = n_first // rows
    assert n_first % rows == 0 and 0 < a_tiles < n_steps
    idx_spec = pl.BlockSpec((1, 1, n_copy), lambda s: (jnp.minimum(s, n_steps - 1), 0, 0),
                            memory_space=pltpu.SMEM)
    row = lambda w: pl.BlockSpec((rows, w), lambda s: (jnp.maximum(s - 1, 0), 0))
    full = lambda a: pl.BlockSpec(a.shape, lambda s: (0, 0))
    return pl.pallas_call(
        functools.partial(_combine_kernel, alpha=alpha, a_tiles=a_tiles),
        grid=(n_steps + 1,),
        in_specs=[idx_spec, pl.BlockSpec(memory_space=pl.ANY), row(d), row(TOP_K), full(ln_g), full(ln_b)],
        out_specs=[pl.BlockSpec((rows, d), lambda s: (jnp.clip(s - 1, 0, a_tiles - 1), 0)),
                   pl.BlockSpec((rows, d), lambda s: (jnp.maximum(s - 1 - a_tiles, 0), 0))],
        out_shape=[jax.ShapeDtypeStruct((n_first, d), F32), jax.ShapeDtypeStruct((t - n_first, d), F32)],
        scratch_shapes=[pltpu.VMEM((2, n_copy, d), F32), pltpu.SemaphoreType.DMA((2,))],
        compiler_params=_cparams(1),
        name="moe_combine",
    )(dest, y_rows, x, gates, ln_g, ln_b)


def _dispatch_plan(top_e, rank, counts, rows):
    n_assign = top_e.size
    n_exp = counts.shape[0]
    flat_e = top_e.reshape(n_assign)
    padded = (counts + rows - 1) // rows * rows
    pad_end = jnp.cumsum(padded)
    pad_start = pad_end - padded
    dest = (pad_start[flat_e] + rank.reshape(n_assign)).astype(jnp.int32)
    n_blk = -(-(n_assign + n_exp * (rows - 1)) // rows)
    block_start = jnp.arange(n_blk, dtype=jnp.int32) * rows
    block_e = jnp.minimum(jnp.sum((pad_end[None, :] <= block_start[:, None]).astype(jnp.int32), axis=1), n_exp - 1)
    n_used = (pad_end[-1] // rows).astype(jnp.int32).reshape(1)
    return dest, block_e, n_used, n_blk


def _moe(tok, w_router, b_router, w_gu, b_gu, w_down, b_down, ln_g, ln_b, alpha, n_first):
    t, d = tok.shape
    n_exp = w_router.shape[1]
    top_e, gates, rank, counts = _router(tok, w_router, b_router.reshape(1, n_exp))
    dest, block_e, n_used, n_blk = _dispatch_plan(top_e, rank, counts[0].astype(jnp.int32), EXPERT_ROWS)
    steps = t // COMBINE_ROWS
    dest_steps = dest.reshape(steps, COMBINE_ROWS, TOP_K).transpose(0, 2, 1).reshape(steps, 1, TOP_K * COMBINE_ROWS)
    x_rows = _dispatch(tok, dest_steps, n_blk * EXPERT_ROWS)
    y_rows = _expert_ffn(x_rows, block_e, n_used, w_gu, b_gu, w_down, b_down, EXPERT_ROWS)
    return _combine(y_rows, dest_steps, tok, gates, ln_g, ln_b, alpha, n_first)


def _rope_tables(pos, n_heads):
    half = HEAD_DIM // 2
    inv_freq = ROPE_THETA ** (-jnp.arange(half, dtype=F32) / half)
    ang = pos.astype(F32)[:, None] * inv_freq[None, :]
    cos = jnp.cos(ang)
    sin = jnp.sin(ang)
    return (jnp.tile(jnp.concatenate([cos, cos], axis=1), (1, n_heads)),
            jnp.tile(jnp.concatenate([-sin, sin], axis=1), (1, n_heads)), cos.T, sin.T)


def kernel(x_prompt, x_sample, mem_prompt, cache_k, cache_v, cache_mem_k, cache_mem_v, page_table,
           w_in, g_moba, g_sb, w_out, ln1_g, ln1_b, w_mq, w_mkv, w_mo, ln2_g, ln2_b,
           w_router, b_router, w_gu, b_gu, w_down, b_down, ln3_g, ln3_b):
    n_seq, seq_len, d = x_prompt.shape
    n_dec, nq, _ = x_sample.shape
    depth = w_in.shape[0]
    gw = w_in.shape[2] // 6
    g_heads = gw // HEAD_DIM
    heads = 2 * g_heads
    n_pages = page_table.shape[1]
    page = cache_k.shape[2]
    past_len = n_pages * page
    n_mem = mem_prompt.shape[1]
    alpha = (2 * depth) ** 0.25
    n_tok_p = n_seq * seq_len
    n_tok_s = n_dec * nq
    assert seq_len % MOBA_BLOCK == 0 and past_len % MOBA_BLOCK == 0 and MOBA_BLOCK % page == 0
    assert nq <= NEW_PAD and n_dec % MEM_GROUP == 0
    assert (n_tok_p + n_tok_s) % COMBINE_ROWS == 0 and n_tok_s % PROJ_ROWS == 0 and seq_len % PROJ_ROWS == 0

    tables_p = _rope_tables(jnp.arange(seq_len), g_heads)
    tables_s = _rope_tables(jnp.tile(past_len + jnp.arange(nq), n_dec), g_heads)
    xp = x_prompt.reshape(n_tok_p, d)
    xs = x_sample.reshape(n_tok_s, d)
    row2 = lambda a: a.reshape(1, -1)

    def to_positions(at, n, length):
        return at.reshape(n, heads, HEAD_DIM, length).transpose(0, 3, 1, 2)

    def new_rows(at):
        a = at.reshape(2, gw, n_dec, nq).transpose(2, 0, 3, 1)
        return jnp.pad(a, ((0, 0), (0, 0), (0, NEW_PAD - nq), (0, 0)))

    outs = [[] for _ in range(6)]
    for l in range(depth):
        w = w_in[l].astype(BF16)
        col = lambda g: w[:, g * gw:(g + 1) * gw]
        wq16 = jnp.concatenate([col(0), col(3)], axis=1)
        wkvt16 = jnp.concatenate([col(1), col(4), col(2), col(5)], axis=1).T
        w_out16 = w_out[l].astype(BF16)
        q_p, kt_p, vt_p, kt16_p, vt16_p = _qkv_project(xp, wq16, wkvt16, *tables_p, PROJ_ROWS, seq_len)
        oa_p = _moba_prompt(q_p, kt16_p, vt16_p, kt_p, n_seq, seq_len, gw)
        ob_p = _sb_prompt(q_p, kt16_p, vt16_p, n_seq, seq_len, gw)
        outs[0].append(to_positions(kt_p, n_seq, seq_len))
        outs[1].append(to_positions(vt_p, n_seq, seq_len))
        q_s, kt_s, vt_s, _, _ = _qkv_project(xs, wq16, wkvt16, *tables_s, PROJ_ROWS, n_tok_s)
        oa_s, ob_s = _decode_attention(
            page_table, jnp.pad(q_s.reshape(n_dec, nq, 2 * gw), ((0, 0), (0, OUT_ROWS - nq), (0, 0))),
            new_rows(kt_s), new_rows(vt_s),
            cache_k[l].transpose(0, 2, 3, 1), cache_v[l].transpose(0, 2, 3, 1), nq)
        outs[2].append(to_positions(kt_s, 1, n_tok_s).reshape(n_dec, nq, heads, HEAD_DIM))
        outs[3].append(to_positions(vt_s, 1, n_tok_s).reshape(n_dec, nq, heads, HEAD_DIM))
        merge = functools.partial(_merge, g_a=row2(g_moba[l]), g_b=row2(g_sb[l]), w16=w_out16,
                                  ln_g=row2(ln1_g[l]), ln_b=row2(ln1_b[l]), alpha=alpha)
        xp = merge(oa_p, ob_p, xp)
        xs = merge(oa_s[:, :nq].reshape(n_tok_s, gw), ob_s[:, :nq].reshape(n_tok_s, gw), xs)
        mk, mv, mk16, mv16 = _mem_kv(mem_prompt.reshape(n_seq * n_mem, d), w_mkv[l].astype(BF16))
        outs[4].append(mk.reshape(n_seq, n_mem, MEM_HEADS, d // MEM_HEADS))
        outs[5].append(mv.reshape(n_seq, n_mem, MEM_HEADS, d // MEM_HEADS))
        wmq16 = w_mq[l].astype(BF16)
        wmo16 = w_mo[l].astype(BF16)
        tok = _mem_attend_prompt(xp, wmq16, mk16, mv16, wmo16, row2(ln2_g[l]), row2(ln2_b[l]), n_seq, alpha,
                                 n_tok_p + n_tok_s)
        tok = _mem_attend_sample(xs, wmq16, cache_mem_k[l].astype(BF16).reshape(n_dec, n_mem, d),
                                 cache_mem_v[l].astype(BF16).reshape(n_dec, n_mem, d), wmo16,
                                 row2(ln2_g[l]), row2(ln2_b[l]), nq, alpha, tok)
        xp, xs = _moe(tok, w_router[l], b_router[l], w_gu[l], b_gu[l], w_down[l],
                      b_down[l], row2(ln3_g[l]), row2(ln3_b[l]), alpha, n_tok_p)
    return (xp.reshape(n_seq, seq_len, d), xs.reshape(n_dec, nq, d)) + tuple(jnp.stack(o) for o in outs)
```

```python
import functools

import jax
import jax.numpy as jnp
from jax import lax
from jax.experimental import pallas as pl
from jax.experimental.pallas import tpu as pltpu

F32 = jnp.float32
BF16 = jnp.bfloat16
HIGHEST = lax.Precision.HIGHEST

HEAD_DIM = 64
MOBA_BLOCK = 256
MOBA_TOP_K = 3
ROPE_THETA = 10000.0
MEM_HEADS = 4
TOP_K = 4
SWIGLU_LIMIT = 7.0
SWIGLU_ALPHA = 1.702
LN_EPS = 1e-5
RMS_EPS = 1e-6
ATT_SCALE = HEAD_DIM ** -0.5
NEG_INF = float("-inf")
SB_NEGLIGIBLE = -105.0

PROJ_ROWS = 256
MOBA_QUERIES = 256
SB_KEYS = 128
MOBA_HEADS = 4
SB_HEADS = 8
DECODE_PAGES = 8
TOKEN_ROWS = 512
EXPERT_ROWS = 256
COMBINE_ROWS = 128
NEW_PAD = 16
OUT_ROWS = 8
MEM_GROUP = 4
DMA_UNROLL = 8
VMEM_LIMIT = 56 * 1024 * 1024

_NT = (((1,), (1,)), ((), ()))


def _cparams(n_axes):
    return pltpu.CompilerParams(dimension_semantics=("arbitrary",) * n_axes,
                                vmem_limit_bytes=VMEM_LIMIT)


def _row_tile(n_rows):
    rows = TOKEN_ROWS
    while n_rows % rows:
        rows //= 2
    assert rows >= 8, n_rows
    return rows


def _idiv(x, n):
    return x >> (n.bit_length() - 1) if n & (n - 1) == 0 else x // n


def _imod(x, n):
    return x & (n - 1) if n & (n - 1) == 0 else x % n


def _layer_norm(x, g, b):
    mu = jnp.mean(x, axis=-1, keepdims=True)
    xc = x - mu
    var = jnp.mean(xc * xc, axis=-1, keepdims=True)
    return xc * lax.rsqrt(var + LN_EPS) * g + b


def _rms_norm(x, g):
    return x * lax.rsqrt(jnp.mean(x * x, axis=-1, keepdims=True) + RMS_EPS) * g


def _log_sigmoid_pair(z):
    t = jnp.log1p(jnp.exp(-jnp.abs(z)))
    return -(jnp.maximum(z, 0.0) + t), jnp.minimum(z, 0.0) - t


def _suffix_sums(xs, upper):
    rows = xs[0].shape[0]
    hi = [x.astype(BF16) for x in xs]
    lo = [(x - xh.astype(F32)).astype(BF16) for x, xh in zip(xs, hi)]
    s = jnp.dot(jnp.concatenate(hi + lo, axis=0), upper, preferred_element_type=F32)
    n = len(xs)
    return [s[i * rows:(i + 1) * rows] + s[(n + i) * rows:(n + i + 1) * rows] for i in range(n)]


def _strict_lower(n):
    r = lax.broadcasted_iota(jnp.int32, (n, n), 0)
    c = lax.broadcasted_iota(jnp.int32, (n, n), 1)
    return jnp.where(r > c, 1.0, 0.0).astype(BF16)


def _top_block_bias(gates, n_valid):
    nb = gates.shape[1]
    n_iota = lax.broadcasted_iota(jnp.int32, gates.shape, 1)
    valid = n_iota < n_valid
    g = jnp.where(valid, gates, NEG_INF)
    cnt = jnp.zeros(gates.shape, jnp.int32)
    for m in range(nb):
        gm = g[:, m:m + 1]
        beats = jnp.where(gm > g, 1, jnp.where(gm == g, jnp.where(n_iota > m, 1, 0), 0))
        cnt = cnt + beats
    sel = jnp.where(valid, jnp.where(cnt < MOBA_TOP_K, 1, 0), 0)
    return jnp.where(sel == 1, 0.0, NEG_INF)


def _top_block_select_t(gates_t, n_valid):
    nb = gates_t.shape[0]
    n_iota = lax.broadcasted_iota(jnp.int32, gates_t.shape, 0)
    valid = n_iota < n_valid
    g = jnp.where(valid, gates_t, NEG_INF)
    cnt = jnp.zeros(gates_t.shape, jnp.int32)
    for m in range(nb):
        gm = g[m:m + 1, :]
        cnt = cnt + jnp.where(gm > g, 1, jnp.where(gm == g, jnp.where(n_iota > m, 1, 0), 0))
    return jnp.where(valid, jnp.where(cnt < MOBA_TOP_K, 1.0, 0.0), 0.0)


def _columns(cols, width):
    lane = lax.broadcasted_iota(jnp.int32, (cols[0].shape[0], width), 1)
    out = jnp.zeros((cols[0].shape[0], width), F32)
    for n, c in enumerate(cols):
        out = jnp.where(lane == n, c, out)
    return out


def _qkv_kernel(x_ref, wq_ref, wkv_ref, cos_ref, sin_ref, cost_ref, sint_ref,
                q_ref, kt_ref, vt_ref, kt16_ref, vt16_ref, *, gw):
    half = HEAD_DIM // 2
    x16 = x_ref[...].astype(BF16)
    cos = cos_ref[...]
    sin = sin_ref[...]
    lane = lax.broadcasted_iota(jnp.int32, cos.shape, 1)
    first_half = (lane & (HEAD_DIM - 1)) < half
    qa = jnp.dot(x16, wq_ref[:, :gw], preferred_element_type=F32)
    rot = jnp.where(first_half, pltpu.roll(qa, gw - half, 1), pltpu.roll(qa, half, 1))
    q_ref[:, :gw] = qa * cos + rot * sin
    q_ref[:, gw:] = jnp.dot(x16, wq_ref[:, gw:], preferred_element_type=F32)

    def proj_t(g):
        return lax.dot_general(wkv_ref[g * gw:(g + 1) * gw, :], x16, _NT, preferred_element_type=F32)

    def put(ref, ref16, r0, val):
        ref[r0:r0 + val.shape[0], :] = val
        ref16[r0:r0 + val.shape[0], :] = val.astype(BF16)

    kat = proj_t(0)
    cost = cost_ref[...]
    sint = sint_ref[...]
    for h in range(gw // HEAD_DIM):
        x1 = kat[h * HEAD_DIM:h * HEAD_DIM + half]
        x2 = kat[h * HEAD_DIM + half:(h + 1) * HEAD_DIM]
        put(kt_ref, kt16_ref, h * HEAD_DIM, x1 * cost - x2 * sint)
        put(kt_ref, kt16_ref, h * HEAD_DIM + half, x2 * cost + x1 * sint)
    put(kt_ref, kt16_ref, gw, proj_t(1))
    put(vt_ref, vt16_ref, 0, proj_t(2))
    put(vt_ref, vt16_ref, gw, proj_t(3))


def _qkv_project(x, wq16, wkvt16, cos, sin, cost, sint, rows, seq_len):
    t, d = x.shape
    gw = wq16.shape[1] // 2
    tiles = seq_len // rows
    full = lambda a: pl.BlockSpec(a.shape, lambda i: (0, 0))
    row_spec = lambda width: pl.BlockSpec((rows, width), lambda i: (i, 0))
    t_spec = pl.BlockSpec((2 * gw, rows), lambda i: (i // tiles, i % tiles))
    t_shape = lambda dt: jax.ShapeDtypeStruct((t // seq_len * 2 * gw, seq_len), dt)
    return pl.pallas_call(
        functools.partial(_qkv_kernel, gw=gw),
        grid=(t // rows,),
        in_specs=[row_spec(d), full(wq16), full(wkvt16),
                  pl.BlockSpec((rows, gw), lambda i: (i % tiles, 0)),
                  pl.BlockSpec((rows, gw), lambda i: (i % tiles, 0)),
                  pl.BlockSpec((HEAD_DIM // 2, rows), lambda i: (0, i % tiles)),
                  pl.BlockSpec((HEAD_DIM // 2, rows), lambda i: (0, i % tiles))],
        out_specs=[row_spec(2 * gw), t_spec, t_spec, t_spec, t_spec],
        out_shape=[jax.ShapeDtypeStruct((t, 2 * gw), F32), t_shape(F32), t_shape(F32), t_shape(BF16), t_shape(BF16)],
        compiler_params=_cparams(1),
        name="qkv_rope",
    )(x, wq16, wkvt16, cos, sin, cost, sint)


def _moba_prompt_kernel(q_ref, kt_ref, vt_ref, ktf_ref, o_ref, kbar_sc):
    c = pl.program_id(2)
    own = c // (MOBA_BLOCK // MOBA_QUERIES)
    tq = q_ref.shape[0]
    heads = q_ref.shape[1] // HEAD_DIM
    nb = kt_ref.shape[1] // MOBA_BLOCK

    nbp = -(-nb // 8) * 8

    @pl.when(c == 0)
    def _block_means():
        kbar_sc[...] = _columns([jnp.mean(ktf_ref[:, n * MOBA_BLOCK:(n + 1) * MOBA_BLOCK], axis=1, keepdims=True)
                                 for n in range(nb)], kbar_sc.shape[0]).T

    row = lax.broadcasted_iota(jnp.int32, (tq, MOBA_BLOCK), 0)
    col = lax.broadcasted_iota(jnp.int32, (tq, MOBA_BLOCK), 1)
    causal = own * MOBA_BLOCK + col <= c * tq + row
    n_iota = lax.broadcasted_iota(jnp.int32, (tq, nbp), 1)
    hs = [slice(h * HEAD_DIM, (h + 1) * HEAD_DIM) for h in range(heads)]
    q16, bias = [], []
    for h in range(heads):
        q = q_ref[:, hs[h]]
        gates_t = lax.dot_general(kbar_sc[:nbp, hs[h]], q, _NT, precision=HIGHEST, preferred_element_type=F32)
        bias.append(jnp.where(_top_block_select_t(gates_t, own).T > 0.5, 0.0, NEG_INF))
        q16.append((q * ATT_SCALE).astype(BF16))

    def scores(j, h):
        keys = pl.ds(pl.multiple_of(j * MOBA_BLOCK, MOBA_BLOCK), MOBA_BLOCK)
        return jnp.dot(q16[h], kt_ref[hs[h], keys], preferred_element_type=F32), vt_ref[hs[h], keys]

    state = []
    for h in range(heads):
        s, vb = scores(own, h)
        s = jnp.where(causal, s, NEG_INF)
        m = jnp.max(s, axis=1, keepdims=True)
        p = jnp.exp(s - m)
        state.append((m, jnp.sum(p, axis=1, keepdims=True),
                      lax.dot_general(p.astype(BF16), vb, _NT, preferred_element_type=F32)))

    def body(j, state):
        sv = [scores(j, h) for h in range(heads)]
        s = [sv[h][0] + jnp.sum(jnp.where(n_iota == j, bias[h], 0.0), axis=1, keepdims=True) for h in range(heads)]
        m_new = [jnp.maximum(state[h][0], jnp.max(s[h], axis=1, keepdims=True)) for h in range(heads)]
        p = [jnp.exp(s[h] - m_new[h]) for h in range(heads)]
        new = []
        for h in range(heads):
            m, l, acc = state[h]
            alpha = jnp.exp(m - m_new[h])
            new.append((m_new[h], alpha * l + jnp.sum(p[h], axis=1, keepdims=True),
                        alpha * acc + lax.dot_general(p[h].astype(BF16), sv[h][1], _NT, preferred_element_type=F32)))
        return tuple(new)

    state = lax.fori_loop(0, own, body, tuple(state))
    for h in range(heads):
        o_ref[:, hs[h]] = state[h][2] / state[h][1]


def _moba_prompt(q, kt16, vt16, kt, n_seq, seq_len, gw):
    width = min(MOBA_HEADS * HEAD_DIM, gw)
    n_groups = gw // width
    nc = seq_len // MOBA_QUERIES
    kv_spec = pl.BlockSpec((width, seq_len), lambda b, hg, c: (b * 2 * n_groups + hg, 0))
    q_spec = pl.BlockSpec((MOBA_QUERIES, width), lambda b, hg, c: (b * nc + c, hg))
    return pl.pallas_call(
        _moba_prompt_kernel,
        grid=(n_seq, n_groups, nc),
        in_specs=[q_spec, kv_spec, kv_spec, kv_spec],
        out_specs=q_spec,
        out_shape=jax.ShapeDtypeStruct((n_seq * seq_len, gw), F32),
        scratch_shapes=[pltpu.VMEM((128, width), F32)],
        compiler_params=_cparams(3),
        name="moba_prompt",
    )(q, kt16, vt16, kt)


def _sb_prompt_kernel(q_ref, kt_ref, vt_ref, o_ref):
    c = pl.program_id(2)
    tq = q_ref.shape[0]
    heads = q_ref.shape[1] // HEAD_DIM
    upper = _strict_lower(SB_KEYS)
    row = lax.broadcasted_iota(jnp.int32, (tq, SB_KEYS), 0)
    col = lax.broadcasted_iota(jnp.int32, (tq, SB_KEYS), 1)
    strict = col < row
    hs = [slice(h * HEAD_DIM, (h + 1) * HEAD_DIM) for h in range(heads)]
    q16 = [(q_ref[:, hs[h]] * ATT_SCALE).astype(BF16) for h in range(heads)]

    def tile(j, state, mask):
        keys = pl.ds(pl.multiple_of(j * SB_KEYS, SB_KEYS), SB_KEYS)
        z = [jnp.dot(q16[h], kt_ref[hs[h], keys], preferred_element_type=F32) for h in range(heads)]
        pairs = [_log_sigmoid_pair(zh) for zh in z]
        lk = [pr[0] if mask is None else jnp.where(mask, pr[0], 0.0) for pr in pairs]
        after = _suffix_sums(lk, upper)
        new = []
        for h in range(heads):
            r, acc = state[h]
            a = jnp.exp(pairs[h][1] + after[h] + r)
            if mask is not None:
                a = jnp.where(mask, a, 0.0)
            acc = acc + lax.dot_general(a.astype(BF16), vt_ref[hs[h], keys], _NT, preferred_element_type=F32)
            new.append((r + jnp.sum(lk[h], axis=1, keepdims=True), acc))
        return tuple(new)

    state = tile(c, tuple((jnp.zeros((tq, 1), F32), jnp.zeros((tq, HEAD_DIM), F32)) for _ in range(heads)), strict)

    def live(state):
        r_max = functools.reduce(jnp.maximum, [st[0] for st in state])
        return (jnp.max(r_max) > SB_NEGLIGIBLE).astype(jnp.int32)

    def cond(carry):
        i, alive, _ = carry
        return jnp.logical_and(i < c, alive > 0)

    def body(carry):
        i, _, state = carry
        state = tile(c - 1 - i, state, None)
        return i + 1, live(state), state

    _, _, state = lax.while_loop(cond, body, (jnp.int32(0), live(state), state))
    for h in range(heads):
        o_ref[:, hs[h]] = state[h][1]


def _sb_prompt(q, kt16, vt16, n_seq, seq_len, gw):
    width = min(SB_HEADS * HEAD_DIM, gw)
    n_groups = gw // width
    nc = seq_len // SB_KEYS
    kv_spec = pl.BlockSpec((width, seq_len), lambda b, hg, c: (b * 2 * n_groups + n_groups + hg, 0))
    return pl.pallas_call(
        _sb_prompt_kernel,
        grid=(n_seq, n_groups, nc),
        in_specs=[pl.BlockSpec((SB_KEYS, width), lambda b, hg, c: (b * nc + c, n_groups + hg)), kv_spec, kv_spec],
        out_specs=pl.BlockSpec((SB_KEYS, width), lambda b, hg, c: (b * nc + c, hg)),
        out_shape=jax.ShapeDtypeStruct((n_seq * seq_len, gw), F32),
        compiler_params=_cparams(3),
        name="sb_prompt",
    )(q, kt16, vt16)


def _decode_kernel(pt_ref, q_ref, knew_ref, vnew_ref, *refs, n_pages, nq):
    del pt_ref
    kv_refs = refs[:2 * DECODE_PAGES]
    oa_ref, ob_ref, qbd_sc, g_sc, m_sc, l_sc, o_sc, r_sc, accb_sc = refs[2 * DECODE_PAGES:]
    p = pl.program_id(1)
    n_steps = n_pages // DECODE_PAGES
    rows = qbd_sc.shape[0] // 2
    heads, hd, page = kv_refs[0].shape[1:]
    gh = heads // 2
    gw = gh * hd
    page_refs = [(kv_refs[2 * j + t // 2], t % 2) for j in range(DECODE_PAGES) for t in range(4)]
    pn = knew_ref.shape[2]
    pages_per_block = MOBA_BLOCK // page
    blocks_per_step = DECODE_PAGES // pages_per_block
    n_blocks = n_pages // pages_per_block
    lane_w = m_sc.shape[2]

    @pl.when(p == 0)
    def _block_diagonal_queries():
        shape = qbd_sc.shape
        rr = lax.broadcasted_iota(jnp.int32, shape, 0)
        cc = lax.broadcasted_iota(jnp.int32, shape, 1)
        sr = lax.broadcasted_iota(jnp.int32, (shape[0], q_ref.shape[1]), 0)
        si = lax.broadcasted_iota(jnp.int32, (shape[0], q_ref.shape[1]), 1)
        spread = jnp.dot(jnp.where(_imod(sr, nq) == si, 1.0, 0.0), q_ref[0], precision=HIGHEST,
                         preferred_element_type=F32)
        qbd_sc[...] = (jnp.where(_idiv(rr, nq) == _idiv(cc, hd), spread, 0.0) * ATT_SCALE).astype(BF16)

    qa16 = qbd_sc[:rows, :gw]
    qb16 = qbd_sc[rows:, gw:]
    new_query = _imod(lax.broadcasted_iota(jnp.int32, (rows, pn), 0), nq)
    new_key = lax.broadcasted_iota(jnp.int32, (rows, pn), 1)

    def mat(ref_group):
        ref, g = ref_group
        return ref[0, g * gh:(g + 1) * gh].reshape(gw, page).astype(BF16)

    @pl.when(p == 0)
    def _init():
        z = lax.dot_general(qb16, knew_ref[0, 1].astype(BF16), _NT, preferred_element_type=F32)
        mask = new_key < new_query
        lk, ls = _log_sigmoid_pair(z)
        lk = jnp.where(mask, lk, 0.0)
        a = jnp.where(mask, jnp.exp(ls + _suffix_sums([lk], _strict_lower(pn))[0]), 0.0)
        accb_sc[...] = jnp.dot(a.astype(BF16), vnew_ref[0, 1].astype(BF16), preferred_element_type=F32)
        r_sc[...] = jnp.broadcast_to(jnp.sum(lk, axis=1, keepdims=True), r_sc.shape)

    all_pages = range(DECODE_PAGES)
    blocks = [range(b * pages_per_block, (b + 1) * pages_per_block) for b in range(blocks_per_step)]
    block_of = lambda vals, op: [functools.reduce(op, [vals[j] for j in pages]) for pages in blocks]
    s = [jnp.dot(qa16, mat(page_refs[4 * j]), preferred_element_type=F32) for j in all_pages]
    z = [jnp.dot(qb16, mat(page_refs[4 * j + 1]), preferred_element_type=F32) for j in all_pages]
    m_blk = block_of([jnp.max(sj, axis=1, keepdims=True) for sj in s], jnp.maximum)
    g_blk = block_of([jnp.sum(sj, axis=1, keepdims=True) for sj in s], jnp.add)
    pr = [jnp.exp(s[j] - m_blk[j // pages_per_block]) for j in all_pages]
    l_blk = block_of([jnp.sum(pj, axis=1, keepdims=True) for pj in pr], jnp.add)
    pv = [lax.dot_general(pr[j].astype(BF16), mat(page_refs[4 * j + 2]), _NT, preferred_element_type=F32)
          for j in all_pages]
    o_blk = block_of(pv, jnp.add)
    for b in range(blocks_per_step):
        n = n_blocks - 1 - (p * blocks_per_step + b)
        m_sc[n] = jnp.broadcast_to(m_blk[b], (rows, lane_w))
        l_sc[n] = jnp.broadcast_to(l_blk[b], (rows, lane_w))
        g_sc[n] = jnp.broadcast_to(g_blk[b], (rows, lane_w))
        o_sc[n] = o_blk[b]

    pairs = [_log_sigmoid_pair(zj) for zj in z]
    after = _suffix_sums([pr_[0] for pr_ in pairs], _strict_lower(page))
    lk_sum = [jnp.sum(pr_[0], axis=1, keepdims=True) for pr_ in pairs]
    r = [r_sc[:, :1]]
    for j in all_pages:
        r.append(r[j] + lk_sum[j])
    a = [jnp.exp(pairs[j][1] + after[j] + r[j]).astype(BF16) for j in all_pages]
    accb_sc[...] = functools.reduce(jnp.add, [accb_sc[...]] + [
        lax.dot_general(a[j], mat(page_refs[4 * j + 3]), _NT, preferred_element_type=F32) for j in all_pages])
    r_sc[...] = jnp.broadcast_to(r[DECODE_PAGES], r_sc.shape)

    @pl.when(p == n_steps - 1)
    def _finish():
        gates = _columns([g_sc[b][:, :1] for b in range(n_blocks)], n_blocks)
        bias = _top_block_bias(gates, n_blocks)
        sn = lax.dot_general(qa16, knew_ref[0, 0].astype(BF16), _NT, preferred_element_type=F32)
        sn = jnp.where(new_key <= new_query, sn, NEG_INF)
        m_own = jnp.max(sn, axis=1, keepdims=True)
        pn_ = jnp.exp(sn - m_own)
        l_own = jnp.sum(pn_, axis=1, keepdims=True)
        o_own = jnp.dot(pn_.astype(BF16), vnew_ref[0, 0].astype(BF16), preferred_element_type=F32)
        m_all = m_own
        for b in range(n_blocks):
            m_all = jnp.maximum(m_all, m_sc[b][:, :1] + bias[:, b:b + 1])
        w_own = jnp.exp(m_own - m_all)
        num = w_own * o_own
        den = w_own * l_own
        for b in range(n_blocks):
            w = jnp.exp(m_sc[b][:, :1] + bias[:, b:b + 1] - m_all)
            num = num + w * o_sc[b]
            den = den + w * l_sc[b][:, :1]
        outa = num / den
        rr = lax.broadcasted_iota(jnp.int32, (rows, gw), 0)
        cc = lax.broadcasted_iota(jnp.int32, (rows, gw), 1)
        diag = _idiv(rr, nq) == _idiv(cc, hd)
        si = lax.broadcasted_iota(jnp.int32, (oa_ref.shape[1], rows), 0)
        sr = lax.broadcasted_iota(jnp.int32, (oa_ref.shape[1], rows), 1)
        pick = jnp.where(_imod(sr, nq) == si, 1.0, 0.0)
        oa_ref[0] = jnp.dot(pick, jnp.where(diag, outa, 0.0), precision=HIGHEST, preferred_element_type=F32)
        ob_ref[0] = jnp.dot(pick, jnp.where(diag, accb_sc[...], 0.0), precision=HIGHEST,
                            preferred_element_type=F32)


def _decode_attention(page_table, q, knew, vnew, cache_kt, cache_vt, nq):
    n_dec, n_pages = page_table.shape
    _, heads, hd, page = cache_kt.shape
    gh = heads // 2
    gw = gh * hd
    rows = gh * nq
    n_blocks = n_pages * page // MOBA_BLOCK
    per_seq = lambda a: pl.BlockSpec((1,) + a.shape[1:], lambda b, p, pt: (b,) + (0,) * (a.ndim - 1))
    page_spec = lambda j: pl.BlockSpec(
        (1, heads, hd, page), lambda b, p, pt: (pt[b, n_pages - 1 - (p * DECODE_PAGES + j)], 0, 0, 0))
    out_spec = pl.BlockSpec((1, OUT_ROWS, gw), lambda b, p, pt: (b, 0, 0))
    assert n_pages % DECODE_PAGES == 0 and DECODE_PAGES % (MOBA_BLOCK // page) == 0 and nq <= OUT_ROWS
    grid_spec = pltpu.PrefetchScalarGridSpec(
        num_scalar_prefetch=1,
        grid=(n_dec, n_pages // DECODE_PAGES),
        in_specs=[per_seq(q), per_seq(knew), per_seq(vnew)]
        + [page_spec(j) for j in range(DECODE_PAGES) for _ in range(2)],
        out_specs=[out_spec, out_spec],
        scratch_shapes=[pltpu.VMEM((2 * rows, 2 * gw), BF16),
                        pltpu.VMEM((n_blocks, rows, 128), F32),
                        pltpu.VMEM((n_blocks, rows, 128), F32),
                        pltpu.VMEM((n_blocks, rows, 128), F32),
                        pltpu.VMEM((n_blocks, rows, gw), F32),
                        pltpu.VMEM((rows, 128), F32),
                        pltpu.VMEM((rows, gw), F32)],
    )
    return pl.pallas_call(
        functools.partial(_decode_kernel, n_pages=n_pages, nq=nq),
        grid_spec=grid_spec,
        out_shape=[jax.ShapeDtypeStruct((n_dec, OUT_ROWS, gw), F32)] * 2,
        compiler_params=_cparams(2),
        name="decode_attention",
    )(page_table, q, knew, vnew, *([cache_kt, cache_vt] * DECODE_PAGES))


def _merge_kernel(oa_ref, ob_ref, x_ref, ga_ref, gb_ref, w_ref, g_ref, b_ref, y_ref, *, alpha):
    gw = oa_ref.shape[1]
    ya = _rms_norm(oa_ref[...], ga_ref[...]).astype(BF16)
    yb = _rms_norm(ob_ref[...], gb_ref[...]).astype(BF16)
    mix = (jnp.dot(ya, w_ref[:gw, :], preferred_element_type=F32)
           + jnp.dot(yb, w_ref[gw:, :], preferred_element_type=F32))
    y_ref[...] = _layer_norm(alpha * x_ref[...] + mix, g_ref[...], b_ref[...])


def _merge(oa, ob, x, g_a, g_b, w16, ln_g, ln_b, alpha):
    t, d = x.shape
    gw = oa.shape[1]
    rows = _row_tile(t)
    row = lambda width: pl.BlockSpec((rows, width), lambda i: (i, 0))
    full = lambda a: pl.BlockSpec(a.shape, lambda i: (0, 0))
    return pl.pallas_call(
        functools.partial(_merge_kernel, alpha=alpha),
        grid=(t // rows,),
        in_specs=[row(gw), row(gw), row(d), full(g_a), full(g_b), full(w16), full(ln_g), full(ln_b)],
        out_specs=row(d),
        out_shape=jax.ShapeDtypeStruct((t, d), F32),
        compiler_params=_cparams(1),
        name="merge_out_proj",
    )(oa, ob, x, g_a, g_b, w16, ln_g, ln_b)


def _mem_kv_kernel(m_ref, w_ref, k_ref, v_ref, k16_ref, v16_ref):
    width = k_ref.shape[1]
    m16 = m_ref[...].astype(BF16)
    k = jnp.dot(m16, w_ref[:, :width], preferred_element_type=F32)
    v = jnp.dot(m16, w_ref[:, width:], preferred_element_type=F32)
    k_ref[...] = k
    v_ref[...] = v
    k16_ref[...] = k.astype(BF16)
    v16_ref[...] = v.astype(BF16)


def _mem_kv(mem, w16):
    t, d = mem.shape
    width = w16.shape[1] // 2
    rows = _row_tile(t)
    row = lambda w: pl.BlockSpec((rows, w), lambda i: (i, 0))
    return pl.pallas_call(
        _mem_kv_kernel,
        grid=(t // rows,),
        in_specs=[row(d), pl.BlockSpec(w16.shape, lambda i: (0, 0))],
        out_specs=[row(width)] * 4,
        out_shape=[jax.ShapeDtypeStruct((t, width), F32)] * 2 + [jax.ShapeDtypeStruct((t, width), BF16)] * 2,
        compiler_params=_cparams(1),
        name="mem_kv",
    )(mem, w16)


def _mem_heads(q16, head_kv, row_mask=None):
    hd = q16.shape[1] // MEM_HEADS
    outs = []
    for h in range(MEM_HEADS):
        k16, v16 = head_kv(h)
        s = lax.dot_general(q16[:, h * hd:(h + 1) * hd], k16, _NT, preferred_element_type=F32)
        m = jnp.max(s, axis=1, keepdims=True)
        p = jnp.exp(s - m)
        l = jnp.sum(p, axis=1, keepdims=True)
        o = jnp.dot(p.astype(BF16), v16, preferred_element_type=F32) / l
        outs.append(o if row_mask is None else jnp.where(row_mask, o, 0.0))
    return jnp.concatenate(outs, axis=1)


def _mem_attend_kernel(x_ref, wq_ref, mk_ref, mv_ref, wo_ref, g_ref, b_ref, y_ref, *, alpha, n_tiles):
    @pl.when(pl.program_id(0) < n_tiles)
    def _tile():
        x = x_ref[...]
        hd = x.shape[1] // MEM_HEADS
        q16 = (jnp.dot(x.astype(BF16), wq_ref[...], preferred_element_type=F32) * hd ** -0.5).astype(BF16)
        o = _mem_heads(q16, lambda h: (mk_ref[:, h * hd:(h + 1) * hd], mv_ref[:, h * hd:(h + 1) * hd]))
        y = jnp.dot(o.astype(BF16), wo_ref[...], preferred_element_type=F32)
        y_ref[...] = _layer_norm(alpha * x + y, g_ref[...], b_ref[...])

    @pl.when(pl.program_id(0) >= n_tiles)
    def _tail():
        y_ref[...] = jnp.zeros(y_ref.shape, F32)


def _mem_attend_prompt(x, wq16, mk16, mv16, wo16, ln_g, ln_b, n_seq, alpha, out_rows):
    t, d = x.shape
    seq_len = t // n_seq
    n_mem = mk16.shape[0] // n_seq
    rows = _row_tile(seq_len)
    tiles = seq_len // rows
    n_tiles = n_seq * tiles
    assert (out_rows - t) % rows == 0
    last = lambda i: jnp.minimum(i, n_tiles - 1)
    full = lambda a: pl.BlockSpec(a.shape, lambda i: (0, 0))
    mem = pl.BlockSpec((n_mem, d), lambda i: (last(i) // tiles, 0))
    return pl.pallas_call(
        functools.partial(_mem_attend_kernel, alpha=alpha, n_tiles=n_tiles),
        grid=(out_rows // rows,),
        in_specs=[pl.BlockSpec((rows, d), lambda i: (last(i), 0)), full(wq16), mem, mem, full(wo16),
                  full(ln_g), full(ln_b)],
        out_specs=pl.BlockSpec((rows, d), lambda i: (i, 0)),
        out_shape=jax.ShapeDtypeStruct((out_rows, d), F32),
        compiler_params=_cparams(1),
        name="mem_attend_prompt",
    )(x, wq16, mk16, mv16, wo16, ln_g, ln_b)


def _mem_attend_sample_kernel(x_ref, wq_ref, mk_ref, mv_ref, wo_ref, g_ref, b_ref, all_ref, y_ref, *, alpha, nq):
    del all_ref
    x = x_ref[...]
    hd = x.shape[1] // MEM_HEADS
    q16 = (jnp.dot(x.astype(BF16), wq_ref[...], preferred_element_type=F32) * hd ** -0.5).astype(BF16)
    seq_of_row = _idiv(lax.broadcasted_iota(jnp.int32, (x.shape[0], 1), 0), nq)
    o = jnp.zeros(x.shape, F32)
    for g in range(mk_ref.shape[0]):
        head_kv = lambda h, g=g: (mk_ref[g, :, h * hd:(h + 1) * hd].astype(BF16),
                                  mv_ref[g, :, h * hd:(h + 1) * hd].astype(BF16))
        o = o + _mem_heads(q16, head_kv, seq_of_row == g)
    y = jnp.dot(o.astype(BF16), wo_ref[...], preferred_element_type=F32)
    y_ref[...] = _layer_norm(alpha * x + y, g_ref[...], b_ref[...])


def _mem_attend_sample(x, wq16, cache_mk, cache_mv, wo16, ln_g, ln_b, nq, alpha, all_rows):
    t, d = x.shape
    n_dec, n_mem, _ = cache_mk.shape
    rows = MEM_GROUP * nq
    base = (all_rows.shape[0] - t) // rows
    assert (all_rows.shape[0] - t) % rows == 0
    row = pl.BlockSpec((rows, d), lambda i: (i, 0))
    full = lambda a: pl.BlockSpec(a.shape, lambda i: (0, 0))
    mem = pl.BlockSpec((MEM_GROUP, n_mem, d), lambda i: (i, 0, 0))
    return pl.pallas_call(
        functools.partial(_mem_attend_sample_kernel, alpha=alpha, nq=nq),
        grid=(n_dec // MEM_GROUP,),
        in_specs=[row, full(wq16), mem, mem, full(wo16), full(ln_g), full(ln_b), pl.BlockSpec(memory_space=pl.ANY)],
        out_specs=pl.BlockSpec((rows, d), lambda i: (base + i, 0)),
        out_shape=jax.ShapeDtypeStruct(all_rows.shape, F32),
        input_output_aliases={7: 0},
        compiler_params=_cparams(1),
        name="mem_attend_sample",
    )(x, wq16, cache_mk, cache_mv, wo16, ln_g, ln_b, all_rows)


def _router_kernel(x_ref, w_ref, b_ref, e_ref, g_ref, rank_ref, cnt_ref, run_sc):
    rows = x_ref.shape[0]
    logits = jnp.dot(x_ref[...], w_ref[...], precision=HIGHEST, preferred_element_type=F32) + b_ref[...]
    n_exp = logits.shape[1]
    e_iota = lax.broadcasted_iota(jnp.int32, logits.shape, 1)
    k_iota = lax.broadcasted_iota(jnp.int32, e_ref.shape, 1)
    top_e = jnp.zeros(e_ref.shape, jnp.int32)
    top_v = jnp.zeros(e_ref.shape, F32)
    onehot = []
    for k in range(TOP_K):
        mx = jnp.max(logits, axis=1, keepdims=True)
        idx = jnp.min(jnp.where(logits == mx, e_iota, n_exp), axis=1, keepdims=True)
        top_e = jnp.where(k_iota == k, idx, top_e)
        top_v = jnp.where(k_iota == k, mx, top_v)
        onehot.append(jnp.where(e_iota == idx, 1.0, 0.0))
        logits = jnp.where(e_iota == idx, NEG_INF, logits)
    w = jnp.exp(top_v - top_v[:, :1])
    e_ref[...] = top_e
    g_ref[...] = w / jnp.sum(w, axis=1, keepdims=True)

    @pl.when(pl.program_id(0) == 0)
    def _zero():
        run_sc[...] = jnp.zeros(run_sc.shape, F32)

    chosen = functools.reduce(jnp.add, onehot)
    before = jnp.dot(_strict_lower(rows), chosen.astype(BF16), preferred_element_type=F32) + run_sc[...]
    rank = jnp.zeros(e_ref.shape, F32)
    for k in range(TOP_K):
        rank = jnp.where(k_iota == k, jnp.sum(onehot[k] * before, axis=1, keepdims=True), rank)
    rank_ref[...] = rank.astype(jnp.int32)
    run_sc[...] = run_sc[...] + jnp.sum(chosen, axis=0, keepdims=True)
    cnt_ref[...] = run_sc[...]


def _router(x, w_router, b_router):
    t, d = x.shape
    n_exp = w_router.shape[1]
    rows = _row_tile(t)
    out = pl.BlockSpec((rows, TOP_K), lambda i: (i, 0))
    return pl.pallas_call(
        _router_kernel,
        grid=(t // rows,),
        in_specs=[pl.BlockSpec((rows, d), lambda i: (i, 0)),
                  pl.BlockSpec(w_router.shape, lambda i: (0, 0)),
                  pl.BlockSpec(b_router.shape, lambda i: (0, 0))],
        out_specs=[out, out, out, pl.BlockSpec((1, n_exp), lambda i: (0, 0))],
        out_shape=[jax.ShapeDtypeStruct((t, TOP_K), jnp.int32), jax.ShapeDtypeStruct((t, TOP_K), F32),
                   jax.ShapeDtypeStruct((t, TOP_K), jnp.int32), jax.ShapeDtypeStruct((1, n_exp), F32)],
        scratch_shapes=[pltpu.VMEM((1, n_exp), F32)],
        compiler_params=_cparams(1),
        name="router_top4",
    )(x, w_router, b_router)


def _row_copy(src_hbm, src_row, dst, r, sem):
    return pltpu.make_async_copy(src_hbm.at[pl.ds(src_row, 1)], dst.at[pl.ds(r, 1)], sem)


def _row_gather(src_hbm, idx_ref, dst, sem, n_rows):
    for r in range(n_rows):
        _row_copy(src_hbm, idx_ref[0, 0, r], dst, r, sem).start(priority=r % 2)


def _row_gather_wait(src_hbm, dst, sem, n_rows):
    def wait(r, carry):
        _row_copy(src_hbm, 0, dst, r, sem).wait()
        return carry
    lax.fori_loop(0, n_rows, wait, 0, unroll=DMA_UNROLL)


def _dispatch_kernel(idx_ref, x_ref, rows_in, rows_out, stage, sems):
    del rows_in
    s = pl.program_id(0)
    n_steps = pl.num_programs(0)
    rows = x_ref.shape[0]
    slot = s % 2

    def copy(buf, t, dst_row):
        return pltpu.make_async_copy(stage.at[buf, pl.ds(t, 1)], rows_out.at[pl.ds(dst_row, 1)], sems.at[buf])

    def drain(buf):
        def wait(r, carry):
            copy(buf, 0, 0).wait()
            return carry
        lax.fori_loop(0, TOP_K * rows, wait, 0, unroll=DMA_UNROLL)

    @pl.when(s >= 2)
    def _reuse():
        drain(slot)

    stage[slot] = x_ref[...]
    for k in range(TOP_K):
        for t in range(rows):
            copy(slot, t, idx_ref[0, 0, k * rows + t]).start(priority=t % 2)

    @pl.when(s == n_steps - 1)
    def _finish():
        @pl.when(s >= 1)
        def _other():
            drain(1 - slot)
        drain(slot)


def _dispatch(x, dest, n_rows):
    t, d = x.shape
    n_steps, _, n_copy = dest.shape
    rows = n_copy // TOP_K
    return pl.pallas_call(
        _dispatch_kernel,
        grid=(n_steps,),
        in_specs=[pl.BlockSpec((1, 1, n_copy), lambda s: (s, 0, 0), memory_space=pltpu.SMEM),
                  pl.BlockSpec((rows, d), lambda s: (s, 0)),
                  pl.BlockSpec(memory_space=pl.ANY)],
        out_specs=pl.BlockSpec(memory_space=pl.ANY),
        out_shape=jax.ShapeDtypeStruct((n_rows, d), F32),
        input_output_aliases={2: 0},
        scratch_shapes=[pltpu.VMEM((2, rows, d), F32), pltpu.SemaphoreType.DMA((2,))],
        compiler_params=_cparams(1),
        name="moe_dispatch",
    )(dest, x, jnp.zeros((n_rows, d), F32))


def _expert_ffn_kernel(be_ref, nu_ref, x_ref, wgu_ref, bgu_ref, wd_ref, bd_ref, y_ref, wgu16, wd16):
    s = pl.program_id(0)
    active = s < nu_ref[0]
    new_expert = jnp.logical_or(s == 0, be_ref[s] != be_ref[jnp.maximum(s - 1, 0)])

    @pl.when(jnp.logical_and(active, new_expert))
    def _cast_weights():
        wgu16[...] = wgu_ref[0].astype(BF16)
        wd16[...] = wd_ref[0].astype(BF16)

    @pl.when(active)
    def _compute():
        f = wd_ref.shape[1]
        hgu = jnp.dot(x_ref[...].astype(BF16), wgu16[...], preferred_element_type=F32) + bgu_ref[0]
        gate = jnp.minimum(hgu[:, :f], SWIGLU_LIMIT)
        up = jnp.clip(hgu[:, f:], -SWIGLU_LIMIT, SWIGLU_LIMIT)
        act = (up + 1.0) * gate * jax.nn.sigmoid(SWIGLU_ALPHA * gate)
        y_ref[...] = jnp.dot(act.astype(BF16), wd16[...], preferred_element_type=F32) + bd_ref[0]

    @pl.when(jnp.logical_not(active))
    def _unused():
        y_ref[...] = jnp.zeros(y_ref.shape, F32)


def _expert_ffn(x_rows, block_e, n_used, w_gu, b_gu, w_down, b_down, rows):
    n_exp, d, f2 = w_gu.shape
    f = f2 // 2
    n_blk = x_rows.shape[0] // rows
    by_expert = lambda shape: pl.BlockSpec(shape, lambda s, be, nu: (be[s], 0, 0))
    row_spec = pl.BlockSpec((rows, d), lambda s, be, nu: (s, 0))
    grid_spec = pltpu.PrefetchScalarGridSpec(
        num_scalar_prefetch=2,
        grid=(n_blk,),
        in_specs=[row_spec, by_expert((1, d, f2)), by_expert((1, 1, f2)), by_expert((1, f, d)),
                  by_expert((1, 1, d))],
        out_specs=row_spec,
        scratch_shapes=[pltpu.VMEM((d, f2), BF16), pltpu.VMEM((f, d), BF16)],
    )
    return pl.pallas_call(
        _expert_ffn_kernel,
        grid_spec=grid_spec,
        out_shape=jax.ShapeDtypeStruct((n_blk * rows, d), F32),
        compiler_params=_cparams(1),
        name="expert_ffn",
    )(block_e, n_used, x_rows, w_gu, b_gu.reshape(n_exp, 1, f2), w_down, b_down.reshape(n_exp, 1, d))


def _combine_kernel(idx_ref, y_hbm, x_ref, gate_ref, g_ref, b_ref, oa_ref, ob_ref, ybuf, sems, *, alpha, a_tiles):
    s = pl.program_id(0)
    n_steps = pl.num_programs(0) - 1
    rows = x_ref.shape[0]
    n_copy = TOP_K * rows

    @pl.when(s < n_steps)
    def _fetch():
        _row_gather(y_hbm, idx_ref, ybuf.at[s % 2], sems.at[s % 2], n_copy)

    @pl.when(s >= 1)
    def _compute():
        slot = (s - 1) % 2
        _row_gather_wait(y_hbm, ybuf.at[slot], sems.at[slot], n_copy)
        gates = gate_ref[...]
        ffn = jnp.zeros(x_ref.shape, F32)
        for k in range(TOP_K):
            ffn = ffn + gates[:, k:k + 1] * ybuf[slot, k * rows:(k + 1) * rows, :]
        out = _layer_norm(alpha * x_ref[...] + ffn, g_ref[...], b_ref[...])

        @pl.when(s - 1 < a_tiles)
        def _first_group():
            oa_ref[...] = out

        @pl.when(s - 1 >= a_tiles)
        def _second_group():
            ob_ref[...] = out


def _combine(y_rows, dest, x, gates, ln_g, ln_b, alpha, n_first):
    t, d = x.shape
    n_steps, _, n_copy = dest.shape
    rows = n_copy // TOP_K
    a_tiles = n_first // rows
    assert n_first % rows == 0 and 0 < a_tiles < n_steps
    idx_spec = pl.BlockSpec((1, 1, n_copy), lambda s: (jnp.minimum(s, n_steps - 1), 0, 0),
                            memory_space=pltpu.SMEM)
    row = lambda w: pl.BlockSpec((rows, w), lambda s: (jnp.maximum(s - 1, 0), 0))
    full = lambda a: pl.BlockSpec(a.shape, lambda s: (0, 0))
    return pl.pallas_call(
        functools.partial(_combine_kernel, alpha=alpha, a_tiles=a_tiles),
        grid=(n_steps + 1,),
        in_specs=[idx_spec, pl.BlockSpec(memory_space=pl.ANY), row(d), row(TOP_K), full(ln_g), full(ln_b)],
        out_specs=[pl.BlockSpec((rows, d), lambda s: (jnp.clip(s - 1, 0, a_tiles - 1), 0)),
                   pl.BlockSpec((rows, d), lambda s: (jnp.maximum(s - 1 - a_tiles, 0), 0))],
        out_shape=[jax.ShapeDtypeStruct((n_first, d), F32), jax.ShapeDtypeStruct((t - n_first, d), F32)],
        scratch_shapes=[pltpu.VMEM((2, n_copy, d), F32), pltpu.SemaphoreType.DMA((2,))],
        compiler_params=_cparams(1),
        name="moe_combine",
    )(dest, y_rows, x, gates, ln_g, ln_b)


def _dispatch_plan(top_e, rank, counts, rows):
    n_assign = top_e.size
    n_exp = counts.shape[0]
    flat_e = top_e.reshape(n_assign)
    padded = (counts + rows - 1) // rows * rows
    pad_end = jnp.cumsum(padded)
    pad_start = pad_end - padded
    dest = (pad_start[flat_e] + rank.reshape(n_assign)).astype(jnp.int32)
    n_blk = -(-(n_assign + n_exp * (rows - 1)) // rows)
    block_start = jnp.arange(n_blk, dtype=jnp.int32) * rows
    block_e = jnp.minimum(jnp.sum((pad_end[None, :] <= block_start[:, None]).astype(jnp.int32), axis=1), n_exp - 1)
    n_used = (pad_end[-1] // rows).astype(jnp.int32).reshape(1)
    return dest, block_e, n_used, n_blk


def _moe(tok, w_router, b_router, w_gu, b_gu, w_down, b_down, ln_g, ln_b, alpha, n_first):
    t, d = tok.shape
    n_exp = w_router.shape[1]
    top_e, gates, rank, counts = _router(tok, w_router, b_router.reshape(1, n_exp))
    dest, block_e, n_used, n_blk = _dispatch_plan(top_e, rank, counts[0].astype(jnp.int32), EXPERT_ROWS)
    steps = t // COMBINE_ROWS
    dest_steps = dest.reshape(steps, COMBINE_ROWS, TOP_K).transpose(0, 2, 1).reshape(steps, 1, TOP_K * COMBINE_ROWS)
    x_rows = _dispatch(tok, dest_steps, n_blk * EXPERT_ROWS)
    y_rows = _expert_ffn(x_rows, block_e, n_used, w_gu, b_gu, w_down, b_down, EXPERT_ROWS)
    return _combine(y_rows, dest_steps, tok, gates, ln_g, ln_b, alpha, n_first)


def _rope_tables(pos, n_heads):
    half = HEAD_DIM // 2
    inv_freq = ROPE_THETA ** (-jnp.arange(half, dtype=F32) / half)
    ang = pos.astype(F32)[:, None] * inv_freq[None, :]
    cos = jnp.cos(ang)
    sin = jnp.sin(ang)
    return (jnp.tile(jnp.concatenate([cos, cos], axis=1), (1, n_heads)),
            jnp.tile(jnp.concatenate([-sin, sin], axis=1), (1, n_heads)), cos.T, sin.T)


def kernel(x_prompt, x_sample, mem_prompt, cache_k, cache_v, cache_mem_k, cache_mem_v, page_table,
           w_in, g_moba, g_sb, w_out, ln1_g, ln1_b, w_mq, w_mkv, w_mo, ln2_g, ln2_b,
           w_router, b_router, w_gu, b_gu, w_down, b_down, ln3_g, ln3_b):
    n_seq, seq_len, d = x_prompt.shape
    n_dec, nq, _ = x_sample.shape
    depth = w_in.shape[0]
    gw = w_in.shape[2] // 6
    g_heads = gw // HEAD_DIM
    heads = 2 * g_heads
    n_pages = page_table.shape[1]
    page = cache_k.shape[2]
    past_len = n_pages * page
    n_mem = mem_prompt.shape[1]
    alpha = (2 * depth) ** 0.25
    n_tok_p = n_seq * seq_len
    n_tok_s = n_dec * nq
    assert seq_len % MOBA_BLOCK == 0 and past_len % MOBA_BLOCK == 0 and MOBA_BLOCK % page == 0
    assert nq <= NEW_PAD and n_dec % MEM_GROUP == 0
    assert (n_tok_p + n_tok_s) % COMBINE_ROWS == 0 and n_tok_s % PROJ_ROWS == 0 and seq_len % PROJ_ROWS == 0

    tables_p = _rope_tables(jnp.arange(seq_len), g_heads)
    tables_s = _rope_tables(jnp.tile(past_len + jnp.arange(nq), n_dec), g_heads)
    xp = x_prompt.reshape(n_tok_p, d)
    xs = x_sample.reshape(n_tok_s, d)
    row2 = lambda a: a.reshape(1, -1)

    def to_positions(at, n, length):
        return at.reshape(n, heads, HEAD_DIM, length).transpose(0, 3, 1, 2)

    def new_rows(at):
        a = at.reshape(2, gw, n_dec, nq).transpose(2, 0, 3, 1)
        return jnp.pad(a, ((0, 0), (0, 0), (0, NEW_PAD - nq), (0, 0)))

    outs = [[] for _ in range(6)]
    for l in range(depth):
        w = w_in[l].astype(BF16)
        col = lambda g: w[:, g * gw:(g + 1) * gw]
        wq16 = jnp.concatenate([col(0), col(3)], axis=1)
        wkvt16 = jnp.concatenate([col(1), col(4), col(2), col(5)], axis=1).T
        w_out16 = w_out[l].astype(BF16)
        q_p, kt_p, vt_p, kt16_p, vt16_p = _qkv_project(xp, wq16, wkvt16, *tables_p, PROJ_ROWS, seq_len)
        oa_p = _moba_prompt(q_p, kt16_p, vt16_p, kt_p, n_seq, seq_len, gw)
        ob_p = _sb_prompt(q_p, kt16_p, vt16_p, n_seq, seq_len, gw)
        outs[0].append(to_positions(kt_p, n_seq, seq_len))
        outs[1].append(to_positions(vt_p, n_seq, seq_len))
        q_s, kt_s, vt_s, _, _ = _qkv_project(xs, wq16, wkvt16, *tables_s, PROJ_ROWS, n_tok_s)
        oa_s, ob_s = _decode_attention(
            page_table, jnp.pad(q_s.reshape(n_dec, nq, 2 * gw), ((0, 0), (0, OUT_ROWS - nq), (0, 0))),
            new_rows(kt_s), new_rows(vt_s),
            cache_k[l].transpose(0, 2, 3, 1), cache_v[l].transpose(0, 2, 3, 1), nq)
        outs[2].append(to_positions(kt_s, 1, n_tok_s).reshape(n_dec, nq, heads, HEAD_DIM))
        outs[3].append(to_positions(vt_s, 1, n_tok_s).reshape(n_dec, nq, heads, HEAD_DIM))
        merge = functools.partial(_merge, g_a=row2(g_moba[l]), g_b=row2(g_sb[l]), w16=w_out16,
                                  ln_g=row2(ln1_g[l]), ln_b=row2(ln1_b[l]), alpha=alpha)
        xp = merge(oa_p, ob_p, xp)
        xs = merge(oa_s[:, :nq].reshape(n_tok_s, gw), ob_s[:, :nq].reshape(n_tok_s, gw), xs)
        mk, mv, mk16, mv16 = _mem_kv(mem_prompt.reshape(n_seq * n_mem, d), w_mkv[l].astype(BF16))
        outs[4].append(mk.reshape(n_seq, n_mem, MEM_HEADS, d // MEM_HEADS))
        outs[5].append(mv.reshape(n_seq, n_mem, MEM_HEADS, d // MEM_HEADS))
        wmq16 = w_mq[l].astype(BF16)
        wmo16 = w_mo[l].astype(BF16)
        tok = _mem_attend_prompt(xp, wmq16, mk16, mv16, wmo16, row2(ln2_g[l]), row2(ln2_b[l]), n_seq, alpha,
                                 n_tok_p + n_tok_s)
        tok = _mem_attend_sample(xs, wmq16, cache_mem_k[l].reshape(n_dec, n_mem, d),
                                 cache_mem_v[l].reshape(n_dec, n_mem, d), wmo16,
                                 row2(ln2_g[l]), row2(ln2_b[l]), nq, alpha, tok)
        xp, xs = _moe(tok, w_router[l], b_router[l], w_gu[l], b_gu[l], w_down[l],
                      b_down[l], row2(ln3_g[l]), row2(ln3_b[l]), alpha, n_tok_p)
    return (xp.reshape(n_seq, seq_len, d), xs.reshape(n_dec, nq, d)) + tuple(jnp.stack(o) for o in outs)
```
